```python
import math
import jax, jax.numpy as jnp
from jax import lax
import numpy as np

D_MODEL = 1024
BATCH = 8
SEQ = 8192
DEPTH = 2

HEAD_DIM = 64
ROPE_THETA = 500000.0
ROT_DIM = HEAD_DIM // 4
Q_BLOCK = 128
NEG_INF = -1e30

DIFF_HEADS = 4
DIFF_QK_DIM = HEAD_DIM
DIFF_V_DIM = 2 * HEAD_DIM
MLA_HEADS = 8
MLA_Q_RANK = 384
MLA_KV_RANK = 256
MLA_NOPE_DIM = 64
MLA_ROPE_DIM = 32
MLA_V_DIM = 64
DIL_PATTERNS = ((128, 1), (512, 4), (2048, 16))
DIL_GROUPS = len(DIL_PATTERNS)
DIL_HEADS = 4
DIL_QK_DIM = HEAD_DIM
DIL_V_DIM = 2 * HEAD_DIM
BRANCH_A = DIFF_HEADS * DIFF_V_DIM
BRANCH_B = MLA_HEADS * MLA_V_DIM
BRANCH_C = DIL_HEADS * DIL_V_DIM
N_BRANCHES = 3
IN_SIZES = (DIFF_HEADS * 2 * DIFF_QK_DIM, DIFF_HEADS * 2 * DIFF_QK_DIM, DIFF_HEADS * DIFF_V_DIM,
            MLA_Q_RANK, MLA_KV_RANK, MLA_ROPE_DIM,
            DIL_GROUPS * DIL_HEADS * DIL_QK_DIM, DIL_GROUPS * DIL_HEADS * DIL_QK_DIM,
            DIL_GROUPS * DIL_HEADS * DIL_V_DIM,
            N_BRANCHES * D_MODEL)
IN_SPLITS = tuple(int(v) for v in np.cumsum(IN_SIZES)[:-1])
IN_WIDTH = int(sum(IN_SIZES))
N_EXPERTS = 16
N_EXPERT_GROUPS = 4
EXPERTS_PER_GROUP = N_EXPERTS // N_EXPERT_GROUPS
TOPK_GROUPS = 1
TOP_K = 2
EXPERT_HIDDEN = 256
SHARED_HIDDEN = 256
DEEPNORM_ALPHA = (2 * DEPTH) ** 0.25
DEEPNORM_BETA = (8 * DEPTH) ** -0.25

kernel_name = 'hybrid_diff_mla_dilated_moe_encoder'

f32 = jnp.float32


def layer_norm(x, g, b, eps=1e-5):
    xf = x.astype(f32)
    mu = xf.mean(-1, keepdims=True)
    var = jnp.square(xf - mu).mean(-1, keepdims=True)
    return ((xf - mu) * lax.rsqrt(var + eps) * g.astype(f32) + b.astype(f32)).astype(x.dtype)


def rms_norm(x, g, eps=1e-6):
    xf = x.astype(f32)
    return (xf * lax.rsqrt(jnp.mean(xf * xf, -1, keepdims=True) + eps) * g.astype(f32)).astype(x.dtype)


def rope_tables(positions, dim):
    inv_freq = ROPE_THETA ** (-jnp.arange(0, dim, 2, dtype=f32) / dim)
    ang = positions.astype(f32)[..., None] * inv_freq
    return jnp.cos(ang)[:, :, None, :], jnp.sin(ang)[:, :, None, :]


def apply_rope(x, cos, sin):
    x1, x2 = jnp.split(x.astype(f32), 2, axis=-1)
    return jnp.concatenate([x1 * cos - x2 * sin, x2 * cos + x1 * sin], axis=-1).astype(x.dtype)


def partial_rope(x, cos, sin):
    r = 2 * cos.shape[-1]
    return jnp.concatenate([apply_rope(x[..., :r], cos, sin), x[..., r:]], axis=-1)


def _to_blocks(t):
    b, s = t.shape[:2]
    return jnp.swapaxes(t.reshape(b, s // Q_BLOCK, Q_BLOCK, *t.shape[2:]), 0, 1)


def _from_blocks(t):
    t = jnp.swapaxes(t, 0, 1)
    return t.reshape(t.shape[0], -1, *t.shape[3:])


def diff_attention(q, k, v, lam):
    scale = q.shape[-1] ** -0.5

    def block(qb):
        s = jnp.einsum('bqhmd,bkhmd->bhmqk', qb, k).astype(f32) * scale
        p = jax.nn.softmax(s, axis=-1)
        a = p[:, :, 0] - lam * p[:, :, 1]
        return jnp.einsum('bhqk,bkhe->bqhe', a.astype(v.dtype), v)

    return _from_blocks(lax.map(block, _to_blocks(q)))


def mla_attention(q_nope, q_rope, k_nope, k_rope, v):
    scale = (q_nope.shape[-1] + q_rope.shape[-1]) ** -0.5

    def block(qs):
        qn, qr = qs
        s = (jnp.einsum('bqhd,bkhd->bhqk', qn, k_nope)
             + jnp.einsum('bqhr,bkr->bhqk', qr, k_rope)).astype(f32) * scale
        p = jax.nn.softmax(s, axis=-1)
        return jnp.einsum('bhqk,bkhe->bqhe', p.astype(v.dtype), v)

    return _from_blocks(lax.map(block, (_to_blocks(q_nope), _to_blocks(q_rope))))


def _to_sub(t, d):
    b, s = t.shape[:2]
    rest = t.shape[2:]
    return jnp.swapaxes(t.reshape(b, s // d, d, *rest), 1, 2).reshape(b * d, s // d, *rest)


def _from_sub(t, b, d):
    l = t.shape[1]
    rest = t.shape[2:]
    return jnp.swapaxes(t.reshape(b, d, l, *rest), 1, 2).reshape(b, l * d, *rest)


def dilated_window_attention(q, k, v, dilation, n_side):
    b, s, h, dk = q.shape
    L = s // dilation
    nb = -(-L // n_side)
    lp = nb * n_side
    bd = b * dilation
    qs, ks, vs = (_to_sub(t, dilation) for t in (q, k, v))
    qs = jnp.pad(qs, ((0, 0), (0, lp - L), (0, 0), (0, 0))).reshape(bd, nb, n_side, h, dk)

    def band(t):
        tp = jnp.pad(t, ((0, 0), (n_side, lp - L + n_side), (0, 0), (0, 0)))
        tp = tp.reshape(bd, nb + 2, n_side, h, t.shape[-1])
        return jnp.concatenate([tp[:, :-2], tp[:, 1:-1], tp[:, 2:]], axis=2)

    kb, vb = band(ks), band(vs)
    scores = jnp.einsum('bnqhd,bnkhd->bnhqk', qs, kb).astype(f32) * dk ** -0.5
    qi = jnp.arange(n_side)[:, None]
    kc = jnp.arange(3 * n_side)[None, :]
    in_band = jnp.abs(kc - n_side - qi) <= n_side
    kpos = jnp.arange(nb)[:, None] * n_side - n_side + kc
    valid = (kpos >= 0) & (kpos < L)
    mask = in_band[None] & valid[:, None]
    scores = jnp.where(mask[None, :, None], scores, NEG_INF)
    lse = jax.nn.logsumexp(scores, axis=-1)
    p = jnp.exp(scores - lse[..., None])
    o = jnp.einsum('bnhqk,bnkhe->bnqhe', p.astype(v.dtype), vb).reshape(bd, lp, h, -1)[:, :L]
    lse = jnp.swapaxes(lse, 2, 3).reshape(bd, lp, h)[:, :L]
    return _from_sub(o, b, dilation), _from_sub(lse, b, dilation)


def hybrid_mixer(x, layer, cos_p, sin_p, cos_m, sin_m, w_in, b_gate, lam_q1, lam_k1, lam_q2, lam_k2,
                 diff_norm_g, q_norm_g, kv_norm_g, w_qb, w_kvb, w_br_a, w_br_b, w_br_c, w_out):
    B, S, D = x.shape
    h = x @ w_in
    a_q, a_k, a_v, b_cq, b_ckv, b_kr, c_q, c_k, c_v, g = jnp.split(h, IN_SPLITS, axis=-1)

    aq = partial_rope(a_q.reshape(B, S, DIFF_HEADS * 2, DIFF_QK_DIM), cos_p, sin_p)
    ak = partial_rope(a_k.reshape(B, S, DIFF_HEADS * 2, DIFF_QK_DIM), cos_p, sin_p)
    aq = aq.reshape(B, S, DIFF_HEADS, 2, DIFF_QK_DIM)
    ak = ak.reshape(B, S, DIFF_HEADS, 2, DIFF_QK_DIM)
    av = a_v.reshape(B, S, DIFF_HEADS, DIFF_V_DIM)
    lam_init = 0.8 - 0.6 * math.exp(-0.3 * layer)
    lam = (jnp.exp(jnp.sum(lam_q1.astype(f32) * lam_k1.astype(f32)))
           - jnp.exp(jnp.sum(lam_q2.astype(f32) * lam_k2.astype(f32))) + lam_init)
    out_a = diff_attention(aq, ak, av, lam)
    out_a = (rms_norm(out_a, diff_norm_g) * (1.0 - lam_init)).reshape(B, S, BRANCH_A)

    q_full = (rms_norm(b_cq, q_norm_g) @ w_qb).reshape(B, S, MLA_HEADS, MLA_NOPE_DIM + MLA_ROPE_DIM)
    q_nope, q_rope = q_full[..., :MLA_NOPE_DIM], apply_rope(q_full[..., MLA_NOPE_DIM:], cos_m, sin_m)
    kv = (rms_norm(b_ckv, kv_norm_g) @ w_kvb).reshape(B, S, MLA_HEADS, MLA_NOPE_DIM + MLA_V_DIM)
    k_nope, v_b = kv[..., :MLA_NOPE_DIM], kv[..., MLA_NOPE_DIM:]
    k_rope = apply_rope(b_kr[:, :, None, :], cos_m, sin_m)[:, :, 0]
    out_b = mla_attention(q_nope, q_rope, k_nope, k_rope, v_b).reshape(B, S, BRANCH_B)

    cq = partial_rope(c_q.reshape(B, S, DIL_GROUPS * DIL_HEADS, DIL_QK_DIM), cos_p, sin_p)
    ck = partial_rope(c_k.reshape(B, S, DIL_GROUPS * DIL_HEADS, DIL_QK_DIM), cos_p, sin_p)
    cq = cq.reshape(B, S, DIL_GROUPS, DIL_HEADS, DIL_QK_DIM)
    ck = ck.reshape(B, S, DIL_GROUPS, DIL_HEADS, DIL_QK_DIM)
    cv = c_v.reshape(B, S, DIL_GROUPS, DIL_HEADS, DIL_V_DIM)
    outs, lses = [], []
    for gi, (window, dilation) in enumerate(DIL_PATTERNS):
        o, lse = dilated_window_attention(cq[:, :, gi], ck[:, :, gi], cv[:, :, gi],
                                          dilation, (window // 2) // dilation)
        outs.append(o)
        lses.append(lse)
    w_grp = jax.nn.softmax(jnp.stack(lses, axis=-1), axis=-1)
    out_c = jnp.einsum('bshg,bshge->bshe', w_grp, jnp.stack(outs, axis=3).astype(f32))
    out_c = out_c.astype(x.dtype).reshape(B, S, BRANCH_C)

    gates = jax.nn.sigmoid((g + b_gate).astype(f32)).astype(x.dtype).reshape(B, S, N_BRANCHES, D)
    y = (gates[:, :, 0] * (out_a @ w_br_a) + gates[:, :, 1] * (out_b @ w_br_b)
         + gates[:, :, 2] * (out_c @ w_br_c))
    return y @ w_out


def moe_ffn(x, router_w, router_bias, w_g, w_u, w_d, w_sg, w_su, w_sd):
    B, S, D = x.shape
    xt = x.reshape(-1, D)
    scores = jax.nn.sigmoid((xt @ router_w).astype(f32))
    biased = scores + router_bias.astype(f32)
    grp = biased.reshape(-1, N_EXPERT_GROUPS, EXPERTS_PER_GROUP)
    grp_score = lax.top_k(grp, 2)[0].sum(-1)
    gsel = lax.top_k(grp_score, TOPK_GROUPS)[1]
    gmask = jax.nn.one_hot(gsel, N_EXPERT_GROUPS, dtype=f32).sum(1) > 0
    emask = jnp.repeat(gmask, EXPERTS_PER_GROUP, axis=-1)
    eidx = lax.top_k(jnp.where(emask, biased, -jnp.inf), TOP_K)[1]
    wsel = jnp.take_along_axis(scores, eidx, axis=-1)
    wsel = wsel / wsel.sum(-1, keepdims=True)
    gate = (jax.nn.one_hot(eidx, N_EXPERTS, dtype=f32) * wsel[..., None]).sum(1)
    routed = jnp.zeros(xt.shape, f32)
    for e in range(N_EXPERTS):
        hid = jax.nn.silu(xt @ w_g[e]) * (xt @ w_u[e])
        routed = routed + gate[:, e:e + 1] * (hid @ w_d[e]).astype(f32)
    shared = (jax.nn.silu(xt @ w_sg) * (xt @ w_su)) @ w_sd
    return (routed.astype(x.dtype) + shared).reshape(B, S, D)


def setup_inputs(seed: int = 0) -> dict:
    key = jax.random.key(seed)
    ks = iter(jax.random.split(key, 40))

    def nrm(shape, scale):
        return scale * jax.random.normal(next(ks), shape, f32)

    def gain(shape):
        return 1.0 + nrm(shape, 0.02)

    L, D = DEPTH, D_MODEL
    x = jax.random.normal(next(ks), (BATCH, SEQ, D), f32)
    offset = jax.random.randint(next(ks), (BATCH, 1), 0, 65536)
    positions = (offset + jnp.arange(SEQ)[None, :]).astype(jnp.int32)
    return {
        'x': x,
        'positions': positions,
        'ln_in_g': gain((D,)),
        'ln_in_b': nrm((D,), 0.02),
        'w_in': nrm((L, D, IN_WIDTH), D ** -0.5),
        'b_gate': nrm((L, N_BRANCHES * D), 0.1),
        'lam_q1': nrm((L, DIFF_QK_DIM), 0.1),
        'lam_k1': nrm((L, DIFF_QK_DIM), 0.1),
        'lam_q2': nrm((L, DIFF_QK_DIM), 0.1),
        'lam_k2': nrm((L, DIFF_QK_DIM), 0.1),
        'diff_norm_g': gain((L, DIFF_V_DIM)),
        'mla_q_norm_g': gain((L, MLA_Q_RANK)),
        'mla_kv_norm_g': gain((L, MLA_KV_RANK)),
        'w_mla_qb': nrm((L, MLA_Q_RANK, MLA_HEADS * (MLA_NOPE_DIM + MLA_ROPE_DIM)), MLA_Q_RANK ** -0.5),
        'w_mla_kvb': nrm((L, MLA_KV_RANK, MLA_HEADS * (MLA_NOPE_DIM + MLA_V_DIM)), MLA_KV_RANK ** -0.5),
        'w_branch_a': nrm((L, BRANCH_A, D), BRANCH_A ** -0.5),
        'w_branch_b': nrm((L, BRANCH_B, D), BRANCH_B ** -0.5),
        'w_branch_c': nrm((L, BRANCH_C, D), BRANCH_C ** -0.5),
        'w_out': nrm((L, D, D), D ** -0.5 * DEEPNORM_BETA),
        'ln1_g': gain((L, D)),
        'ln1_b': nrm((L, D), 0.02),
        'router_w': nrm((D, N_EXPERTS), D ** -0.5),
        'router_bias': nrm((N_EXPERTS,), 0.01),
        'w_exp_gate': nrm((L, N_EXPERTS, D, EXPERT_HIDDEN), D ** -0.5),
        'w_exp_up': nrm((L, N_EXPERTS, D, EXPERT_HIDDEN), D ** -0.5),
        'w_exp_down': nrm((L, N_EXPERTS, EXPERT_HIDDEN, D), EXPERT_HIDDEN ** -0.5 * DEEPNORM_BETA),
        'w_sh_gate': nrm((L, D, SHARED_HIDDEN), D ** -0.5),
        'w_sh_up': nrm((L, D, SHARED_HIDDEN), D ** -0.5),
        'w_sh_down': nrm((L, SHARED_HIDDEN, D), SHARED_HIDDEN ** -0.5 * DEEPNORM_BETA),
        'ln2_g': gain((L, D)),
        'ln2_b': nrm((L, D), 0.02),
    }


def reference(x, positions, ln_in_g, ln_in_b, w_in, b_gate, lam_q1, lam_k1, lam_q2, lam_k2,
              diff_norm_g, mla_q_norm_g, mla_kv_norm_g, w_mla_qb, w_mla_kvb, w_branch_a, w_branch_b,
              w_branch_c, w_out, ln1_g, ln1_b, router_w, router_bias, w_exp_gate, w_exp_up,
              w_exp_down, w_sh_gate, w_sh_up, w_sh_down, ln2_g, ln2_b):
    cos_p, sin_p = rope_tables(positions, ROT_DIM)
    cos_m, sin_m = rope_tables(positions, MLA_ROPE_DIM)
    x = layer_norm(x, ln_in_g, ln_in_b)
    for l in range(DEPTH):
        mix = hybrid_mixer(x, l, cos_p, sin_p, cos_m, sin_m, w_in[l], b_gate[l],
                           lam_q1[l], lam_k1[l], lam_q2[l], lam_k2[l], diff_norm_g[l],
                           mla_q_norm_g[l], mla_kv_norm_g[l], w_mla_qb[l], w_mla_kvb[l],
                           w_branch_a[l], w_branch_b[l], w_branch_c[l], w_out[l])
        x = layer_norm(DEEPNORM_ALPHA * x + mix, ln1_g[l], ln1_b[l])
        ffn = moe_ffn(x, router_w, router_bias, w_exp_gate[l], w_exp_up[l], w_exp_down[l],
                      w_sh_gate[l], w_sh_up[l], w_sh_down[l])
        x = layer_norm(DEEPNORM_ALPHA * x + ffn, ln2_g[l], ln2_b[l])
    return x
```

```python
import functools
import math

import jax
import jax.numpy as jnp
from jax import lax
from jax.experimental import pallas as pl
from jax.experimental.pallas import tpu as pltpu

f32 = jnp.float32
bf16 = jnp.bfloat16

D_MODEL = 1024
HEAD_DIM = 64
ROPE_THETA = 500000.0
ROT_DIM = HEAD_DIM // 4
DIFF_HEADS = 4
MLA_HEADS = 8
MLA_Q_RANK = 384
MLA_KV_RANK = 256
MLA_NOPE_DIM = 64
MLA_ROPE_DIM = 32
MLA_V_DIM = 64
DIL_PATTERNS = ((128, 1), (512, 4), (2048, 16))
DIL_GROUPS = 3
DIL_HEADS = 4
DIL_SIDE = 64
N_EXPERTS = 16
N_EXPERT_GROUPS = 4
EXPERTS_PER_GROUP = 4
EXPERT_HIDDEN = 256
IN_SIZES = (512, 512, 512, 384, 256, 32, 768, 768, 1536, 3072)
NEG_INF = -1e30
LOG2E = math.log2(math.e)

LANES = 128
VMEM_LIMIT = 52 * 1024 * 1024

HA_W = 1536
HB_W = 768
HC_W = 3072
HG_W = 3072


def _cparams(sem):
    return pltpu.CompilerParams(dimension_semantics=sem, vmem_limit_bytes=VMEM_LIMIT)


def _layer_norm(z, g, b):
    mu = jnp.mean(z, axis=-1, keepdims=True)
    zc = z - mu
    var = jnp.mean(zc * zc, axis=-1, keepdims=True)
    return zc * lax.rsqrt(var + 1e-5) * g + b


def _rms_norm(z, g):
    return z * lax.rsqrt(jnp.mean(z * z, axis=-1, keepdims=True) + 1e-6) * g


def _dot(a, b):
    return jnp.dot(a, b, preferred_element_type=f32)


def _dot_nt(a, b):
    return lax.dot_general(a, b, (((1,), (1,)), ((), ())), preferred_element_type=f32)


def _rope_tile(xt, cos_t, sin_t, first_half, shift):
    xr = jnp.where(first_half, pltpu.roll(xt, LANES - shift, 1), pltpu.roll(xt, shift, 1))
    return xt * cos_t + xr * sin_t


def _ln_kernel(x_ref, g_ref, b_ref, o32_ref, ob_ref):
    y = _layer_norm(x_ref[...], g_ref[...], b_ref[...])
    o32_ref[...] = y
    ob_ref[...] = y.astype(bf16)


def _ln_call(x, g, b, tm=512):
    T, D = x.shape
    row = pl.BlockSpec((tm, D), lambda i: (i, 0))
    vec = pl.BlockSpec((1, D), lambda i: (0, 0))
    return pl.pallas_call(
        _ln_kernel, grid=(T // tm,), in_specs=[row, vec, vec], out_specs=[row, row],
        out_shape=[jax.ShapeDtypeStruct((T, D), f32), jax.ShapeDtypeStruct((T, D), bf16)],
        compiler_params=_cparams(("parallel",)), name="ln_in")(x, g, b)


def _inproj_plan():
    qscale = HEAD_DIM ** -0.5 * LOG2E
    plan = []
    for c in range(0, 512, 256):
        plan.append((c, 256, "rope", 0, c, qscale))
    for c in range(512, 1024, 256):
        plan.append((c, 256, "rope", 0, c, 1.0))
    plan.append((1024, 512, "plain", 0, 1024, 1.0))
    plan.append((HA_W, HB_W, "plain", 1, 0, 1.0))
    base = HA_W + HB_W
    for g in range(DIL_GROUPS):
        o = 1024 * g
        plan.append((base + o, 256, "rope", 2, o, qscale))
        plan.append((base + o + 256, 256, "rope", 2, o + 256, 1.0))
        plan.append((base + o + 512, 512, "plain", 2, o + 512, 1.0))
    base = HA_W + HB_W + HC_W
    for c in range(0, HG_W, 512):
        plan.append((base + c, 512, "gate", 3, c, 1.0))
    return plan


def _inproj_kernel(x_ref, w_ref, bg_ref, cp_ref, sp_ref, ha_ref, hb_ref, hc_ref, hg_ref):
    outs = (ha_ref, hb_ref, hc_ref, hg_ref)
    x = x_ref[...]
    cos_t = cp_ref[...]
    sin_t = sp_ref[...]
    lane = lax.broadcasted_iota(jnp.int32, cos_t.shape, 1)
    first_half = (lane % HEAD_DIM) < (ROT_DIM // 2)
    for (c0, w, kind, oi, o0, scale) in _inproj_plan():
        acc = _dot(x, w_ref[:, c0:c0 + w])
        if kind == "rope":
            tiles = []
            for t in range(w // LANES):
                y = _rope_tile(acc[:, t * LANES:(t + 1) * LANES], cos_t, sin_t, first_half, ROT_DIM // 2)
                tiles.append(y * scale if scale != 1.0 else y)
            acc = jnp.concatenate(tiles, axis=1)
        elif kind == "gate":
            acc = jax.nn.sigmoid(acc + bg_ref[:, o0:o0 + w])
        outs[oi][:, o0:o0 + w] = acc.astype(outs[oi].dtype)


def _inproj_call(xb, w, bg, cos_t, sin_t, tm=256):
    T, D = xb.shape
    NW = w.shape[1]
    row = lambda width: pl.BlockSpec((tm, width), lambda i: (i, 0))
    return pl.pallas_call(
        _inproj_kernel, grid=(T // tm,),
        in_specs=[row(D), pl.BlockSpec((D, NW), lambda i: (0, 0)), pl.BlockSpec((1, HG_W), lambda i: (0, 0)),
                  row(LANES), row(LANES)],
        out_specs=[row(HA_W), row(HB_W), row(HC_W), row(HG_W)],
        out_shape=[jax.ShapeDtypeStruct((T, HA_W), bf16), jax.ShapeDtypeStruct((T, HB_W), f32),
                   jax.ShapeDtypeStruct((T, HC_W), bf16), jax.ShapeDtypeStruct((T, HG_W), bf16)],
        compiler_params=_cparams(("parallel",)), name="inproj")(xb, w, bg, cos_t, sin_t)


def _flash_kernel(*refs, n_maps, tq, tk, n_kv, lam_init):
    if n_maps == 2:
        q_ref, k_ref, v_ref, lam_ref, g_ref, o_ref, m_sc, l_sc, acc_sc = refs
    else:
        q_ref, k_ref, v_ref, o_ref, m_sc, l_sc, acc_sc = refs
    q = q_ref[...]
    if n_maps == 2:
        lane = lax.broadcasted_iota(jnp.int32, q.shape, 1)
        zero = jnp.zeros_like(q)
        qq = jnp.concatenate([jnp.where(lane < HEAD_DIM, q, zero), jnp.where(lane >= HEAD_DIM, q, zero)], axis=0)
    else:
        qq = q
    m_sc[...] = jnp.full(m_sc.shape, -jnp.inf, f32)
    l_sc[...] = jnp.zeros(l_sc.shape, f32)
    acc_sc[...] = jnp.zeros(acc_sc.shape, f32)

    def body(j, carry):
        off = pl.multiple_of(j * tk, tk)
        kc = k_ref[pl.ds(off, tk), :]
        vc = v_ref[pl.ds(off, tk), :]
        s = _dot_nt(qq, kc)
        m_prev = m_sc[...]
        m_new = jnp.maximum(m_prev, jnp.max(s, axis=-1, keepdims=True))
        p = jnp.exp2(s - m_new)
        a = jnp.exp2(m_prev - m_new)
        l_sc[...] = a * l_sc[...] + jnp.sum(p, axis=-1, keepdims=True)
        acc_sc[...] = a * acc_sc[...] + _dot(p.astype(bf16), vc)
        m_sc[...] = m_new
        return carry

    lax.fori_loop(0, n_kv, body, 0)
    o = acc_sc[...] / l_sc[...]
    if n_maps == 2:
        lv = lam_ref[...]
        lam = (jnp.exp(jnp.sum(lv[0:1] * lv[1:2], axis=-1, keepdims=True))
               - jnp.exp(jnp.sum(lv[2:3] * lv[3:4], axis=-1, keepdims=True)) + lam_init)
        o = o[:tq] - lam * o[tq:]
        o = _rms_norm(o, g_ref[...]) * (1.0 - lam_init)
    o_ref[...] = o.astype(o_ref.dtype)


def _flash_call(q_arr, k_arr, v_arr, n_heads, q_blk0, k_blk0, v_blk0, n_maps, tq, tk, name,
                lam_vec=None, norm_g=None, lam_init=0.0):
    B, S, _ = q_arr.shape
    rows = n_maps * tq
    in_specs = [pl.BlockSpec((None, tq, LANES), lambda b, h, i: (b, i, q_blk0 + h)),
                pl.BlockSpec((None, S, LANES), lambda b, h, i: (b, 0, k_blk0 + h)),
                pl.BlockSpec((None, S, LANES), lambda b, h, i: (b, 0, v_blk0 + h))]
    args = [q_arr, k_arr, v_arr]
    if n_maps == 2:
        in_specs += [pl.BlockSpec((8, LANES), lambda b, h, i: (0, 0)),
                     pl.BlockSpec((1, LANES), lambda b, h, i: (0, 0))]
        args += [lam_vec, norm_g]
    kern = functools.partial(_flash_kernel, n_maps=n_maps, tq=tq, tk=tk, n_kv=S // tk, lam_init=lam_init)
    return pl.pallas_call(
        kern, grid=(B, n_heads, S // tq), in_specs=in_specs,
        out_specs=pl.BlockSpec((None, tq, LANES), lambda b, h, i: (b, i, h)),
        out_shape=jax.ShapeDtypeStruct((B, S, n_heads * LANES), bf16),
        scratch_shapes=[pltpu.VMEM((rows, 1), f32), pltpu.VMEM((rows, 1), f32), pltpu.VMEM((rows, LANES), f32)],
        compiler_params=_cparams(("parallel", "parallel", "arbitrary")), name=name)(*args)


def _mla_q_kernel(cq_ref, g_ref, w_ref, cm_ref, sm_ref, o_ref, *, scale):
    qn = _rms_norm(cq_ref[...], g_ref[...]).astype(bf16)
    cos_t = cm_ref[...]
    sin_t = sm_ref[...]
    lane = lax.broadcasted_iota(jnp.int32, cos_t.shape, 1)
    first_half = lane < MLA_NOPE_DIM + MLA_ROPE_DIM // 2
    for h in range(MLA_HEADS):
        acc = _dot(qn, w_ref[:, h * LANES:(h + 1) * LANES])
        y = _rope_tile(acc, cos_t, sin_t, first_half, MLA_ROPE_DIM // 2) * scale
        o_ref[:, h * LANES:(h + 1) * LANES] = y.astype(o_ref.dtype)


def _mla_kv_kernel(ckv_ref, kr_ref, g_ref, wk_ref, wv_ref, cm_ref, sm_ref, k_ref, v_ref):
    cn = _rms_norm(ckv_ref[...], g_ref[...]).astype(bf16)
    cos_t = cm_ref[...]
    lane = lax.broadcasted_iota(jnp.int32, cos_t.shape, 1)
    first_half = lane < MLA_NOPE_DIM + MLA_ROPE_DIM // 2
    kr = _rope_tile(kr_ref[...], cos_t, sm_ref[...], first_half, MLA_ROPE_DIM // 2)
    for h in range(MLA_HEADS):
        sl = slice(h * LANES, (h + 1) * LANES)
        k_ref[:, sl] = (_dot(cn, wk_ref[:, sl]) + kr).astype(k_ref.dtype)
        v_ref[:, sl] = _dot(cn, wv_ref[:, sl]).astype(v_ref.dtype)


def _mla_proj_calls(hb, qg, kvg, wq, wk, wv, cos_m, sin_m, tm=512):
    T = hb.shape[0]
    W = MLA_HEADS * LANES
    row = lambda width, blk=0: pl.BlockSpec((tm, width), lambda i: (i, blk))
    full = lambda a: pl.BlockSpec(a.shape, lambda i: (0, 0))
    scale = (MLA_NOPE_DIM + MLA_ROPE_DIM) ** -0.5 * LOG2E
    qb = pl.pallas_call(
        functools.partial(_mla_q_kernel, scale=scale), grid=(T // tm,),
        in_specs=[row(MLA_Q_RANK, 1), full(qg), full(wq), row(LANES), row(LANES)],
        out_specs=row(W), out_shape=jax.ShapeDtypeStruct((T, W), bf16),
        compiler_params=_cparams(("parallel",)), name="mla_q")(hb, qg, wq, cos_m, sin_m)
    kb, vb = pl.pallas_call(
        _mla_kv_kernel, grid=(T // tm,),
        in_specs=[row(MLA_KV_RANK, 0), row(LANES, 2), full(kvg), full(wk), full(wv), row(LANES), row(LANES)],
        out_specs=[row(W), row(W)],
        out_shape=[jax.ShapeDtypeStruct((T, W), bf16), jax.ShapeDtypeStruct((T, W), bf16)],
        compiler_params=_cparams(("parallel",)), name="mla_kv")(hb, hb, kvg, wk, wv, cos_m, sin_m)
    return qb, kb, vb


def _dil_kernel(q_ref, kp_ref, kc_ref, kn_ref, vp_ref, vc_ref, vn_ref, o_ref, lse_ref, kcat, vcat, *, tq, sub_len):
    i = pl.program_id(2)
    side = DIL_SIDE
    kcat[0:side, :] = kp_ref[...]
    kcat[side:side + tq, :] = kc_ref[...]
    kcat[side + tq:, :] = kn_ref[...]
    vcat[0:side, :] = vp_ref[...]
    vcat[side:side + tq, :] = vc_ref[...]
    vcat[side + tq:, :] = vn_ref[...]
    qs, ks = 2 * side, 4 * side
    lane = lax.broadcasted_iota(jnp.int32, (qs, LANES), 1)
    ii = lax.broadcasted_iota(jnp.int32, (qs, ks), 0)
    jj = lax.broadcasted_iota(jnp.int32, (qs, ks), 1)
    band = (jj - ii >= 0) & (jj - ii <= 2 * side)
    for sb in range(tq // qs):
        q0 = sb * qs
        pos = jj + (i * tq + q0 - side)
        mask = band & (pos >= 0) & (pos < sub_len)
        lse_tile = jnp.zeros((qs, LANES), f32)
        for h in range(DIL_HEADS):
            t = h // 2
            qt = q_ref[q0:q0 + qs, t * LANES:(t + 1) * LANES]
            keep = (lane < HEAD_DIM) if h % 2 == 0 else (lane >= HEAD_DIM)
            qm = jnp.where(keep, qt, jnp.zeros_like(qt))
            s = _dot_nt(qm, kcat[q0:q0 + ks, t * LANES:(t + 1) * LANES])
            s = jnp.where(mask, s, NEG_INF)
            m = jnp.max(s, axis=-1, keepdims=True)
            p = jnp.exp2(s - m)
            l = jnp.sum(p, axis=-1, keepdims=True)
            o = _dot(p.astype(bf16), vcat[q0:q0 + ks, h * LANES:(h + 1) * LANES]) / l
            o_ref[q0:q0 + qs, h * LANES:(h + 1) * LANES] = o.astype(o_ref.dtype)
            lse_tile = jnp.where(lane == h, m + jnp.log2(l), lse_tile)
        lse_ref[q0:q0 + qs, :] = lse_tile


def _dil_call(hc, B, S, g, dilation):
    d = dilation
    L = S // d
    tq = min(512, L)
    side = DIL_SIDE
    hcv = hc.reshape(B, L, d * HC_W)
    nblk = tq // side
    last = L // side - 1
    qw, vw = 2 * LANES, 4 * LANES
    qcol = lambda r: (r * HC_W + 1024 * g) // qw
    vcol = lambda r: (r * HC_W + 1024 * g + 512) // vw
    in_specs = [
        pl.BlockSpec((None, tq, qw), lambda b, r, i: (b, i, qcol(r))),
        pl.BlockSpec((None, side, qw), lambda b, r, i: (b, jnp.maximum(i * nblk - 1, 0), qcol(r) + 1)),
        pl.BlockSpec((None, tq, qw), lambda b, r, i: (b, i, qcol(r) + 1)),
        pl.BlockSpec((None, side, qw), lambda b, r, i: (b, jnp.minimum((i + 1) * nblk, last), qcol(r) + 1)),
        pl.BlockSpec((None, side, vw), lambda b, r, i: (b, jnp.maximum(i * nblk - 1, 0), vcol(r))),
        pl.BlockSpec((None, tq, vw), lambda b, r, i: (b, i, vcol(r))),
        pl.BlockSpec((None, side, vw), lambda b, r, i: (b, jnp.minimum((i + 1) * nblk, last), vcol(r))),
    ]
    o, lse = pl.pallas_call(
        functools.partial(_dil_kernel, tq=tq, sub_len=L), grid=(B, d, L // tq), in_specs=in_specs,
        out_specs=[pl.BlockSpec((None, tq, vw), lambda b, r, i: (b, i, r)),
                   pl.BlockSpec((None, tq, LANES), lambda b, r, i: (b, i, r))],
        out_shape=[jax.ShapeDtypeStruct((B, L, d * vw), bf16), jax.ShapeDtypeStruct((B, L, d * LANES), f32)],
        scratch_shapes=[pltpu.VMEM((tq + 2 * side, qw), bf16), pltpu.VMEM((tq + 2 * side, vw), bf16)],
        compiler_params=_cparams(("parallel", "parallel", "arbitrary")),
        name=f"dilated_d{d}")(hcv, hcv, hcv, hcv, hcv, hcv, hcv)
    return o.reshape(B * S, vw), lse.reshape(B * S, LANES)


def _merge_kernel(x_ref, oa_ref, ob_ref, o1_ref, o2_ref, o3_ref, l1_ref, l2_ref, l3_ref, g_ref,
                  wa_ref, wb_ref, wc_ref, wo_ref, lg_ref, lb_ref, x32_ref, xb_ref, *, alpha):
    l1, l2, l3 = l1_ref[...], l2_ref[...], l3_ref[...]
    mx = jnp.maximum(jnp.maximum(l1, l2), l3)
    e1, e2, e3 = jnp.exp2(l1 - mx), jnp.exp2(l2 - mx), jnp.exp2(l3 - mx)
    den = e1 + e2 + e3
    w1, w2, w3 = e1 / den, e2 / den, e3 / den
    tiles = []
    for h in range(DIL_HEADS):
        sl = slice(h * LANES, (h + 1) * LANES)
        oc = (w1[:, h:h + 1] * o1_ref[:, sl].astype(f32) + w2[:, h:h + 1] * o2_ref[:, sl].astype(f32)
              + w3[:, h:h + 1] * o3_ref[:, sl].astype(f32))
        tiles.append(oc.astype(bf16))
    oc = jnp.concatenate(tiles, axis=1)
    D = D_MODEL
    y = (g_ref[:, 0:D].astype(f32) * _dot(oa_ref[...], wa_ref[...])
         + g_ref[:, D:2 * D].astype(f32) * _dot(ob_ref[...], wb_ref[...])
         + g_ref[:, 2 * D:3 * D].astype(f32) * _dot(oc, wc_ref[...]))
    mix = _dot(y.astype(bf16), wo_ref[...])
    out = _layer_norm(alpha * x_ref[...] + mix, lg_ref[...], lb_ref[...])
    x32_ref[...] = out
    xb_ref[...] = out.astype(bf16)


def _merge_call(x32, oa, ob, o1, o2, o3, l1, l2, l3, hg, wa, wb, wc, wo, lg, lb, alpha, tm=256):
    T, D = x32.shape
    row = lambda a: pl.BlockSpec((tm, a.shape[1]), lambda i: (i, 0))
    full = lambda a: pl.BlockSpec(a.shape, lambda i: (0, 0))
    acts = [x32, oa, ob, o1, o2, o3, l1, l2, l3, hg]
    consts = [wa, wb, wc, wo, lg, lb]
    return pl.pallas_call(
        functools.partial(_merge_kernel, alpha=alpha), grid=(T // tm,),
        in_specs=[row(a) for a in acts] + [full(a) for a in consts],
        out_specs=[pl.BlockSpec((tm, D), lambda i: (i, 0))] * 2,
        out_shape=[jax.ShapeDtypeStruct((T, D), f32), jax.ShapeDtypeStruct((T, D), bf16)],
        compiler_params=_cparams(("parallel",)), name="merge")(*acts, *consts)


def _first_true(flags):
    out, seen = [], None
    for f in flags:
        out.append(f if seen is None else f & ~seen)
        seen = f if seen is None else seen | f
    return out


def _router_kernel(x_ref, rw_ref, rb_ref, gate_ref):
    logits = _dot_nt(rw_ref[...], x_ref[...])
    scores = jax.nn.sigmoid(logits)
    biased = scores + rb_ref[...]
    sc = [scores[e:e + 1, :] for e in range(N_EXPERTS)]
    bi = [biased[e:e + 1, :] for e in range(N_EXPERTS)]
    npg = EXPERTS_PER_GROUP
    gscore = []
    for g in range(N_EXPERT_GROUPS):
        v = bi[g * npg:(g + 1) * npg]
        best = None
        for a in range(npg):
            for b in range(a + 1, npg):
                best = v[a] + v[b] if best is None else jnp.maximum(best, v[a] + v[b])
        gscore.append(best)
    gbest = functools.reduce(jnp.maximum, gscore)
    gsel = _first_true([gs == gbest for gs in gscore])
    zero = jnp.zeros_like(gbest)
    bsel = [functools.reduce(lambda x, y: x + y, [jnp.where(gsel[g], bi[g * npg + j], zero)
                                                  for g in range(N_EXPERT_GROUPS)]) for j in range(npg)]
    ssel = [functools.reduce(lambda x, y: x + y, [jnp.where(gsel[g], sc[g * npg + j], zero)
                                                  for g in range(N_EXPERT_GROUPS)]) for j in range(npg)]
    t1 = functools.reduce(jnp.maximum, bsel)
    i1 = _first_true([b == t1 for b in bsel])
    rest = [jnp.where(i1[j], -jnp.inf, bsel[j]) for j in range(npg)]
    t2 = functools.reduce(jnp.maximum, rest)
    i2 = _first_true([rest[j] == t2 for j in range(npg)])
    w1 = functools.reduce(lambda x, y: x + y, [jnp.where(i1[j], ssel[j], zero) for j in range(npg)])
    w2 = functools.reduce(lambda x, y: x + y, [jnp.where(i2[j], ssel[j], zero) for j in range(npg)])
    den = w1 + w2
    local = [jnp.where(i1[j], w1 / den, zero) + jnp.where(i2[j], w2 / den, zero) for j in range(npg)]
    rows = [jnp.where(gsel[e // npg], local[e % npg], zero) for e in range(N_EXPERTS)]
    gate_ref[...] = jnp.concatenate(rows, axis=0)


def _router_call(xb, rw_t, rb, tm=1024):
    T, D = xb.shape
    return pl.pallas_call(
        _router_kernel, grid=(T // tm,),
        in_specs=[pl.BlockSpec((tm, D), lambda i: (i, 0)), pl.BlockSpec((N_EXPERTS, D), lambda i: (0, 0)),
                  pl.BlockSpec((N_EXPERTS, 1), lambda i: (0, 0))],
        out_specs=pl.BlockSpec((N_EXPERTS, tm), lambda i: (0, i)),
        out_shape=jax.ShapeDtypeStruct((N_EXPERTS, T), f32),
        compiler_params=_cparams(("parallel",)), name="router")(xb, rw_t, rb)


def _moe_kernel(x32_ref, xb_ref, gate_ref, wg_ref, wu_ref, wd_ref, wsg_ref, wsu_ref, wsd_ref, lg_ref, lb_ref,
                o32_ref, ob_ref, acc_sc, *, alpha, n_chunks, per_chunk):
    c = pl.program_id(1)
    x = xb_ref[...]

    @pl.when(c == 0)
    def _():
        hs = jax.nn.silu(_dot(x, wsg_ref[...])) * _dot(x, wsu_ref[...])
        acc_sc[...] = _dot(hs.astype(bf16), wsd_ref[...])

    gate = gate_ref[...]
    hid = []
    for e in range(per_chunk):
        sl = slice(e * EXPERT_HIDDEN, (e + 1) * EXPERT_HIDDEN)
        h = jax.nn.silu(_dot(x, wg_ref[:, sl])) * _dot(x, wu_ref[:, sl]) * gate[:, e:e + 1]
        hid.append(h.astype(bf16))
    acc_sc[...] += _dot(jnp.concatenate(hid, axis=1), wd_ref[...])

    @pl.when(c == n_chunks - 1)
    def _():
        out = _layer_norm(alpha * x32_ref[...] + acc_sc[...], lg_ref[...], lb_ref[...])
        o32_ref[...] = out
        ob_ref[...] = out.astype(bf16)


def _moe_call(x32, xb, gate_c, wg, wu, wd, wsg, wsu, wsd, lg, lb, alpha, tm=1024, per_chunk=4):
    T, D = x32.shape
    n_chunks = N_EXPERTS // per_chunk
    cw = per_chunk * EXPERT_HIDDEN
    row = pl.BlockSpec((tm, D), lambda i, c: (i, 0))
    full = lambda a: pl.BlockSpec(a.shape, lambda i, c: (0, 0))
    return pl.pallas_call(
        functools.partial(_moe_kernel, alpha=alpha, n_chunks=n_chunks, per_chunk=per_chunk),
        grid=(T // tm, n_chunks),
        in_specs=[row, row, pl.BlockSpec((None, tm, per_chunk), lambda i, c: (c, i, 0)),
                  pl.BlockSpec((D, cw), lambda i, c: (0, c)), pl.BlockSpec((D, cw), lambda i, c: (0, c)),
                  pl.BlockSpec((cw, D), lambda i, c: (c, 0)),
                  full(wsg), full(wsu), full(wsd), full(lg), full(lb)],
        out_specs=[row, row],
        out_shape=[jax.ShapeDtypeStruct((T, D), f32), jax.ShapeDtypeStruct((T, D), bf16)],
        scratch_shapes=[pltpu.VMEM((tm, D), f32)],
        compiler_params=_cparams(("parallel", "arbitrary")), name="moe")(
            x32, xb, gate_c, wg, wu, wd, wsg, wsu, wsd, lg, lb)


def _rope_tables(positions):
    pos = positions.astype(f32).reshape(-1, 1)
    T = pos.shape[0]
    ones, zeros = jnp.ones, jnp.zeros
    inv_p = ROPE_THETA ** (-jnp.arange(0, ROT_DIM, 2, dtype=f32) / ROT_DIM)
    ang = pos * inv_p
    c, s = jnp.cos(ang), jnp.sin(ang)
    pad = HEAD_DIM - ROT_DIM
    cos_p = jnp.tile(jnp.concatenate([c, c, ones((T, pad), f32)], axis=1), (1, 2))
    sin_p = jnp.tile(jnp.concatenate([-s, s, zeros((T, pad), f32)], axis=1), (1, 2))
    inv_m = ROPE_THETA ** (-jnp.arange(0, MLA_ROPE_DIM, 2, dtype=f32) / MLA_ROPE_DIM)
    ang = pos * inv_m
    c, s = jnp.cos(ang), jnp.sin(ang)
    tail = LANES - MLA_NOPE_DIM - MLA_ROPE_DIM
    cos_m = jnp.concatenate([ones((T, MLA_NOPE_DIM), f32), c, c, ones((T, tail), f32)], axis=1)
    sin_m = jnp.concatenate([zeros((T, MLA_NOPE_DIM), f32), -s, s, zeros((T, tail), f32)], axis=1)
    return cos_p, sin_p, cos_m, sin_m


def _pack_w_in(w):
    offs = [0]
    for sz in IN_SIZES:
        offs.append(offs[-1] + sz)
    a_q, a_k, a_v, b_cq, b_ckv, b_kr, c_q, c_k, c_v, g = [w[:, offs[i]:offs[i + 1]] for i in range(len(IN_SIZES))]
    D = w.shape[0]
    z = lambda n: jnp.zeros((D, n), w.dtype)
    kr_pad = jnp.concatenate([z(MLA_NOPE_DIM), b_kr, z(LANES - MLA_NOPE_DIM - MLA_ROPE_DIM)], axis=1)
    cols = [a_q, a_k, a_v, b_ckv, kr_pad, b_cq]
    for gi in range(DIL_GROUPS):
        cols += [c_q[:, 256 * gi:256 * (gi + 1)], c_k[:, 256 * gi:256 * (gi + 1)], c_v[:, 512 * gi:512 * (gi + 1)]]
    cols.append(g)
    return jnp.concatenate(cols, axis=1).astype(bf16)


def _pad_heads(w, n_heads, lo, hi):
    K = w.shape[0]
    per = w.shape[1] // n_heads
    wh = w.reshape(K, n_heads, per)[:, :, lo:hi]
    wh = jnp.pad(wh, ((0, 0), (0, 0), (0, LANES - (hi - lo))))
    return wh.reshape(K, n_heads * LANES).astype(bf16)


def kernel(x, positions, ln_in_g, ln_in_b, w_in, b_gate, lam_q1, lam_k1, lam_q2, lam_k2, diff_norm_g, mla_q_norm_g, mla_kv_norm_g, w_mla_qb, w_mla_kvb, w_branch_a, w_branch_b, w_branch_c, w_out, ln1_g, ln1_b, router_w, router_bias, w_exp_gate, w_exp_up, w_exp_down, w_sh_gate, w_sh_up, w_sh_down, ln2_g, ln2_b):
    B, S, D = x.shape
    T = B * S
    depth = w_in.shape[0]
    alpha = (2 * depth) ** 0.25
    cos_p, sin_p, cos_m, sin_m = _rope_tables(positions)
    x32, xb = _ln_call(x.reshape(T, D), ln_in_g.reshape(1, D), ln_in_b.reshape(1, D))
    rw_t = router_w.T.astype(bf16)
    rb = router_bias.reshape(N_EXPERTS, 1).astype(f32)
    vec = lambda v: v.reshape(1, -1).astype(f32)
    for l in range(depth):
        lam_init = 0.8 - 0.6 * math.exp(-0.3 * l)
        ha, hb, hc, hg = _inproj_call(xb, _pack_w_in(w_in[l]), vec(b_gate[l]), cos_p, sin_p)
        lam_vec = jnp.pad(jnp.stack([lam_q1[l], lam_k1[l], lam_q2[l], lam_k2[l]]).astype(f32),
                          ((0, 4), (0, LANES - HEAD_DIM)))
        ha3 = ha.reshape(B, S, HA_W)
        oa = _flash_call(ha3, ha3, ha3, DIFF_HEADS, 0, DIFF_HEADS, 2 * DIFF_HEADS, n_maps=2, tq=256, tk=512,
                         name="diff_attn", lam_vec=lam_vec, norm_g=vec(diff_norm_g[l]), lam_init=lam_init)
        per_q = MLA_NOPE_DIM + MLA_ROPE_DIM
        per_kv = MLA_NOPE_DIM + MLA_V_DIM
        qb, kb, vb = _mla_proj_calls(
            hb, vec(mla_q_norm_g[l]), vec(mla_kv_norm_g[l]), _pad_heads(w_mla_qb[l], MLA_HEADS, 0, per_q),
            _pad_heads(w_mla_kvb[l], MLA_HEADS, 0, MLA_NOPE_DIM), _pad_heads(w_mla_kvb[l], MLA_HEADS, MLA_NOPE_DIM, per_kv),
            cos_m, sin_m)
        ob = _flash_call(qb.reshape(B, S, -1), kb.reshape(B, S, -1), vb.reshape(B, S, -1), MLA_HEADS, 0, 0, 0,
                         n_maps=1, tq=512, tk=512, name="mla_attn")
        dil = [_dil_call(hc, B, S, gi, d) for gi, (_, d) in enumerate(DIL_PATTERNS)]
        wb_pad = jnp.pad(w_branch_b[l].reshape(MLA_HEADS, MLA_V_DIM, D),
                         ((0, 0), (0, LANES - MLA_V_DIM), (0, 0))).reshape(MLA_HEADS * LANES, D).astype(bf16)
        x32, xb = _merge_call(
            x32, oa.reshape(T, -1), ob.reshape(T, -1), dil[0][0], dil[1][0], dil[2][0], dil[0][1], dil[1][1], dil[2][1],
            hg, w_branch_a[l].astype(bf16), wb_pad, w_branch_c[l].astype(bf16), w_out[l].astype(bf16),
            vec(ln1_g[l]), vec(ln1_b[l]), alpha)
        gate_t = _router_call(xb, rw_t, rb)
        per_chunk = 4
        gate_c = gate_t.reshape(N_EXPERTS // per_chunk, per_chunk, T).transpose(0, 2, 1)
        H = EXPERT_HIDDEN
        wg = w_exp_gate[l].transpose(1, 0, 2).reshape(D, N_EXPERTS * H).astype(bf16)
        wu = w_exp_up[l].transpose(1, 0, 2).reshape(D, N_EXPERTS * H).astype(bf16)
        wd = w_exp_down[l].reshape(N_EXPERTS * H, D).astype(bf16)
        x32, xb = _moe_call(x32, xb, gate_c, wg, wu, wd, w_sh_gate[l].astype(bf16), w_sh_up[l].astype(bf16),
                            w_sh_down[l].astype(bf16), vec(ln2_g[l]), vec(ln2_b[l]), alpha, per_chunk=per_chunk)
    return x32.reshape(B, S, D)
```

```python
import functools
import math

import jax
import jax.numpy as jnp
from jax import lax
from jax.experimental import pallas as pl
from jax.experimental.pallas import tpu as pltpu

f32 = jnp.float32
bf16 = jnp.bfloat16

D_MODEL = 1024
HEAD_DIM = 64
ROPE_THETA = 500000.0
ROT_DIM = HEAD_DIM // 4
DIFF_HEADS = 4
MLA_HEADS = 8
MLA_Q_RANK = 384
MLA_KV_RANK = 256
MLA_NOPE_DIM = 64
MLA_ROPE_DIM = 32
MLA_V_DIM = 64
DIL_PATTERNS = ((128, 1), (512, 4), (2048, 16))
DIL_GROUPS = 3
DIL_HEADS = 4
DIL_SIDE = 64
N_EXPERTS = 16
N_EXPERT_GROUPS = 4
EXPERTS_PER_GROUP = 4
EXPERT_HIDDEN = 256
IN_SIZES = (512, 512, 512, 384, 256, 32, 768, 768, 1536, 3072)
NEG_INF = -1e30
LOG2E = math.log2(math.e)

LANES = 128
VMEM_LIMIT = 52 * 1024 * 1024

HA_W = 1536
HB_W = 768
HC_W = 3072
HG_W = 3072


def _cparams(sem):
    return pltpu.CompilerParams(dimension_semantics=sem, vmem_limit_bytes=VMEM_LIMIT)


def _layer_norm(z, g, b):
    mu = jnp.mean(z, axis=-1, keepdims=True)
    zc = z - mu
    var = jnp.mean(zc * zc, axis=-1, keepdims=True)
    return zc * lax.rsqrt(var + 1e-5) * g + b


def _rms_norm(z, g):
    return z * lax.rsqrt(jnp.mean(z * z, axis=-1, keepdims=True) + 1e-6) * g


def _dot(a, b):
    return jnp.dot(a, b, preferred_element_type=f32)


def _dot_nt(a, b):
    return lax.dot_general(a, b, (((1,), (1,)), ((), ())), preferred_element_type=f32)


def _rope_tile(xt, cos_t, sin_t, first_half, shift):
    xr = jnp.where(first_half, pltpu.roll(xt, LANES - shift, 1), pltpu.roll(xt, shift, 1))
    return xt * cos_t + xr * sin_t


def _ln_kernel(x_ref, g_ref, b_ref, o32_ref, ob_ref):
    y = _layer_norm(x_ref[...], g_ref[...], b_ref[...])
    o32_ref[...] = y
    ob_ref[...] = y.astype(bf16)


def _ln_call(x, g, b, tm=512):
    T, D = x.shape
    row = pl.BlockSpec((tm, D), lambda i: (i, 0))
    vec = pl.BlockSpec((1, D), lambda i: (0, 0))
    return pl.pallas_call(
        _ln_kernel, grid=(T // tm,), in_specs=[row, vec, vec], out_specs=[row, row],
        out_shape=[jax.ShapeDtypeStruct((T, D), f32), jax.ShapeDtypeStruct((T, D), bf16)],
        compiler_params=_cparams(("parallel",)), name="ln_in")(x, g, b)


def _inproj_plan():
    qscale = HEAD_DIM ** -0.5 * LOG2E
    plan = []
    for c in range(0, 512, 256):
        plan.append((c, 256, "rope", 0, c, qscale))
    for c in range(512, 1024, 256):
        plan.append((c, 256, "rope", 0, c, 1.0))
    plan.append((1024, 512, "plain", 0, 1024, 1.0))
    plan.append((HA_W, HB_W, "plain", 1, 0, 1.0))
    base = HA_W + HB_W
    for g in range(DIL_GROUPS):
        o = 1024 * g
        plan.append((base + o, 256, "rope", 2, o, qscale))
        plan.append((base + o + 256, 256, "rope", 2, o + 256, 1.0))
        plan.append((base + o + 512, 512, "plain", 2, o + 512, 1.0))
    base = HA_W + HB_W + HC_W
    for c in range(0, HG_W, 512):
        plan.append((base + c, 512, "gate", 3, c, 1.0))
    return plan


def _inproj_kernel(x_ref, w_ref, bg_ref, cp_ref, sp_ref, ha_ref, hb_ref, hc_ref, hg_ref):
    outs = (ha_ref, hb_ref, hc_ref, hg_ref)
    x = x_ref[...]
    cos_t = cp_ref[...]
    sin_t = sp_ref[...]
    lane = lax.broadcasted_iota(jnp.int32, cos_t.shape, 1)
    first_half = (lane % HEAD_DIM) < (ROT_DIM // 2)
    for (c0, w, kind, oi, o0, scale) in _inproj_plan():
        acc = _dot(x, w_ref[:, c0:c0 + w])
        if kind == "rope":
            tiles = []
            for t in range(w // LANES):
                y = _rope_tile(acc[:, t * LANES:(t + 1) * LANES], cos_t, sin_t, first_half, ROT_DIM // 2)
                tiles.append(y * scale if scale != 1.0 else y)
            acc = jnp.concatenate(tiles, axis=1)
        elif kind == "gate":
            acc = jax.nn.sigmoid(acc + bg_ref[:, o0:o0 + w])
        outs[oi][:, o0:o0 + w] = acc.astype(outs[oi].dtype)


def _inproj_call(xb, w, bg, cos_t, sin_t, tm=256):
    T, D = xb.shape
    NW = w.shape[1]
    row = lambda width: pl.BlockSpec((tm, width), lambda i: (i, 0))
    return pl.pallas_call(
        _inproj_kernel, grid=(T // tm,),
        in_specs=[row(D), pl.BlockSpec((D, NW), lambda i: (0, 0)), pl.BlockSpec((1, HG_W), lambda i: (0, 0)),
                  row(LANES), row(LANES)],
        out_specs=[row(HA_W), row(HB_W), row(HC_W), row(HG_W)],
        out_shape=[jax.ShapeDtypeStruct((T, HA_W), bf16), jax.ShapeDtypeStruct((T, HB_W), f32),
                   jax.ShapeDtypeStruct((T, HC_W), bf16), jax.ShapeDtypeStruct((T, HG_W), bf16)],
        compiler_params=_cparams(("parallel",)), name="inproj")(xb, w, bg, cos_t, sin_t)


def _flash_kernel(*refs, n_maps, tq, tk, n_kv, lam_init):
    if n_maps == 2:
        q_ref, k_ref, v_ref, lam_ref, g_ref, o_ref, qt_sc, m_sc, l_sc, acc_sc, s_sc, mc_sc = refs
    else:
        q_ref, k_ref, v_ref, o_ref, qt_sc, m_sc, l_sc, acc_sc, s_sc, mc_sc = refs
    q = q_ref[...].astype(f32)
    if n_maps == 2:
        lane = lax.broadcasted_iota(jnp.int32, q.shape, 1)
        zero = jnp.zeros_like(q)
        q = jnp.concatenate([jnp.where(lane < HEAD_DIM, q, zero), jnp.where(lane >= HEAD_DIM, q, zero)], axis=0)
    qt_sc[...] = q.T.astype(bf16)
    m_sc[...] = jnp.full(m_sc.shape, -jnp.inf, f32)
    l_sc[...] = jnp.zeros(l_sc.shape, f32)
    acc_sc[...] = jnp.zeros(acc_sc.shape, f32)

    def scores(j, buf):
        off = pl.multiple_of(jnp.minimum(j, n_kv - 1) * tk, tk)
        s = _dot(k_ref[pl.ds(off, tk), :], qt_sc[...])
        s_sc[buf] = s
        mc_sc[buf] = jnp.max(s, axis=0, keepdims=True)

    def accumulate(j, buf):
        off = pl.multiple_of(j * tk, tk)
        m_prev = m_sc[...]
        m_new = jnp.maximum(m_prev, mc_sc[buf])
        p = jnp.exp2(s_sc[buf] - m_new)
        a = jnp.exp2(m_prev - m_new)
        l_sc[...] = a * l_sc[...] + jnp.sum(p, axis=0, keepdims=True)
        pv = lax.dot_general(v_ref[pl.ds(off, tk), :], p.astype(bf16), (((0,), (0,)), ((), ())),
                             preferred_element_type=f32)
        acc_sc[...] = a * acc_sc[...] + pv
        m_sc[...] = m_new

    scores(0, 0)

    def body(jj, carry):
        for u in range(2):
            j = 2 * jj + u
            scores(j + 1, 1 - u)
            accumulate(j, u)
        return carry

    lax.fori_loop(0, n_kv // 2, body, 0)
    o = acc_sc[...] / l_sc[...]
    if n_maps == 2:
        lv = lam_ref[...]
        lam = (jnp.exp(jnp.sum(lv[0:1] * lv[1:2], axis=-1, keepdims=True))
               - jnp.exp(jnp.sum(lv[2:3] * lv[3:4], axis=-1, keepdims=True)) + lam_init)
        o = o[:, :tq] - lam * o[:, tq:]
        ms = jnp.mean(o * o, axis=0, keepdims=True)
        o = o * lax.rsqrt(ms + 1e-6) * g_ref[...] * (1.0 - lam_init)
    o_ref[...] = o.T.astype(o_ref.dtype)


def _flash_call(q_arr, k_arr, v_arr, n_heads, q_blk0, k_blk0, v_blk0, n_maps, tq, tk, name,
                lam_vec=None, norm_g=None, lam_init=0.0):
    B, S, _ = q_arr.shape
    cols = n_maps * tq
    in_specs = [pl.BlockSpec((None, tq, LANES), lambda b, h, i: (b, i, q_blk0 + h)),
                pl.BlockSpec((None, S, LANES), lambda b, h, i: (b, 0, k_blk0 + h)),
                pl.BlockSpec((None, S, LANES), lambda b, h, i: (b, 0, v_blk0 + h))]
    args = [q_arr, k_arr, v_arr]
    if n_maps == 2:
        in_specs += [pl.BlockSpec((8, LANES), lambda b, h, i: (0, 0)),
                     pl.BlockSpec((LANES, 1), lambda b, h, i: (0, 0))]
        args += [lam_vec, norm_g]
    kern = functools.partial(_flash_kernel, n_maps=n_maps, tq=tq, tk=tk, n_kv=S // tk, lam_init=lam_init)
    return pl.pallas_call(
        kern, grid=(B, n_heads, S // tq), in_specs=in_specs,
        out_specs=pl.BlockSpec((None, tq, LANES), lambda b, h, i: (b, i, h)),
        out_shape=jax.ShapeDtypeStruct((B, S, n_heads * LANES), bf16),
        scratch_shapes=[pltpu.VMEM((LANES, cols), bf16), pltpu.VMEM((1, cols), f32), pltpu.VMEM((1, cols), f32),
                        pltpu.VMEM((LANES, cols), f32), pltpu.VMEM((2, tk, cols), f32), pltpu.VMEM((2, 1, cols), f32)],
        compiler_params=_cparams(("parallel", "parallel", "arbitrary")), name=name)(*args)


def _mla_q_kernel(cq_ref, g_ref, w_ref, cm_ref, sm_ref, o_ref, *, scale):
    qn = _rms_norm(cq_ref[...], g_ref[...]).astype(bf16)
    cos_t = cm_ref[...]
    sin_t = sm_ref[...]
    lane = lax.broadcasted_iota(jnp.int32, cos_t.shape, 1)
    first_half = lane < MLA_NOPE_DIM + MLA_ROPE_DIM // 2
    for h in range(MLA_HEADS):
        acc = _dot(qn, w_ref[:, h * LANES:(h + 1) * LANES])
        y = _rope_tile(acc, cos_t, sin_t, first_half, MLA_ROPE_DIM // 2) * scale
        o_ref[:, h * LANES:(h + 1) * LANES] = y.astype(o_ref.dtype)


def _mla_kv_kernel(ckv_ref, kr_ref, g_ref, wk_ref, wv_ref, cm_ref, sm_ref, k_ref, v_ref):
    cn = _rms_norm(ckv_ref[...], g_ref[...]).astype(bf16)
    cos_t = cm_ref[...]
    lane = lax.broadcasted_iota(jnp.int32, cos_t.shape, 1)
    first_half = lane < MLA_NOPE_DIM + MLA_ROPE_DIM // 2
    kr = _rope_tile(kr_ref[...], cos_t, sm_ref[...], first_half, MLA_ROPE_DIM // 2)
    for h in range(MLA_HEADS):
        sl = slice(h * LANES, (h + 1) * LANES)
        k_ref[:, sl] = (_dot(cn, wk_ref[:, sl]) + kr).astype(k_ref.dtype)
        v_ref[:, sl] = _dot(cn, wv_ref[:, sl]).astype(v_ref.dtype)


def _mla_proj_calls(hb, qg, kvg, wq, wk, wv, cos_m, sin_m, tm=512):
    T = hb.shape[0]
    W = MLA_HEADS * LANES
    row = lambda width, blk=0: pl.BlockSpec((tm, width), lambda i: (i, blk))
    full = lambda a: pl.BlockSpec(a.shape, lambda i: (0, 0))
    scale = (MLA_NOPE_DIM + MLA_ROPE_DIM) ** -0.5 * LOG2E
    qb = pl.pallas_call(
        functools.partial(_mla_q_kernel, scale=scale), grid=(T // tm,),
        in_specs=[row(MLA_Q_RANK, 1), full(qg), full(wq), row(LANES), row(LANES)],
        out_specs=row(W), out_shape=jax.ShapeDtypeStruct((T, W), bf16),
        compiler_params=_cparams(("parallel",)), name="mla_q")(hb, qg, wq, cos_m, sin_m)
    kb, vb = pl.pallas_call(
        _mla_kv_kernel, grid=(T // tm,),
        in_specs=[row(MLA_KV_RANK, 0), row(LANES, 2), full(kvg), full(wk), full(wv), row(LANES), row(LANES)],
        out_specs=[row(W), row(W)],
        out_shape=[jax.ShapeDtypeStruct((T, W), bf16), jax.ShapeDtypeStruct((T, W), bf16)],
        compiler_params=_cparams(("parallel",)), name="mla_kv")(hb, hb, kvg, wk, wv, cos_m, sin_m)
    return qb, kb, vb


def _dil_kernel(q_ref, kp_ref, kc_ref, kn_ref, vp_ref, vc_ref, vn_ref, o_ref, lse_ref, kcat, vcat, *, tq, sub_len):
    i = pl.program_id(2)
    side = DIL_SIDE
    kcat[0:side, :] = kp_ref[...]
    kcat[side:side + tq, :] = kc_ref[...]
    kcat[side + tq:, :] = kn_ref[...]
    vcat[0:side, :] = vp_ref[...]
    vcat[side:side + tq, :] = vc_ref[...]
    vcat[side + tq:, :] = vn_ref[...]
    qs, ks = 2 * side, 4 * side
    lane = lax.broadcasted_iota(jnp.int32, (qs, LANES), 1)
    ii = lax.broadcasted_iota(jnp.int32, (qs, ks), 0)
    jj = lax.broadcasted_iota(jnp.int32, (qs, ks), 1)
    band = (jj - ii >= 0) & (jj - ii <= 2 * side)
    for sb in range(tq // qs):
        q0 = sb * qs
        pos = jj + (i * tq + q0 - side)
        mask = band & (pos >= 0) & (pos < sub_len)
        lse_tile = jnp.zeros((qs, LANES), f32)
        for h in range(DIL_HEADS):
            t = h // 2
            qt = q_ref[q0:q0 + qs, t * LANES:(t + 1) * LANES]
            keep = (lane < HEAD_DIM) if h % 2 == 0 else (lane >= HEAD_DIM)
            qm = jnp.where(keep, qt, jnp.zeros_like(qt))
            s = _dot_nt(qm, kcat[q0:q0 + ks, t * LANES:(t + 1) * LANES])
            s = jnp.where(mask, s, NEG_INF)
            m = jnp.max(s, axis=-1, keepdims=True)
            p = jnp.exp2(s - m)
            l = jnp.sum(p, axis=-1, keepdims=True)
            o = _dot(p.astype(bf16), vcat[q0:q0 + ks, h * LANES:(h + 1) * LANES]) / l
            o_ref[q0:q0 + qs, h * LANES:(h + 1) * LANES] = o.astype(o_ref.dtype)
            lse_tile = jnp.where(lane == h, m + jnp.log2(l), lse_tile)
        lse_ref[q0:q0 + qs, :] = lse_tile


def _dil_call(hc, B, S, g, dilation):
    d = dilation
    L = S // d
    tq = min(512, L)
    side = DIL_SIDE
    hcv = hc.reshape(B, L, d * HC_W)
    nblk = tq // side
    last = L // side - 1
    qw, vw = 2 * LANES, 4 * LANES
    qcol = lambda r: (r * HC_W + 1024 * g) // qw
    vcol = lambda r: (r * HC_W + 1024 * g + 512) // vw
    in_specs = [
        pl.BlockSpec((None, tq, qw), lambda b, r, i: (b, i, qcol(r))),
        pl.BlockSpec((None, side, qw), lambda b, r, i: (b, jnp.maximum(i * nblk - 1, 0), qcol(r) + 1)),
        pl.BlockSpec((None, tq, qw), lambda b, r, i: (b, i, qcol(r) + 1)),
        pl.BlockSpec((None, side, qw), lambda b, r, i: (b, jnp.minimum((i + 1) * nblk, last), qcol(r) + 1)),
        pl.BlockSpec((None, side, vw), lambda b, r, i: (b, jnp.maximum(i * nblk - 1, 0), vcol(r))),
        pl.BlockSpec((None, tq, vw), lambda b, r, i: (b, i, vcol(r))),
        pl.BlockSpec((None, side, vw), lambda b, r, i: (b, jnp.minimum((i + 1) * nblk, last), vcol(r))),
    ]
    o, lse = pl.pallas_call(
        functools.partial(_dil_kernel, tq=tq, sub_len=L), grid=(B, d, L // tq), in_specs=in_specs,
        out_specs=[pl.BlockSpec((None, tq, vw), lambda b, r, i: (b, i, r)),
                   pl.BlockSpec((None, tq, LANES), lambda b, r, i: (b, i, r))],
        out_shape=[jax.ShapeDtypeStruct((B, L, d * vw), bf16), jax.ShapeDtypeStruct((B, L, d * LANES), f32)],
        scratch_shapes=[pltpu.VMEM((tq + 2 * side, qw), bf16), pltpu.VMEM((tq + 2 * side, vw), bf16)],
        compiler_params=_cparams(("parallel", "parallel", "arbitrary")),
        name=f"dilated_d{d}")(hcv, hcv, hcv, hcv, hcv, hcv, hcv)
    return o.reshape(B * S, vw), lse.reshape(B * S, LANES)


def _merge_kernel(x_ref, oa_ref, ob_ref, o1_ref, o2_ref, o3_ref, l1_ref, l2_ref, l3_ref, g_ref,
                  wa_ref, wb_ref, wc_ref, wo_ref, lg_ref, lb_ref, x32_ref, xb_ref, *, alpha):
    l1, l2, l3 = l1_ref[...], l2_ref[...], l3_ref[...]
    mx = jnp.maximum(jnp.maximum(l1, l2), l3)
    e1, e2, e3 = jnp.exp2(l1 - mx), jnp.exp2(l2 - mx), jnp.exp2(l3 - mx)
    den = e1 + e2 + e3
    w1, w2, w3 = e1 / den, e2 / den, e3 / den
    tiles = []
    for h in range(DIL_HEADS):
        sl = slice(h * LANES, (h + 1) * LANES)
        oc = (w1[:, h:h + 1] * o1_ref[:, sl].astype(f32) + w2[:, h:h + 1] * o2_ref[:, sl].astype(f32)
              + w3[:, h:h + 1] * o3_ref[:, sl].astype(f32))
        tiles.append(oc.astype(bf16))
    oc = jnp.concatenate(tiles, axis=1)
    D = D_MODEL
    y = (g_ref[:, 0:D].astype(f32) * _dot(oa_ref[...], wa_ref[...])
         + g_ref[:, D:2 * D].astype(f32) * _dot(ob_ref[...], wb_ref[...])
         + g_ref[:, 2 * D:3 * D].astype(f32) * _dot(oc, wc_ref[...]))
    mix = _dot(y.astype(bf16), wo_ref[...])
    out = _layer_norm(alpha * x_ref[...] + mix, lg_ref[...], lb_ref[...])
    x32_ref[...] = out
    xb_ref[...] = out.astype(bf16)


def _merge_call(x32, oa, ob, o1, o2, o3, l1, l2, l3, hg, wa, wb, wc, wo, lg, lb, alpha, tm=256):
    T, D = x32.shape
    row = lambda a: pl.BlockSpec((tm, a.shape[1]), lambda i: (i, 0))
    full = lambda a: pl.BlockSpec(a.shape, lambda i: (0, 0))
    acts = [x32, oa, ob, o1, o2, o3, l1, l2, l3, hg]
    consts = [wa, wb, wc, wo, lg, lb]
    return pl.pallas_call(
        functools.partial(_merge_kernel, alpha=alpha), grid=(T // tm,),
        in_specs=[row(a) for a in acts] + [full(a) for a in consts],
        out_specs=[pl.BlockSpec((tm, D), lambda i: (i, 0))] * 2,
        out_shape=[jax.ShapeDtypeStruct((T, D), f32), jax.ShapeDtypeStruct((T, D), bf16)],
        compiler_params=_cparams(("parallel",)), name="merge")(*acts, *consts)


def _first_true(flags):
    out, seen = [], None
    for f in flags:
        out.append(f if seen is None else f & ~seen)
        seen = f if seen is None else seen | f
    return out


def _router_kernel(x_ref, rw_ref, rb_ref, gate_ref):
    logits = _dot_nt(rw_ref[...], x_ref[...])
    scores = jax.nn.sigmoid(logits)
    biased = scores + rb_ref[...]
    sc = [scores[e:e + 1, :] for e in range(N_EXPERTS)]
    bi = [biased[e:e + 1, :] for e in range(N_EXPERTS)]
    npg = EXPERTS_PER_GROUP
    gscore = []
    for g in range(N_EXPERT_GROUPS):
        v = bi[g * npg:(g + 1) * npg]
        best = None
        for a in range(npg):
            for b in range(a + 1, npg):
                best = v[a] + v[b] if best is None else jnp.maximum(best, v[a] + v[b])
        gscore.append(best)
    gbest = functools.reduce(jnp.maximum, gscore)
    gsel = _first_true([gs == gbest for gs in gscore])
    zero = jnp.zeros_like(gbest)
    bsel = [functools.reduce(lambda x, y: x + y, [jnp.where(gsel[g], bi[g * npg + j], zero)
                                                  for g in range(N_EXPERT_GROUPS)]) for j in range(npg)]
    ssel = [functools.reduce(lambda x, y: x + y, [jnp.where(gsel[g], sc[g * npg + j], zero)
                                                  for g in range(N_EXPERT_GROUPS)]) for j in range(npg)]
    t1 = functools.reduce(jnp.maximum, bsel)
    i1 = _first_true([b == t1 for b in bsel])
    rest = [jnp.where(i1[j], -jnp.inf, bsel[j]) for j in range(npg)]
    t2 = functools.reduce(jnp.maximum, rest)
    i2 = _first_true([rest[j] == t2 for j in range(npg)])
    w1 = functools.reduce(lambda x, y: x + y, [jnp.where(i1[j], ssel[j], zero) for j in range(npg)])
    w2 = functools.reduce(lambda x, y: x + y, [jnp.where(i2[j], ssel[j], zero) for j in range(npg)])
    den = w1 + w2
    local = [jnp.where(i1[j], w1 / den, zero) + jnp.where(i2[j], w2 / den, zero) for j in range(npg)]
    rows = [jnp.where(gsel[e // npg], local[e % npg], zero) for e in range(N_EXPERTS)]
    gate_ref[...] = jnp.concatenate(rows, axis=0)


def _router_call(xb, rw_t, rb, tm=1024):
    T, D = xb.shape
    return pl.pallas_call(
        _router_kernel, grid=(T // tm,),
        in_specs=[pl.BlockSpec((tm, D), lambda i: (i, 0)), pl.BlockSpec((N_EXPERTS, D), lambda i: (0, 0)),
                  pl.BlockSpec((N_EXPERTS, 1), lambda i: (0, 0))],
        out_specs=pl.BlockSpec((N_EXPERTS, tm), lambda i: (0, i)),
        out_shape=jax.ShapeDtypeStruct((N_EXPERTS, T), f32),
        compiler_params=_cparams(("parallel",)), name="router")(xb, rw_t, rb)


def _moe_kernel(x32_ref, xb_ref, gate_ref, wg_ref, wu_ref, wd_ref, wsg_ref, wsu_ref, wsd_ref, lg_ref, lb_ref,
                o32_ref, ob_ref, acc_sc, *, alpha, n_chunks, per_chunk):
    c = pl.program_id(1)
    x = xb_ref[...]

    @pl.when(c == 0)
    def _():
        hs = jax.nn.silu(_dot(x, wsg_ref[...])) * _dot(x, wsu_ref[...])
        acc_sc[...] = _dot(hs.astype(bf16), wsd_ref[...])

    gate = gate_ref[...]
    hid = []
    for e in range(per_chunk):
        sl = slice(e * EXPERT_HIDDEN, (e + 1) * EXPERT_HIDDEN)
        h = jax.nn.silu(_dot(x, wg_ref[:, sl])) * _dot(x, wu_ref[:, sl]) * gate[:, e:e + 1]
        hid.append(h.astype(bf16))
    acc_sc[...] += _dot(jnp.concatenate(hid, axis=1), wd_ref[...])

    @pl.when(c == n_chunks - 1)
    def _():
        out = _layer_norm(alpha * x32_ref[...] + acc_sc[...], lg_ref[...], lb_ref[...])
        o32_ref[...] = out
        ob_ref[...] = out.astype(bf16)


def _moe_call(x32, xb, gate_c, wg, wu, wd, wsg, wsu, wsd, lg, lb, alpha, tm=1024, per_chunk=4):
    T, D = x32.shape
    n_chunks = N_EXPERTS // per_chunk
    cw = per_chunk * EXPERT_HIDDEN
    row = pl.BlockSpec((tm, D), lambda i, c: (i, 0))
    full = lambda a: pl.BlockSpec(a.shape, lambda i, c: (0, 0))
    return pl.pallas_call(
        functools.partial(_moe_kernel, alpha=alpha, n_chunks=n_chunks, per_chunk=per_chunk),
        grid=(T // tm, n_chunks),
        in_specs=[row, row, pl.BlockSpec((None, tm, per_chunk), lambda i, c: (c, i, 0)),
                  pl.BlockSpec((D, cw), lambda i, c: (0, c)), pl.BlockSpec((D, cw), lambda i, c: (0, c)),
                  pl.BlockSpec((cw, D), lambda i, c: (c, 0)),
                  full(wsg), full(wsu), full(wsd), full(lg), full(lb)],
        out_specs=[row, row],
        out_shape=[jax.ShapeDtypeStruct((T, D), f32), jax.ShapeDtypeStruct((T, D), bf16)],
        scratch_shapes=[pltpu.VMEM((tm, D), f32)],
        compiler_params=_cparams(("parallel", "arbitrary")), name="moe")(
            x32, xb, gate_c, wg, wu, wd, wsg, wsu, wsd, lg, lb)


def _rope_tables(positions):
    pos = positions.astype(f32).reshape(-1, 1)
    T = pos.shape[0]
    ones, zeros = jnp.ones, jnp.zeros
    inv_p = ROPE_THETA ** (-jnp.arange(0, ROT_DIM, 2, dtype=f32) / ROT_DIM)
    ang = pos * inv_p
    c, s = jnp.cos(ang), jnp.sin(ang)
    pad = HEAD_DIM - ROT_DIM
    cos_p = jnp.tile(jnp.concatenate([c, c, ones((T, pad), f32)], axis=1), (1, 2))
    sin_p = jnp.tile(jnp.concatenate([-s, s, zeros((T, pad), f32)], axis=1), (1, 2))
    inv_m = ROPE_THETA ** (-jnp.arange(0, MLA_ROPE_DIM, 2, dtype=f32) / MLA_ROPE_DIM)
    ang = pos * inv_m
    c, s = jnp.cos(ang), jnp.sin(ang)
    tail = LANES - MLA_NOPE_DIM - MLA_ROPE_DIM
    cos_m = jnp.concatenate([ones((T, MLA_NOPE_DIM), f32), c, c, ones((T, tail), f32)], axis=1)
    sin_m = jnp.concatenate([zeros((T, MLA_NOPE_DIM), f32), -s, s, zeros((T, tail), f32)], axis=1)
    return cos_p, sin_p, cos_m, sin_m


def _pack_w_in(w):
    offs = [0]
    for sz in IN_SIZES:
        offs.append(offs[-1] + sz)
    a_q, a_k, a_v, b_cq, b_ckv, b_kr, c_q, c_k, c_v, g = [w[:, offs[i]:offs[i + 1]] for i in range(len(IN_SIZES))]
    D = w.shape[0]
    z = lambda n: jnp.zeros((D, n), w.dtype)
    kr_pad = jnp.concatenate([z(MLA_NOPE_DIM), b_kr, z(LANES - MLA_NOPE_DIM - MLA_ROPE_DIM)], axis=1)
    cols = [a_q, a_k, a_v, b_ckv, kr_pad, b_cq]
    for gi in range(DIL_GROUPS):
        cols += [c_q[:, 256 * gi:256 * (gi + 1)], c_k[:, 256 * gi:256 * (gi + 1)], c_v[:, 512 * gi:512 * (gi + 1)]]
    cols.append(g)
    return jnp.concatenate(cols, axis=1).astype(bf16)


def _pad_heads(w, n_heads, lo, hi):
    K = w.shape[0]
    per = w.shape[1] // n_heads
    wh = w.reshape(K, n_heads, per)[:, :, lo:hi]
    wh = jnp.pad(wh, ((0, 0), (0, 0), (0, LANES - (hi - lo))))
    return wh.reshape(K, n_heads * LANES).astype(bf16)


def kernel(x, positions, ln_in_g, ln_in_b, w_in, b_gate, lam_q1, lam_k1, lam_q2, lam_k2, diff_norm_g, mla_q_norm_g, mla_kv_norm_g, w_mla_qb, w_mla_kvb, w_branch_a, w_branch_b, w_branch_c, w_out, ln1_g, ln1_b, router_w, router_bias, w_exp_gate, w_exp_up, w_exp_down, w_sh_gate, w_sh_up, w_sh_down, ln2_g, ln2_b):
    B, S, D = x.shape
    T = B * S
    depth = w_in.shape[0]
    alpha = (2 * depth) ** 0.25
    cos_p, sin_p, cos_m, sin_m = _rope_tables(positions)
    x32, xb = _ln_call(x.reshape(T, D), ln_in_g.reshape(1, D), ln_in_b.reshape(1, D))
    rw_t = router_w.T.astype(bf16)
    rb = router_bias.reshape(N_EXPERTS, 1).astype(f32)
    vec = lambda v: v.reshape(1, -1).astype(f32)
    for l in range(depth):
        lam_init = 0.8 - 0.6 * math.exp(-0.3 * l)
        ha, hb, hc, hg = _inproj_call(xb, _pack_w_in(w_in[l]), vec(b_gate[l]), cos_p, sin_p)
        lam_vec = jnp.pad(jnp.stack([lam_q1[l], lam_k1[l], lam_q2[l], lam_k2[l]]).astype(f32),
                          ((0, 4), (0, LANES - HEAD_DIM)))
        ha3 = ha.reshape(B, S, HA_W)
        oa = _flash_call(ha3, ha3, ha3, DIFF_HEADS, 0, DIFF_HEADS, 2 * DIFF_HEADS, n_maps=2, tq=256, tk=512,
                         name="diff_attn", lam_vec=lam_vec, norm_g=diff_norm_g[l].reshape(-1, 1).astype(f32),
                         lam_init=lam_init)
        per_q = MLA_NOPE_DIM + MLA_ROPE_DIM
        per_kv = MLA_NOPE_DIM + MLA_V_DIM
        qb, kb, vb = _mla_proj_calls(
            hb, vec(mla_q_norm_g[l]), vec(mla_kv_norm_g[l]), _pad_heads(w_mla_qb[l], MLA_HEADS, 0, per_q),
            _pad_heads(w_mla_kvb[l], MLA_HEADS, 0, MLA_NOPE_DIM), _pad_heads(w_mla_kvb[l], MLA_HEADS, MLA_NOPE_DIM, per_kv),
            cos_m, sin_m)
        ob = _flash_call(qb.reshape(B, S, -1), kb.reshape(B, S, -1), vb.reshape(B, S, -1), MLA_HEADS, 0, 0, 0,
                         n_maps=1, tq=512, tk=512, name="mla_attn")
        dil = [_dil_call(hc, B, S, gi, d) for gi, (_, d) in enumerate(DIL_PATTERNS)]
        wb_pad = jnp.pad(w_branch_b[l].reshape(MLA_HEADS, MLA_V_DIM, D),
                         ((0, 0), (0, LANES - MLA_V_DIM), (0, 0))).reshape(MLA_HEADS * LANES, D).astype(bf16)
        x32, xb = _merge_call(
            x32, oa.reshape(T, -1), ob.reshape(T, -1), dil[0][0], dil[1][0], dil[2][0], dil[0][1], dil[1][1], dil[2][1],
            hg, w_branch_a[l].astype(bf16), wb_pad, w_branch_c[l].astype(bf16), w_out[l].astype(bf16),
            vec(ln1_g[l]), vec(ln1_b[l]), alpha)
        gate_t = _router_call(xb, rw_t, rb)
        per_chunk = 4
        gate_c = gate_t.reshape(N_EXPERTS // per_chunk, per_chunk, T).transpose(0, 2, 1)
        H = EXPERT_HIDDEN
        wg = w_exp_gate[l].transpose(1, 0, 2).reshape(D, N_EXPERTS * H).astype(bf16)
        wu = w_exp_up[l].transpose(1, 0, 2).reshape(D, N_EXPERTS * H).astype(bf16)
        wd = w_exp_down[l].reshape(N_EXPERTS * H, D).astype(bf16)
        x32, xb = _moe_call(x32, xb, gate_c, wg, wu, wd, w_sh_gate[l].astype(bf16), w_sh_up[l].astype(bf16),
                            w_sh_down[l].astype(bf16), vec(ln2_g[l]), vec(ln2_b[l]), alpha, per_chunk=per_chunk)
    return x32.reshape(B, S, D)
```

```python
import functools
import math

import jax
import jax.numpy as jnp
from jax import lax
from jax.experimental import pallas as pl
from jax.experimental.pallas import tpu as pltpu

f32 = jnp.float32
bf16 = jnp.bfloat16

D_MODEL = 1024
HEAD_DIM = 64
ROPE_THETA = 500000.0
ROT_DIM = HEAD_DIM // 4
DIFF_HEADS = 4
MLA_HEADS = 8
MLA_Q_RANK = 384
MLA_KV_RANK = 256
MLA_NOPE_DIM = 64
MLA_ROPE_DIM = 32
MLA_V_DIM = 64
DIL_PATTERNS = ((128, 1), (512, 4), (2048, 16))
DIL_GROUPS = 3
DIL_HEADS = 4
DIL_SIDE = 64
N_EXPERTS = 16
N_EXPERT_GROUPS = 4
EXPERTS_PER_GROUP = 4
EXPERT_HIDDEN = 256
IN_SIZES = (512, 512, 512, 384, 256, 32, 768, 768, 1536, 3072)
NEG_INF = -1e30
LOG2E = math.log2(math.e)

LANES = 128
VMEM_LIMIT = 52 * 1024 * 1024

HA_W = 1536
HB_W = 768
HCG_W = 1024
HC_W = 3 * HCG_W
HG_W = 3072

FLASH_TK = 512
FLASH_COLS = 1024
FLASH_UNROLL = 8


def _cparams(sem):
    return pltpu.CompilerParams(dimension_semantics=sem, vmem_limit_bytes=VMEM_LIMIT)


def _layer_norm(z, g, b):
    mu = jnp.mean(z, axis=-1, keepdims=True)
    zc = z - mu
    var = jnp.mean(zc * zc, axis=-1, keepdims=True)
    return zc * lax.rsqrt(var + 1e-5) * g + b


def _rms_norm(z, g):
    return z * lax.rsqrt(jnp.mean(z * z, axis=-1, keepdims=True) + 1e-6) * g


def _dot(a, b):
    return jnp.dot(a, b, preferred_element_type=f32)


def _dot_nt(a, b):
    return lax.dot_general(a, b, (((1,), (1,)), ((), ())), preferred_element_type=f32)


def _rope_tile(xt, cos_t, sin_t, first_half, shift):
    xr = jnp.where(first_half, pltpu.roll(xt, LANES - shift, 1), pltpu.roll(xt, shift, 1))
    return xt * cos_t + xr * sin_t


def _ln_kernel(x_ref, g_ref, b_ref, o32_ref, ob_ref):
    y = _layer_norm(x_ref[...], g_ref[...], b_ref[...])
    o32_ref[...] = y
    ob_ref[...] = y.astype(bf16)


def _ln_call(x, g, b, tm=512):
    T, D = x.shape
    row = pl.BlockSpec((tm, D), lambda i: (i, 0))
    vec = pl.BlockSpec((1, D), lambda i: (0, 0))
    return pl.pallas_call(
        _ln_kernel, grid=(T // tm,), in_specs=[row, vec, vec], out_specs=[row, row],
        out_shape=[jax.ShapeDtypeStruct((T, D), f32), jax.ShapeDtypeStruct((T, D), bf16)],
        compiler_params=_cparams(("parallel",)), name="ln_in")(x, g, b)


def _inproj_plan():
    qscale = HEAD_DIM ** -0.5 * LOG2E
    plan = []
    for c in range(0, 512, 256):
        plan.append((c, 256, "rope", 0, c, qscale))
    for c in range(512, 1024, 256):
        plan.append((c, 256, "rope", 0, c, 1.0))
    plan.append((1024, 512, "plain", 0, 1024, 1.0))
    plan.append((HA_W, HB_W, "plain", 1, 0, 1.0))
    base = HA_W + HB_W
    for g in range(DIL_GROUPS):
        o = HCG_W * g
        plan.append((base + o, 256, "rope", 2 + g, 0, qscale))
        plan.append((base + o + 256, 256, "rope", 2 + g, 256, 1.0))
        plan.append((base + o + 512, 512, "plain", 2 + g, 512, 1.0))
    base = HA_W + HB_W + HC_W
    for c in range(0, HG_W, 512):
        plan.append((base + c, 512, "gate", 2 + DIL_GROUPS, c, 1.0))
    return plan


def _inproj_kernel(x_ref, w_ref, bg_ref, cp_ref, sp_ref, ha_ref, hb_ref, hc0_ref, hc1_ref, hc2_ref, hg_ref, stage):
    outs = (ha_ref, hb_ref, hc0_ref, hc1_ref, hc2_ref, hg_ref)
    x = x_ref[...]
    tm = x.shape[0]
    cos_t = cp_ref[...]
    sin_t = sp_ref[...]
    lane = lax.broadcasted_iota(jnp.int32, cos_t.shape, 1)
    first_half = (lane % HEAD_DIM) < (ROT_DIM // 2)
    for (c0, w, kind, oi, o0, scale) in _inproj_plan():
        acc = _dot(x, w_ref[:, c0:c0 + w])
        if kind == "rope":
            tiles = []
            for t in range(w // LANES):
                y = _rope_tile(acc[:, t * LANES:(t + 1) * LANES], cos_t, sin_t, first_half, ROT_DIM // 2)
                tiles.append(y * scale if scale != 1.0 else y)
            acc = jnp.concatenate(tiles, axis=1)
        elif kind == "gate":
            acc = jax.nn.sigmoid(acc + bg_ref[:, o0:o0 + w])
        out = outs[oi]
        if 2 <= oi < 2 + DIL_GROUPS:
            d = DIL_PATTERNS[oi - 2][1]
            if d == 1:
                out[0, :, o0:o0 + w] = acc.astype(out.dtype)
            else:
                nt = w // LANES
                for t in range(nt):
                    stage[t] = acc[:, t * LANES:(t + 1) * LANES]
                for r in range(d):
                    rows = [stage[t, pl.ds(r, tm // d, stride=d), :] for t in range(nt)]
                    out[r, :, o0:o0 + w] = jnp.concatenate(rows, axis=1).astype(out.dtype)
        else:
            out[:, o0:o0 + w] = acc.astype(out.dtype)


def _inproj_call(xb, w, bg, cos_t, sin_t, B, S, tm=256):
    T, D = xb.shape
    NW = w.shape[1]
    nb = S // tm
    row = lambda width: pl.BlockSpec((tm, width), lambda b, i: (b * nb + i, 0))
    const = lambda shape: pl.BlockSpec(shape, lambda b, i: (0, 0))
    hc_specs, hc_shapes = [], []
    for _, d in DIL_PATTERNS:
        hc_specs.append(pl.BlockSpec((None, d, tm // d, HCG_W), lambda b, i: (b, 0, i, 0)))
        hc_shapes.append(jax.ShapeDtypeStruct((B, d, S // d, HCG_W), bf16))
    return pl.pallas_call(
        _inproj_kernel, grid=(B, nb),
        in_specs=[row(D), const((D, NW)), const((1, HG_W)), row(LANES), row(LANES)],
        out_specs=[row(HA_W), row(HB_W)] + hc_specs + [row(HG_W)],
        out_shape=[jax.ShapeDtypeStruct((T, HA_W), bf16), jax.ShapeDtypeStruct((T, HB_W), f32)] + hc_shapes
        + [jax.ShapeDtypeStruct((T, HG_W), bf16)],
        scratch_shapes=[pltpu.VMEM((4, tm, LANES), f32)],
        compiler_params=_cparams(("parallel", "parallel")), name="inproj")(xb, w, bg, cos_t, sin_t)


def _flash_kernel(*refs, n_maps, tq, tk, n_kv, lam_init, dv):
    if n_maps == 2:
        q_ref, k_ref, v_ref, lam_ref, g_ref, o_ref, qt_sc, m_sc, l_sc, acc_sc, s_sc, mc_sc = refs
    else:
        q_ref, k_ref, v_ref, o_ref, qt_sc, m_sc, l_sc, acc_sc, s_sc, mc_sc = refs
    q = q_ref[...].astype(f32)
    if n_maps == 2:
        lane = lax.broadcasted_iota(jnp.int32, q.shape, 1)
        zero = jnp.zeros_like(q)
        q = jnp.concatenate([jnp.where(lane < HEAD_DIM, q, zero), jnp.where(lane >= HEAD_DIM, q, zero)], axis=0)
    qt_sc[...] = q.T.astype(bf16)
    m_sc[...] = jnp.full(m_sc.shape, -jnp.inf, f32)
    l_sc[...] = jnp.zeros(l_sc.shape, f32)
    acc_sc[...] = jnp.zeros(acc_sc.shape, f32)

    def scores(j, buf):
        off = pl.multiple_of(jnp.minimum(j, n_kv - 1) * tk, tk)
        s = _dot(k_ref[pl.ds(off, tk), :], qt_sc[...])
        s_sc[buf] = s
        mc_sc[buf] = jnp.max(s, axis=0, keepdims=True)

    def accumulate(j, buf):
        off = pl.multiple_of(j * tk, tk)
        m_prev = m_sc[...]
        m_new = jnp.maximum(m_prev, mc_sc[buf])
        p = jnp.exp2(s_sc[buf] - m_new)
        a = jnp.exp2(m_prev - m_new)
        l_sc[...] = a * l_sc[...] + jnp.sum(p, axis=0, keepdims=True)
        pv = lax.dot_general(v_ref[pl.ds(off, tk), 0:dv], p.astype(bf16), (((0,), (0,)), ((), ())),
                             preferred_element_type=f32)
        acc_sc[...] = a * acc_sc[...] + pv
        m_sc[...] = m_new

    scores(0, 0)

    unroll = min(FLASH_UNROLL, n_kv)
    assert unroll % 2 == 0 and n_kv % unroll == 0

    def body(jj, carry):
        for u in range(unroll):
            j = unroll * jj + u
            scores(j + 1, (u + 1) % 2)
            accumulate(j, u % 2)
        return carry

    lax.fori_loop(0, n_kv // unroll, body, 0)
    o = acc_sc[...] / l_sc[...]
    if n_maps == 2:
        lv = lam_ref[...]
        lam = (jnp.exp(jnp.sum(lv[0:1] * lv[1:2], axis=-1, keepdims=True))
               - jnp.exp(jnp.sum(lv[2:3] * lv[3:4], axis=-1, keepdims=True)) + lam_init)
        o = o[:, :tq] - lam * o[:, tq:]
        ms = jnp.mean(o * o, axis=0, keepdims=True)
        o = o * lax.rsqrt(ms + 1e-6) * g_ref[...] * (1.0 - lam_init)
    if dv < LANES:
        o = jnp.concatenate([o, jnp.zeros((LANES - dv, o.shape[1]), f32)], axis=0)
    o_ref[...] = o.T.astype(o_ref.dtype)


def _flash_call(q_arr, k_arr, v_arr, n_heads, q_blk0, k_blk0, v_blk0, n_maps, tq, tk, name,
                lam_vec=None, norm_g=None, lam_init=0.0, dv=LANES):
    B, S, _ = q_arr.shape
    cols = n_maps * tq
    in_specs = [pl.BlockSpec((None, tq, LANES), lambda b, h, i: (b, i, q_blk0 + h)),
                pl.BlockSpec((None, S, LANES), lambda b, h, i: (b, 0, k_blk0 + h)),
                pl.BlockSpec((None, S, LANES), lambda b, h, i: (b, 0, v_blk0 + h))]
    args = [q_arr, k_arr, v_arr]
    if n_maps == 2:
        in_specs += [pl.BlockSpec((8, LANES), lambda b, h, i: (0, 0)),
                     pl.BlockSpec((LANES, 1), lambda b, h, i: (0, 0))]
        args += [lam_vec, norm_g]
    kern = functools.partial(_flash_kernel, n_maps=n_maps, tq=tq, tk=tk, n_kv=S // tk, lam_init=lam_init,
                             dv=dv)
    return pl.pallas_call(
        kern, grid=(B, n_heads, S // tq), in_specs=in_specs,
        out_specs=pl.BlockSpec((None, tq, LANES), lambda b, h, i: (b, i, h)),
        out_shape=jax.ShapeDtypeStruct((B, S, n_heads * LANES), bf16),
        scratch_shapes=[pltpu.VMEM((LANES, cols), bf16), pltpu.VMEM((1, cols), f32), pltpu.VMEM((1, cols), f32),
                        pltpu.VMEM((dv, cols), f32), pltpu.VMEM((2, tk, cols), f32), pltpu.VMEM((2, 1, cols), f32)],
        compiler_params=_cparams(("parallel", "parallel", "arbitrary")), name=name)(*args)


def _mla_q_kernel(cq_ref, g_ref, w_ref, cm_ref, sm_ref, o_ref, *, scale):
    qn = _rms_norm(cq_ref[...], g_ref[...]).astype(bf16)
    cos_t = cm_ref[...]
    sin_t = sm_ref[...]
    lane = lax.broadcasted_iota(jnp.int32, cos_t.shape, 1)
    first_half = lane < MLA_NOPE_DIM + MLA_ROPE_DIM // 2
    for h in range(MLA_HEADS):
        acc = _dot(qn, w_ref[:, h * LANES:(h + 1) * LANES])
        y = _rope_tile(acc, cos_t, sin_t, first_half, MLA_ROPE_DIM // 2) * scale
        o_ref[:, h * LANES:(h + 1) * LANES] = y.astype(o_ref.dtype)


def _mla_kv_kernel(ckv_ref, kr_ref, g_ref, wk_ref, wv_ref, cm_ref, sm_ref, k_ref, v_ref):
    cn = _rms_norm(ckv_ref[...], g_ref[...]).astype(bf16)
    cos_t = cm_ref[...]
    lane = lax.broadcasted_iota(jnp.int32, cos_t.shape, 1)
    first_half = lane < MLA_NOPE_DIM + MLA_ROPE_DIM // 2
    kr = _rope_tile(kr_ref[...], cos_t, sm_ref[...], first_half, MLA_ROPE_DIM // 2)
    for h in range(MLA_HEADS):
        sl = slice(h * LANES, (h + 1) * LANES)
        k_ref[:, sl] = (_dot(cn, wk_ref[:, sl]) + kr).astype(k_ref.dtype)
        v_ref[:, sl] = _dot(cn, wv_ref[:, sl]).astype(v_ref.dtype)


def _mla_proj_calls(hb, qg, kvg, wq, wk, wv, cos_m, sin_m, tm=512):
    T = hb.shape[0]
    W = MLA_HEADS * LANES
    row = lambda width, blk=0: pl.BlockSpec((tm, width), lambda i: (i, blk))
    full = lambda a: pl.BlockSpec(a.shape, lambda i: (0, 0))
    scale = (MLA_NOPE_DIM + MLA_ROPE_DIM) ** -0.5 * LOG2E
    qb = pl.pallas_call(
        functools.partial(_mla_q_kernel, scale=scale), grid=(T // tm,),
        in_specs=[row(MLA_Q_RANK, 1), full(qg), full(wq), row(LANES), row(LANES)],
        out_specs=row(W), out_shape=jax.ShapeDtypeStruct((T, W), bf16),
        compiler_params=_cparams(("parallel",)), name="mla_q")(hb, qg, wq, cos_m, sin_m)
    kb, vb = pl.pallas_call(
        _mla_kv_kernel, grid=(T // tm,),
        in_specs=[row(MLA_KV_RANK, 0), row(LANES, 2), full(kvg), full(wk), full(wv), row(LANES), row(LANES)],
        out_specs=[row(W), row(W)],
        out_shape=[jax.ShapeDtypeStruct((T, W), bf16), jax.ShapeDtypeStruct((T, W), bf16)],
        compiler_params=_cparams(("parallel",)), name="mla_kv")(hb, hb, kvg, wk, wv, cos_m, sin_m)
    return qb, kb, vb


def _dil_kernel(q_ref, kp_ref, kc_ref, kn_ref, vp_ref, vc_ref, vn_ref, o_ref, lse_ref, kcat, vcat, *, tq, sub_len):
    i = pl.program_id(2)
    side = DIL_SIDE
    kcat[0:side, :] = kp_ref[...]
    kcat[side:side + tq, :] = kc_ref[...]
    kcat[side + tq:, :] = kn_ref[...]
    vcat[0:side, :] = vp_ref[...]
    vcat[side:side + tq, :] = vc_ref[...]
    vcat[side + tq:, :] = vn_ref[...]
    qs, ks = 2 * side, 4 * side
    lane = lax.broadcasted_iota(jnp.int32, (qs, LANES), 1)
    ii = lax.broadcasted_iota(jnp.int32, (qs, ks), 0)
    jj = lax.broadcasted_iota(jnp.int32, (qs, ks), 1)
    band = (jj - ii >= 0) & (jj - ii <= 2 * side)
    for sb in range(tq // qs):
        q0 = sb * qs
        pos = jj + (i * tq + q0 - side)
        mask = band & (pos >= 0) & (pos < sub_len)
        lse_tile = jnp.zeros((qs, LANES), f32)
        for h in range(DIL_HEADS):
            t = h // 2
            qt = q_ref[q0:q0 + qs, t * LANES:(t + 1) * LANES]
            keep = (lane < HEAD_DIM) if h % 2 == 0 else (lane >= HEAD_DIM)
            qm = jnp.where(keep, qt, jnp.zeros_like(qt))
            s = _dot_nt(qm, kcat[q0:q0 + ks, t * LANES:(t + 1) * LANES])
            s = jnp.where(mask, s, NEG_INF)
            m = jnp.max(s, axis=-1, keepdims=True)
            p = jnp.exp2(s - m)
            l = jnp.sum(p, axis=-1, keepdims=True)
            o = _dot(p.astype(bf16), vcat[q0:q0 + ks, h * LANES:(h + 1) * LANES]) / l
            o_ref[q0:q0 + qs, h * LANES:(h + 1) * LANES] = o.astype(o_ref.dtype)
            lse_tile = jnp.where(lane == h, m + jnp.log2(l), lse_tile)
        lse_ref[q0:q0 + qs, :] = lse_tile


def _dil_call(hc):
    B, d, L, _ = hc.shape
    tq = min(512, L)
    side = DIL_SIDE
    nblk = tq // side
    last = L // side - 1
    qw, vw = 2 * LANES, 4 * LANES
    prev = lambda i: jnp.maximum(i * nblk - 1, 0)
    nxt = lambda i: jnp.minimum((i + 1) * nblk, last)
    in_specs = [
        pl.BlockSpec((None, None, tq, qw), lambda b, r, i: (b, r, i, 0)),
        pl.BlockSpec((None, None, side, qw), lambda b, r, i: (b, r, prev(i), 1)),
        pl.BlockSpec((None, None, tq, qw), lambda b, r, i: (b, r, i, 1)),
        pl.BlockSpec((None, None, side, qw), lambda b, r, i: (b, r, nxt(i), 1)),
        pl.BlockSpec((None, None, side, vw), lambda b, r, i: (b, r, prev(i), 1)),
        pl.BlockSpec((None, None, tq, vw), lambda b, r, i: (b, r, i, 1)),
        pl.BlockSpec((None, None, side, vw), lambda b, r, i: (b, r, nxt(i), 1)),
    ]
    return pl.pallas_call(
        functools.partial(_dil_kernel, tq=tq, sub_len=L), grid=(B, d, L // tq), in_specs=in_specs,
        out_specs=[pl.BlockSpec((None, None, tq, vw), lambda b, r, i: (b, r, i, 0)),
                   pl.BlockSpec((None, None, tq, LANES), lambda b, r, i: (b, r, i, 0))],
        out_shape=[jax.ShapeDtypeStruct((B, d, L, vw), bf16), jax.ShapeDtypeStruct((B, d, L, LANES), f32)],
        scratch_shapes=[pltpu.VMEM((tq + 2 * side, qw), bf16), pltpu.VMEM((tq + 2 * side, vw), bf16)],
        compiler_params=_cparams(("parallel", "parallel", "arbitrary")),
        name=f"dilated_d{d}")(hc, hc, hc, hc, hc, hc, hc)


def _token_order(ref, stage):
    d, n, w = ref.shape
    if d == 1:
        return ref[0].astype(f32)
    nt = w // LANES
    for r in range(d):
        v = ref[r].astype(f32)
        for t in range(nt):
            stage[t, pl.ds(r, n, stride=d), :] = v[:, t * LANES:(t + 1) * LANES]
    return jnp.concatenate([stage[t] for t in range(nt)], axis=1)


def _merge_kernel(x_ref, oa_ref, ob_ref, o1_ref, o2_ref, o3_ref, l1_ref, l2_ref, l3_ref, g_ref,
                  wa_ref, wb_ref, wc_ref, wo_ref, lg_ref, lb_ref, x32_ref, xb_ref,
                  so2, so3, sl2, sl3, *, alpha):
    l1, l2, l3 = _token_order(l1_ref, None), _token_order(l2_ref, sl2), _token_order(l3_ref, sl3)
    o1, o2, o3 = _token_order(o1_ref, None), _token_order(o2_ref, so2), _token_order(o3_ref, so3)
    mx = jnp.maximum(jnp.maximum(l1, l2), l3)
    e1, e2, e3 = jnp.exp2(l1 - mx), jnp.exp2(l2 - mx), jnp.exp2(l3 - mx)
    den = e1 + e2 + e3
    w1, w2, w3 = e1 / den, e2 / den, e3 / den
    tiles = []
    for h in range(DIL_HEADS):
        sl = slice(h * LANES, (h + 1) * LANES)
        oc = w1[:, h:h + 1] * o1[:, sl] + w2[:, h:h + 1] * o2[:, sl] + w3[:, h:h + 1] * o3[:, sl]
        tiles.append(oc.astype(bf16))
    oc = jnp.concatenate(tiles, axis=1)
    D = D_MODEL
    y = (g_ref[:, 0:D].astype(f32) * _dot(oa_ref[...], wa_ref[...])
         + g_ref[:, D:2 * D].astype(f32) * _dot(ob_ref[...], wb_ref[...])
         + g_ref[:, 2 * D:3 * D].astype(f32) * _dot(oc, wc_ref[...]))
    mix = _dot(y.astype(bf16), wo_ref[...])
    out = _layer_norm(alpha * x_ref[...] + mix, lg_ref[...], lb_ref[...])
    x32_ref[...] = out
    xb_ref[...] = out.astype(bf16)


def _merge_call(x32, oa, ob, dil_o, dil_l, hg, wa, wb, wc, wo, lg, lb, alpha, B, S, tm=256):
    T, D = x32.shape
    nb = S // tm
    row = lambda a: pl.BlockSpec((tm, a.shape[1]), lambda b, i: (b * nb + i, 0))
    cls = lambda a: pl.BlockSpec((None, a.shape[1], tm // a.shape[1], a.shape[3]), lambda b, i: (b, 0, i, 0))
    full = lambda a: pl.BlockSpec(a.shape, lambda b, i: (0, 0))
    consts = [wa, wb, wc, wo, lg, lb]
    ow, lw = dil_o[0].shape[3], dil_l[0].shape[3]
    return pl.pallas_call(
        functools.partial(_merge_kernel, alpha=alpha), grid=(B, nb),
        in_specs=[row(x32), row(oa), row(ob)] + [cls(a) for a in dil_o] + [cls(a) for a in dil_l] + [row(hg)]
        + [full(a) for a in consts],
        out_specs=[pl.BlockSpec((tm, D), lambda b, i: (b * nb + i, 0))] * 2,
        out_shape=[jax.ShapeDtypeStruct((T, D), f32), jax.ShapeDtypeStruct((T, D), bf16)],
        scratch_shapes=[pltpu.VMEM((ow // LANES, tm, LANES), f32), pltpu.VMEM((ow // LANES, tm, LANES), f32),
                        pltpu.VMEM((lw // LANES, tm, LANES), f32), pltpu.VMEM((lw // LANES, tm, LANES), f32)],
        compiler_params=_cparams(("parallel", "parallel")), name="merge")(
            x32, oa, ob, *dil_o, *dil_l, hg, *consts)


def _first_true(flags):
    out, seen = [], None
    for f in flags:
        out.append(f if seen is None else f & ~seen)
        seen = f if seen is None else seen | f
    return out


def _router_kernel(x_ref, rw_ref, rb_ref, gate_ref):
    logits = _dot_nt(rw_ref[...], x_ref[...])
    scores = jax.nn.sigmoid(logits)
    biased = scores + rb_ref[...]
    sc = [scores[e:e + 1, :] for e in range(N_EXPERTS)]
    bi = [biased[e:e + 1, :] for e in range(N_EXPERTS)]
    npg = EXPERTS_PER_GROUP
    gscore = []
    for g in range(N_EXPERT_GROUPS):
        v = bi[g * npg:(g + 1) * npg]
        best = None
        for a in range(npg):
            for b in range(a + 1, npg):
                best = v[a] + v[b] if best is None else jnp.maximum(best, v[a] + v[b])
        gscore.append(best)
    gbest = functools.reduce(jnp.maximum, gscore)
    gsel = _first_true([gs == gbest for gs in gscore])
    zero = jnp.zeros_like(gbest)
    bsel = [functools.reduce(lambda x, y: x + y, [jnp.where(gsel[g], bi[g * npg + j], zero)
                                                  for g in range(N_EXPERT_GROUPS)]) for j in range(npg)]
    ssel = [functools.reduce(lambda x, y: x + y, [jnp.where(gsel[g], sc[g * npg + j], zero)
                                                  for g in range(N_EXPERT_GROUPS)]) for j in range(npg)]
    t1 = functools.reduce(jnp.maximum, bsel)
    i1 = _first_true([b == t1 for b in bsel])
    rest = [jnp.where(i1[j], -jnp.inf, bsel[j]) for j in range(npg)]
    t2 = functools.reduce(jnp.maximum, rest)
    i2 = _first_true([rest[j] == t2 for j in range(npg)])
    w1 = functools.reduce(lambda x, y: x + y, [jnp.where(i1[j], ssel[j], zero) for j in range(npg)])
    w2 = functools.reduce(lambda x, y: x + y, [jnp.where(i2[j], ssel[j], zero) for j in range(npg)])
    den = w1 + w2
    local = [jnp.where(i1[j], w1 / den, zero) + jnp.where(i2[j], w2 / den, zero) for j in range(npg)]
    rows = [jnp.where(gsel[e // npg], local[e % npg], zero) for e in range(N_EXPERTS)]
    gate_ref[...] = jnp.concatenate(rows, axis=0)


def _router_call(xb, rw_t, rb, tm=1024):
    T, D = xb.shape
    return pl.pallas_call(
        _router_kernel, grid=(T // tm,),
        in_specs=[pl.BlockSpec((tm, D), lambda i: (i, 0)), pl.BlockSpec((N_EXPERTS, D), lambda i: (0, 0)),
                  pl.BlockSpec((N_EXPERTS, 1), lambda i: (0, 0))],
        out_specs=pl.BlockSpec((N_EXPERTS, tm), lambda i: (0, i)),
        out_shape=jax.ShapeDtypeStruct((N_EXPERTS, T), f32),
        compiler_params=_cparams(("parallel",)), name="router")(xb, rw_t, rb)


def _moe_kernel(x32_ref, xb_ref, gate_ref, wg_ref, wu_ref, wd_ref, wsg_ref, wsu_ref, wsd_ref, lg_ref, lb_ref,
                o32_ref, ob_ref, acc_sc, *, alpha, n_chunks, per_chunk):
    c = pl.program_id(1)
    x = xb_ref[...]

    @pl.when(c == 0)
    def _():
        hs = jax.nn.silu(_dot(x, wsg_ref[...])) * _dot(x, wsu_ref[...])
        acc_sc[...] = _dot(hs.astype(bf16), wsd_ref[...])

    gate = gate_ref[...]
    hid = []
    for e in range(per_chunk):
        sl = slice(e * EXPERT_HIDDEN, (e + 1) * EXPERT_HIDDEN)
        h = jax.nn.silu(_dot(x, wg_ref[:, sl])) * _dot(x, wu_ref[:, sl]) * gate[:, e:e + 1]
        hid.append(h.astype(bf16))
    acc_sc[...] += _dot(jnp.concatenate(hid, axis=1), wd_ref[...])

    @pl.when(c == n_chunks - 1)
    def _():
        out = _layer_norm(alpha * x32_ref[...] + acc_sc[...], lg_ref[...], lb_ref[...])
        o32_ref[...] = out
        ob_ref[...] = out.astype(bf16)


def _moe_call(x32, xb, gate_c, wg, wu, wd, wsg, wsu, wsd, lg, lb, alpha, tm=1024, per_chunk=4):
    T, D = x32.shape
    n_chunks = N_EXPERTS // per_chunk
    cw = per_chunk * EXPERT_HIDDEN
    row = pl.BlockSpec((tm, D), lambda i, c: (i, 0))
    full = lambda a: pl.BlockSpec(a.shape, lambda i, c: (0, 0))
    return pl.pallas_call(
        functools.partial(_moe_kernel, alpha=alpha, n_chunks=n_chunks, per_chunk=per_chunk),
        grid=(T // tm, n_chunks),
        in_specs=[row, row, pl.BlockSpec((None, tm, per_chunk), lambda i, c: (c, i, 0)),
                  pl.BlockSpec((D, cw), lambda i, c: (0, c)), pl.BlockSpec((D, cw), lambda i, c: (0, c)),
                  pl.BlockSpec((cw, D), lambda i, c: (c, 0)),
                  full(wsg), full(wsu), full(wsd), full(lg), full(lb)],
        out_specs=[row, row],
        out_shape=[jax.ShapeDtypeStruct((T, D), f32), jax.ShapeDtypeStruct((T, D), bf16)],
        scratch_shapes=[pltpu.VMEM((tm, D), f32)],
        compiler_params=_cparams(("parallel", "arbitrary")), name="moe")(
            x32, xb, gate_c, wg, wu, wd, wsg, wsu, wsd, lg, lb)


def _rope_tables(positions):
    pos = positions.astype(f32).reshape(-1, 1)
    T = pos.shape[0]
    ones, zeros = jnp.ones, jnp.zeros
    inv_p = ROPE_THETA ** (-jnp.arange(0, ROT_DIM, 2, dtype=f32) / ROT_DIM)
    ang = pos * inv_p
    c, s = jnp.cos(ang), jnp.sin(ang)
    pad = HEAD_DIM - ROT_DIM
    cos_p = jnp.tile(jnp.concatenate([c, c, ones((T, pad), f32)], axis=1), (1, 2))
    sin_p = jnp.tile(jnp.concatenate([-s, s, zeros((T, pad), f32)], axis=1), (1, 2))
    inv_m = ROPE_THETA ** (-jnp.arange(0, MLA_ROPE_DIM, 2, dtype=f32) / MLA_ROPE_DIM)
    ang = pos * inv_m
    c, s = jnp.cos(ang), jnp.sin(ang)
    tail = LANES - MLA_NOPE_DIM - MLA_ROPE_DIM
    cos_m = jnp.concatenate([ones((T, MLA_NOPE_DIM), f32), c, c, ones((T, tail), f32)], axis=1)
    sin_m = jnp.concatenate([zeros((T, MLA_NOPE_DIM), f32), -s, s, zeros((T, tail), f32)], axis=1)
    return cos_p, sin_p, cos_m, sin_m


def _pack_w_in(w):
    offs = [0]
    for sz in IN_SIZES:
        offs.append(offs[-1] + sz)
    a_q, a_k, a_v, b_cq, b_ckv, b_kr, c_q, c_k, c_v, g = [w[:, offs[i]:offs[i + 1]] for i in range(len(IN_SIZES))]
    D = w.shape[0]
    z = lambda n: jnp.zeros((D, n), w.dtype)
    kr_pad = jnp.concatenate([z(MLA_NOPE_DIM), b_kr, z(LANES - MLA_NOPE_DIM - MLA_ROPE_DIM)], axis=1)
    cols = [a_q, a_k, a_v, b_ckv, kr_pad, b_cq]
    for gi in range(DIL_GROUPS):
        cols += [c_q[:, 256 * gi:256 * (gi + 1)], c_k[:, 256 * gi:256 * (gi + 1)], c_v[:, 512 * gi:512 * (gi + 1)]]
    cols.append(g)
    return jnp.concatenate(cols, axis=1).astype(bf16)


def _pad_heads(w, n_heads, lo, hi):
    K = w.shape[0]
    per = w.shape[1] // n_heads
    wh = w.reshape(K, n_heads, per)[:, :, lo:hi]
    wh = jnp.pad(wh, ((0, 0), (0, 0), (0, LANES - (hi - lo))))
    return wh.reshape(K, n_heads * LANES).astype(bf16)


def kernel(x, positions, ln_in_g, ln_in_b, w_in, b_gate, lam_q1, lam_k1, lam_q2, lam_k2, diff_norm_g, mla_q_norm_g, mla_kv_norm_g, w_mla_qb, w_mla_kvb, w_branch_a, w_branch_b, w_branch_c, w_out, ln1_g, ln1_b, router_w, router_bias, w_exp_gate, w_exp_up, w_exp_down, w_sh_gate, w_sh_up, w_sh_down, ln2_g, ln2_b):
    B, S, D = x.shape
    T = B * S
    depth = w_in.shape[0]
    alpha = (2 * depth) ** 0.25
    cos_p, sin_p, cos_m, sin_m = _rope_tables(positions)
    x32, xb = _ln_call(x.reshape(T, D), ln_in_g.reshape(1, D), ln_in_b.reshape(1, D))
    rw_t = router_w.T.astype(bf16)
    rb = router_bias.reshape(N_EXPERTS, 1).astype(f32)
    vec = lambda v: v.reshape(1, -1).astype(f32)
    for l in range(depth):
        lam_init = 0.8 - 0.6 * math.exp(-0.3 * l)
        ha, hb, hc0, hc1, hc2, hg = _inproj_call(xb, _pack_w_in(w_in[l]), vec(b_gate[l]), cos_p, sin_p, B, S)
        lam_vec = jnp.pad(jnp.stack([lam_q1[l], lam_k1[l], lam_q2[l], lam_k2[l]]).astype(f32),
                          ((0, 4), (0, LANES - HEAD_DIM)))
        ha3 = ha.reshape(B, S, HA_W)
        oa = _flash_call(ha3, ha3, ha3, DIFF_HEADS, 0, DIFF_HEADS, 2 * DIFF_HEADS, n_maps=2, tq=FLASH_COLS // 2, tk=FLASH_TK,
                         name="diff_attn", lam_vec=lam_vec, norm_g=diff_norm_g[l].reshape(-1, 1).astype(f32),
                         lam_init=lam_init)
        per_q = MLA_NOPE_DIM + MLA_ROPE_DIM
        per_kv = MLA_NOPE_DIM + MLA_V_DIM
        qb, kb, vb = _mla_proj_calls(
            hb, vec(mla_q_norm_g[l]), vec(mla_kv_norm_g[l]), _pad_heads(w_mla_qb[l], MLA_HEADS, 0, per_q),
            _pad_heads(w_mla_kvb[l], MLA_HEADS, 0, MLA_NOPE_DIM), _pad_heads(w_mla_kvb[l], MLA_HEADS, MLA_NOPE_DIM, per_kv),
            cos_m, sin_m)
        ob = _flash_call(qb.reshape(B, S, -1), kb.reshape(B, S, -1), vb.reshape(B, S, -1), MLA_HEADS, 0, 0, 0,
                         n_maps=1, tq=FLASH_COLS, tk=FLASH_TK, name="mla_attn", dv=MLA_V_DIM)
        dil = [_dil_call(hc) for hc in (hc0, hc1, hc2)]
        wb_pad = jnp.pad(w_branch_b[l].reshape(MLA_HEADS, MLA_V_DIM, D),
                         ((0, 0), (0, LANES - MLA_V_DIM), (0, 0))).reshape(MLA_HEADS * LANES, D).astype(bf16)
        x32, xb = _merge_call(
            x32, oa.reshape(T, -1), ob.reshape(T, -1), [o for o, _ in dil], [lse for _, lse in dil],
            hg, w_branch_a[l].astype(bf16), wb_pad, w_branch_c[l].astype(bf16), w_out[l].astype(bf16),
            vec(ln1_g[l]), vec(ln1_b[l]), alpha, B, S)
        gate_t = _router_call(xb, rw_t, rb)
        per_chunk = 4
        gate_c = gate_t.reshape(N_EXPERTS // per_chunk, per_chunk, T).transpose(0, 2, 1)
        H = EXPERT_HIDDEN
        wg = w_exp_gate[l].transpose(1, 0, 2).reshape(D, N_EXPERTS * H).astype(bf16)
        wu = w_exp_up[l].transpose(1, 0, 2).reshape(D, N_EXPERTS * H).astype(bf16)
        wd = w_exp_down[l].reshape(N_EXPERTS * H, D).astype(bf16)
        x32, xb = _moe_call(x32, xb, gate_c, wg, wu, wd, w_sh_gate[l].astype(bf16), w_sh_up[l].astype(bf16),
                            w_sh_down[l].astype(bf16), vec(ln2_g[l]), vec(ln2_b[l]), alpha, per_chunk=per_chunk)
    return x32.reshape(B, S, D)
```

```python
import functools
import math

import jax
import jax.numpy as jnp
from jax import lax
from jax.experimental import pallas as pl
from jax.experimental.pallas import tpu as pltpu

f32 = jnp.float32
bf16 = jnp.bfloat16

D_MODEL = 1024
HEAD_DIM = 64
ROPE_THETA = 500000.0
ROT_DIM = HEAD_DIM // 4
DIFF_HEADS = 4
MLA_HEADS = 8
MLA_Q_RANK = 384
MLA_KV_RANK = 256
MLA_NOPE_DIM = 64
MLA_ROPE_DIM = 32
MLA_V_DIM = 64
DIL_PATTERNS = ((128, 1), (512, 4), (2048, 16))
DIL_GROUPS = 3
DIL_HEADS = 4
DIL_SIDE = 64
N_EXPERTS = 16
N_EXPERT_GROUPS = 4
EXPERTS_PER_GROUP = 4
EXPERT_HIDDEN = 256
IN_SIZES = (512, 512, 512, 384, 256, 32, 768, 768, 1536, 3072)
NEG_INF = -1e30
LOG2E = math.log2(math.e)

LANES = 128
VMEM_LIMIT = 52 * 1024 * 1024

HA_W = 1536
HB_W = 768
HCG_W = 1024
HC_W = 3 * HCG_W
HG_W = 3072

FLASH_TK = 512
FLASH_COLS = 1024
FLASH_UNROLL = 8


def _cparams(sem):
    return pltpu.CompilerParams(dimension_semantics=sem, vmem_limit_bytes=VMEM_LIMIT)


def _layer_norm(z, g, b):
    mu = jnp.mean(z, axis=-1, keepdims=True)
    zc = z - mu
    var = jnp.mean(zc * zc, axis=-1, keepdims=True)
    return zc * lax.rsqrt(var + 1e-5) * g + b


def _rms_norm(z, g):
    return z * lax.rsqrt(jnp.mean(z * z, axis=-1, keepdims=True) + 1e-6) * g


def _dot(a, b):
    return jnp.dot(a, b, preferred_element_type=f32)


def _dot_nt(a, b):
    return lax.dot_general(a, b, (((1,), (1,)), ((), ())), preferred_element_type=f32)


def _rope_tile(xt, cos_t, sin_t, first_half, shift):
    xr = jnp.where(first_half, pltpu.roll(xt, LANES - shift, 1), pltpu.roll(xt, shift, 1))
    return xt * cos_t + xr * sin_t


def _ln_kernel(x_ref, g_ref, b_ref, o32_ref, ob_ref):
    y = _layer_norm(x_ref[...], g_ref[...], b_ref[...])
    o32_ref[...] = y
    ob_ref[...] = y.astype(bf16)


def _ln_call(x, g, b, tm=512):
    T, D = x.shape
    row = pl.BlockSpec((tm, D), lambda i: (i, 0))
    vec = pl.BlockSpec((1, D), lambda i: (0, 0))
    return pl.pallas_call(
        _ln_kernel, grid=(T // tm,), in_specs=[row, vec, vec], out_specs=[row, row],
        out_shape=[jax.ShapeDtypeStruct((T, D), f32), jax.ShapeDtypeStruct((T, D), bf16)],
        compiler_params=_cparams(("parallel",)), name="ln_in")(x, g, b)


def _inproj_plan():
    qscale = HEAD_DIM ** -0.5 * LOG2E
    plan = []
    for c in range(0, 512, 256):
        plan.append((c, 256, "rope", 0, c, qscale))
    for c in range(512, 1024, 256):
        plan.append((c, 256, "rope", 0, c, 1.0))
    plan.append((1024, 512, "plain", 0, 1024, 1.0))
    plan.append((HA_W, HB_W, "plain", 1, 0, 1.0))
    base = HA_W + HB_W
    for g in range(DIL_GROUPS):
        o = HCG_W * g
        plan.append((base + o, 256, "rope", 2 + g, 0, qscale))
        plan.append((base + o + 256, 256, "rope", 2 + g, 256, 1.0))
        plan.append((base + o + 512, 512, "plain", 2 + g, 512, 1.0))
    base = HA_W + HB_W + HC_W
    for c in range(0, HG_W, 512):
        plan.append((base + c, 512, "gate", 2 + DIL_GROUPS, c, 1.0))
    return plan


def _inproj_kernel(x_ref, w_ref, bg_ref, cp_ref, sp_ref, ha_ref, hb_ref, hc0_ref, hc1_ref, hc2_ref, hg_ref, stage):
    outs = (ha_ref, hb_ref, hc0_ref, hc1_ref, hc2_ref, hg_ref)
    x = x_ref[...]
    tm = x.shape[0]
    cos_t = cp_ref[...]
    sin_t = sp_ref[...]
    lane = lax.broadcasted_iota(jnp.int32, cos_t.shape, 1)
    first_half = (lane % HEAD_DIM) < (ROT_DIM // 2)
    for (c0, w, kind, oi, o0, scale) in _inproj_plan():
        acc = _dot(x, w_ref[:, c0:c0 + w])
        if kind == "rope":
            tiles = []
            for t in range(w // LANES):
                y = _rope_tile(acc[:, t * LANES:(t + 1) * LANES], cos_t, sin_t, first_half, ROT_DIM // 2)
                tiles.append(y * scale if scale != 1.0 else y)
            acc = jnp.concatenate(tiles, axis=1)
        elif kind == "gate":
            acc = jax.nn.sigmoid(acc + bg_ref[:, o0:o0 + w])
        out = outs[oi]
        if 2 <= oi < 2 + DIL_GROUPS:
            d = DIL_PATTERNS[oi - 2][1]
            if d == 1:
                out[0, :, o0:o0 + w] = acc.astype(out.dtype)
            else:
                nt = w // LANES
                for t in range(nt):
                    stage[t] = acc[:, t * LANES:(t + 1) * LANES]
                for r in range(d):
                    rows = [stage[t, pl.ds(r, tm // d, stride=d), :] for t in range(nt)]
                    out[r, :, o0:o0 + w] = jnp.concatenate(rows, axis=1).astype(out.dtype)
        else:
            out[:, o0:o0 + w] = acc.astype(out.dtype)


def _inproj_call(xb, w, bg, cos_t, sin_t, B, S, tm=256):
    T, D = xb.shape
    NW = w.shape[1]
    nb = S // tm
    row = lambda width: pl.BlockSpec((tm, width), lambda b, i: (b * nb + i, 0))
    const = lambda shape: pl.BlockSpec(shape, lambda b, i: (0, 0))
    hc_specs, hc_shapes = [], []
    for _, d in DIL_PATTERNS:
        hc_specs.append(pl.BlockSpec((None, d, tm // d, HCG_W), lambda b, i: (b, 0, i, 0)))
        hc_shapes.append(jax.ShapeDtypeStruct((B, d, S // d, HCG_W), bf16))
    return pl.pallas_call(
        _inproj_kernel, grid=(B, nb),
        in_specs=[row(D), const((D, NW)), const((1, HG_W)), row(LANES), row(LANES)],
        out_specs=[row(HA_W), row(HB_W)] + hc_specs + [row(HG_W)],
        out_shape=[jax.ShapeDtypeStruct((T, HA_W), bf16), jax.ShapeDtypeStruct((T, HB_W), f32)] + hc_shapes
        + [jax.ShapeDtypeStruct((T, HG_W), bf16)],
        scratch_shapes=[pltpu.VMEM((4, tm, LANES), f32)],
        compiler_params=_cparams(("parallel", "parallel")), name="inproj")(xb, w, bg, cos_t, sin_t)


def _flash_kernel(*refs, n_maps, tq, tk, n_kv, lam_init, dv):
    if n_maps == 2:
        q_ref, qn_ref, k_ref, v_ref, lam_ref, g_ref, o_ref, qt_sc, m_sc, l_sc, acc_sc, s_sc, mc_sc = refs
    else:
        q_ref, qn_ref, k_ref, v_ref, o_ref, qt_sc, m_sc, l_sc, acc_sc, s_sc, mc_sc = refs
    i = pl.program_id(2)
    cur = i % 2
    nxt = 1 - cur

    def load_queries(ref, slot):
        q = ref[...].astype(f32)
        if n_maps == 2:
            lane = lax.broadcasted_iota(jnp.int32, q.shape, 1)
            zero = jnp.zeros_like(q)
            q = jnp.concatenate([jnp.where(lane < HEAD_DIM, q, zero), jnp.where(lane >= HEAD_DIM, q, zero)], axis=0)
        qt_sc[slot] = q.T.astype(bf16)

    def scores(j, buf, slot):
        off = pl.multiple_of(j * tk, tk)
        s = _dot(k_ref[pl.ds(off, tk), :], qt_sc[slot])
        s_sc[buf] = s
        mc_sc[buf] = jnp.max(s, axis=0, keepdims=True)

    def accumulate(j, buf):
        off = pl.multiple_of(j * tk, tk)
        m_prev = m_sc[...]
        m_new = jnp.maximum(m_prev, mc_sc[buf])
        p = jnp.exp2(s_sc[buf] - m_new)
        a = jnp.exp2(m_prev - m_new)
        l_sc[...] = a * l_sc[...] + jnp.sum(p, axis=0, keepdims=True)
        pv = lax.dot_general(v_ref[pl.ds(off, tk), 0:dv], p.astype(bf16), (((0,), (0,)), ((), ())),
                             preferred_element_type=f32)
        acc_sc[...] = a * acc_sc[...] + pv
        m_sc[...] = m_new

    @pl.when(i == 0)
    def _():
        load_queries(q_ref, cur)
        scores(0, 0, cur)

    load_queries(qn_ref, nxt)
    m_sc[...] = jnp.full(m_sc.shape, -jnp.inf, f32)
    l_sc[...] = jnp.zeros(l_sc.shape, f32)
    acc_sc[...] = jnp.zeros(acc_sc.shape, f32)

    unroll = min(FLASH_UNROLL, n_kv)
    assert unroll % 2 == 0 and n_kv % unroll == 0

    def body(jj, carry):
        for u in range(unroll):
            j = unroll * jj + u
            scores(j + 1, (u + 1) % 2, cur)
            accumulate(j, u % 2)
        return carry

    lax.fori_loop(0, n_kv // unroll - 1, body, 0)
    for j in range(n_kv - unroll, n_kv):
        if j + 1 < n_kv:
            scores(j + 1, (j + 1) % 2, cur)
        else:
            scores(0, 0, nxt)
        accumulate(j, j % 2)
    o = acc_sc[...] / l_sc[...]
    if n_maps == 2:
        lv = lam_ref[...]
        lam = (jnp.exp(jnp.sum(lv[0:1] * lv[1:2], axis=-1, keepdims=True))
               - jnp.exp(jnp.sum(lv[2:3] * lv[3:4], axis=-1, keepdims=True)) + lam_init)
        o = o[:, :tq] - lam * o[:, tq:]
        ms = jnp.mean(o * o, axis=0, keepdims=True)
        o = o * lax.rsqrt(ms + 1e-6) * g_ref[...] * (1.0 - lam_init)
    if dv < LANES:
        o = jnp.concatenate([o, jnp.zeros((LANES - dv, o.shape[1]), f32)], axis=0)
    o_ref[...] = o.T.astype(o_ref.dtype)


def _flash_call(q_arr, k_arr, v_arr, n_heads, q_blk0, k_blk0, v_blk0, n_maps, tq, tk, name,
                lam_vec=None, norm_g=None, lam_init=0.0, dv=LANES):
    B, S, _ = q_arr.shape
    cols = n_maps * tq
    last = S // tq - 1
    in_specs = [pl.BlockSpec((None, tq, LANES), lambda b, h, i: (b, i, q_blk0 + h)),
                pl.BlockSpec((None, tq, LANES), lambda b, h, i: (b, jnp.minimum(i + 1, last), q_blk0 + h)),
                pl.BlockSpec((None, S, LANES), lambda b, h, i: (b, 0, k_blk0 + h)),
                pl.BlockSpec((None, S, LANES), lambda b, h, i: (b, 0, v_blk0 + h))]
    args = [q_arr, q_arr, k_arr, v_arr]
    if n_maps == 2:
        in_specs += [pl.BlockSpec((8, LANES), lambda b, h, i: (0, 0)),
                     pl.BlockSpec((LANES, 1), lambda b, h, i: (0, 0))]
        args += [lam_vec, norm_g]
    kern = functools.partial(_flash_kernel, n_maps=n_maps, tq=tq, tk=tk, n_kv=S // tk, lam_init=lam_init,
                             dv=dv)
    return pl.pallas_call(
        kern, grid=(B, n_heads, S // tq), in_specs=in_specs,
        out_specs=pl.BlockSpec((None, tq, LANES), lambda b, h, i: (b, i, h)),
        out_shape=jax.ShapeDtypeStruct((B, S, n_heads * LANES), bf16),
        scratch_shapes=[pltpu.VMEM((2, LANES, cols), bf16), pltpu.VMEM((1, cols), f32), pltpu.VMEM((1, cols), f32),
                        pltpu.VMEM((dv, cols), f32), pltpu.VMEM((2, tk, cols), f32), pltpu.VMEM((2, 1, cols), f32)],
        compiler_params=_cparams(("parallel", "parallel", "arbitrary")), name=name)(*args)


def _mla_q_kernel(cq_ref, g_ref, w_ref, cm_ref, sm_ref, o_ref, *, scale):
    qn = _rms_norm(cq_ref[...], g_ref[...]).astype(bf16)
    cos_t = cm_ref[...]
    sin_t = sm_ref[...]
    lane = lax.broadcasted_iota(jnp.int32, cos_t.shape, 1)
    first_half = lane < MLA_NOPE_DIM + MLA_ROPE_DIM // 2
    for h in range(MLA_HEADS):
        acc = _dot(qn, w_ref[:, h * LANES:(h + 1) * LANES])
        y = _rope_tile(acc, cos_t, sin_t, first_half, MLA_ROPE_DIM // 2) * scale
        o_ref[:, h * LANES:(h + 1) * LANES] = y.astype(o_ref.dtype)


def _mla_kv_kernel(ckv_ref, kr_ref, g_ref, wk_ref, wv_ref, cm_ref, sm_ref, k_ref, v_ref):
    cn = _rms_norm(ckv_ref[...], g_ref[...]).astype(bf16)
    cos_t = cm_ref[...]
    lane = lax.broadcasted_iota(jnp.int32, cos_t.shape, 1)
    first_half = lane < MLA_NOPE_DIM + MLA_ROPE_DIM // 2
    kr = _rope_tile(kr_ref[...], cos_t, sm_ref[...], first_half, MLA_ROPE_DIM // 2)
    for h in range(MLA_HEADS):
        sl = slice(h * LANES, (h + 1) * LANES)
        k_ref[:, sl] = (_dot(cn, wk_ref[:, sl]) + kr).astype(k_ref.dtype)
        v_ref[:, sl] = _dot(cn, wv_ref[:, sl]).astype(v_ref.dtype)


def _mla_proj_calls(hb, qg, kvg, wq, wk, wv, cos_m, sin_m, tm=512):
    T = hb.shape[0]
    W = MLA_HEADS * LANES
    row = lambda width, blk=0: pl.BlockSpec((tm, width), lambda i: (i, blk))
    full = lambda a: pl.BlockSpec(a.shape, lambda i: (0, 0))
    scale = (MLA_NOPE_DIM + MLA_ROPE_DIM) ** -0.5 * LOG2E
    qb = pl.pallas_call(
        functools.partial(_mla_q_kernel, scale=scale), grid=(T // tm,),
        in_specs=[row(MLA_Q_RANK, 1), full(qg), full(wq), row(LANES), row(LANES)],
        out_specs=row(W), out_shape=jax.ShapeDtypeStruct((T, W), bf16),
        compiler_params=_cparams(("parallel",)), name="mla_q")(hb, qg, wq, cos_m, sin_m)
    kb, vb = pl.pallas_call(
        _mla_kv_kernel, grid=(T // tm,),
        in_specs=[row(MLA_KV_RANK, 0), row(LANES, 2), full(kvg), full(wk), full(wv), row(LANES), row(LANES)],
        out_specs=[row(W), row(W)],
        out_shape=[jax.ShapeDtypeStruct((T, W), bf16), jax.ShapeDtypeStruct((T, W), bf16)],
        compiler_params=_cparams(("parallel",)), name="mla_kv")(hb, hb, kvg, wk, wv, cos_m, sin_m)
    return qb, kb, vb


def _dil_kernel(q_ref, kp_ref, kc_ref, kn_ref, vp_ref, vc_ref, vn_ref, o_ref, lse_ref, kcat, vcat, *, tq, sub_len):
    i = pl.program_id(2)
    side = DIL_SIDE
    kcat[0:side, :] = kp_ref[...]
    kcat[side:side + tq, :] = kc_ref[...]
    kcat[side + tq:, :] = kn_ref[...]
    vcat[0:side, :] = vp_ref[...]
    vcat[side:side + tq, :] = vc_ref[...]
    vcat[side + tq:, :] = vn_ref[...]
    qs, ks = 2 * side, 4 * side
    lane = lax.broadcasted_iota(jnp.int32, (qs, LANES), 1)
    ii = lax.broadcasted_iota(jnp.int32, (qs, ks), 0)
    jj = lax.broadcasted_iota(jnp.int32, (qs, ks), 1)
    band = (jj - ii >= 0) & (jj - ii <= 2 * side)
    for sb in range(tq // qs):
        q0 = sb * qs
        pos = jj + (i * tq + q0 - side)
        mask = band & (pos >= 0) & (pos < sub_len)
        lse_tile = jnp.zeros((qs, LANES), f32)
        for h in range(DIL_HEADS):
            t = h // 2
            qt = q_ref[q0:q0 + qs, t * LANES:(t + 1) * LANES]
            keep = (lane < HEAD_DIM) if h % 2 == 0 else (lane >= HEAD_DIM)
            qm = jnp.where(keep, qt, jnp.zeros_like(qt))
            s = _dot_nt(qm, kcat[q0:q0 + ks, t * LANES:(t + 1) * LANES])
            s = jnp.where(mask, s, NEG_INF)
            m = jnp.max(s, axis=-1, keepdims=True)
            p = jnp.exp2(s - m)
            l = jnp.sum(p, axis=-1, keepdims=True)
            o = _dot(p.astype(bf16), vcat[q0:q0 + ks, h * LANES:(h + 1) * LANES]) / l
            o_ref[q0:q0 + qs, h * LANES:(h + 1) * LANES] = o.astype(o_ref.dtype)
            lse_tile = jnp.where(lane == h, m + jnp.log2(l), lse_tile)
        lse_ref[q0:q0 + qs, :] = lse_tile


def _dil_call(hc):
    B, d, L, _ = hc.shape
    tq = min(512, L)
    side = DIL_SIDE
    nblk = tq // side
    last = L // side - 1
    qw, vw = 2 * LANES, 4 * LANES
    prev = lambda i: jnp.maximum(i * nblk - 1, 0)
    nxt = lambda i: jnp.minimum((i + 1) * nblk, last)
    in_specs = [
        pl.BlockSpec((None, None, tq, qw), lambda b, r, i: (b, r, i, 0)),
        pl.BlockSpec((None, None, side, qw), lambda b, r, i: (b, r, prev(i), 1)),
        pl.BlockSpec((None, None, tq, qw), lambda b, r, i: (b, r, i, 1)),
        pl.BlockSpec((None, None, side, qw), lambda b, r, i: (b, r, nxt(i), 1)),
        pl.BlockSpec((None, None, side, vw), lambda b, r, i: (b, r, prev(i), 1)),
        pl.BlockSpec((None, None, tq, vw), lambda b, r, i: (b, r, i, 1)),
        pl.BlockSpec((None, None, side, vw), lambda b, r, i: (b, r, nxt(i), 1)),
    ]
    return pl.pallas_call(
        functools.partial(_dil_kernel, tq=tq, sub_len=L), grid=(B, d, L // tq), in_specs=in_specs,
        out_specs=[pl.BlockSpec((None, None, tq, vw), lambda b, r, i: (b, r, i, 0)),
                   pl.BlockSpec((None, None, tq, LANES), lambda b, r, i: (b, r, i, 0))],
        out_shape=[jax.ShapeDtypeStruct((B, d, L, vw), bf16), jax.ShapeDtypeStruct((B, d, L, LANES), f32)],
        scratch_shapes=[pltpu.VMEM((tq + 2 * side, qw), bf16), pltpu.VMEM((tq + 2 * side, vw), bf16)],
        compiler_params=_cparams(("parallel", "parallel", "arbitrary")),
        name=f"dilated_d{d}")(hc, hc, hc, hc, hc, hc, hc)


def _token_order(ref, stage):
    d, n, w = ref.shape
    if d == 1:
        return ref[0].astype(f32)
    nt = w // LANES
    for r in range(d):
        v = ref[r].astype(f32)
        for t in range(nt):
            stage[t, pl.ds(r, n, stride=d), :] = v[:, t * LANES:(t + 1) * LANES]
    return jnp.concatenate([stage[t] for t in range(nt)], axis=1)


def _merge_kernel(x_ref, oa_ref, ob_ref, o1_ref, o2_ref, o3_ref, l1_ref, l2_ref, l3_ref, g_ref,
                  wa_ref, wb_ref, wc_ref, wo_ref, lg_ref, lb_ref, x32_ref, xb_ref,
                  so2, so3, sl2, sl3, *, alpha):
    l1, l2, l3 = _token_order(l1_ref, None), _token_order(l2_ref, sl2), _token_order(l3_ref, sl3)
    o1, o2, o3 = _token_order(o1_ref, None), _token_order(o2_ref, so2), _token_order(o3_ref, so3)
    mx = jnp.maximum(jnp.maximum(l1, l2), l3)
    e1, e2, e3 = jnp.exp2(l1 - mx), jnp.exp2(l2 - mx), jnp.exp2(l3 - mx)
    den = e1 + e2 + e3
    w1, w2, w3 = e1 / den, e2 / den, e3 / den
    tiles = []
    for h in range(DIL_HEADS):
        sl = slice(h * LANES, (h + 1) * LANES)
        oc = w1[:, h:h + 1] * o1[:, sl] + w2[:, h:h + 1] * o2[:, sl] + w3[:, h:h + 1] * o3[:, sl]
        tiles.append(oc.astype(bf16))
    oc = jnp.concatenate(tiles, axis=1)
    D = D_MODEL
    y = (g_ref[:, 0:D].astype(f32) * _dot(oa_ref[...], wa_ref[...])
         + g_ref[:, D:2 * D].astype(f32) * _dot(ob_ref[...], wb_ref[...])
         + g_ref[:, 2 * D:3 * D].astype(f32) * _dot(oc, wc_ref[...]))
    mix = _dot(y.astype(bf16), wo_ref[...])
    out = _layer_norm(alpha * x_ref[...] + mix, lg_ref[...], lb_ref[...])
    x32_ref[...] = out
    xb_ref[...] = out.astype(bf16)


def _merge_call(x32, oa, ob, dil_o, dil_l, hg, wa, wb, wc, wo, lg, lb, alpha, B, S, tm=256):
    T, D = x32.shape
    nb = S // tm
    row = lambda a: pl.BlockSpec((tm, a.shape[1]), lambda b, i: (b * nb + i, 0))
    cls = lambda a: pl.BlockSpec((None, a.shape[1], tm // a.shape[1], a.shape[3]), lambda b, i: (b, 0, i, 0))
    full = lambda a: pl.BlockSpec(a.shape, lambda b, i: (0, 0))
    consts = [wa, wb, wc, wo, lg, lb]
    ow, lw = dil_o[0].shape[3], dil_l[0].shape[3]
    return pl.pallas_call(
        functools.partial(_merge_kernel, alpha=alpha), grid=(B, nb),
        in_specs=[row(x32), row(oa), row(ob)] + [cls(a) for a in dil_o] + [cls(a) for a in dil_l] + [row(hg)]
        + [full(a) for a in consts],
        out_specs=[pl.BlockSpec((tm, D), lambda b, i: (b * nb + i, 0))] * 2,
        out_shape=[jax.ShapeDtypeStruct((T, D), f32), jax.ShapeDtypeStruct((T, D), bf16)],
        scratch_shapes=[pltpu.VMEM((ow // LANES, tm, LANES), f32), pltpu.VMEM((ow // LANES, tm, LANES), f32),
                        pltpu.VMEM((lw // LANES, tm, LANES), f32), pltpu.VMEM((lw // LANES, tm, LANES), f32)],
        compiler_params=_cparams(("parallel", "parallel")), name="merge")(
            x32, oa, ob, *dil_o, *dil_l, hg, *consts)


def _first_true(flags):
    out, seen = [], None
    for f in flags:
        out.append(f if seen is None else f & ~seen)
        seen = f if seen is None else seen | f
    return out


def _router_kernel(x_ref, rw_ref, rb_ref, gate_ref):
    logits = _dot_nt(rw_ref[...], x_ref[...])
    scores = jax.nn.sigmoid(logits)
    biased = scores + rb_ref[...]
    sc = [scores[e:e + 1, :] for e in range(N_EXPERTS)]
    bi = [biased[e:e + 1, :] for e in range(N_EXPERTS)]
    npg = EXPERTS_PER_GROUP
    gscore = []
    for g in range(N_EXPERT_GROUPS):
        v = bi[g * npg:(g + 1) * npg]
        best = None
        for a in range(npg):
            for b in range(a + 1, npg):
                best = v[a] + v[b] if best is None else jnp.maximum(best, v[a] + v[b])
        gscore.append(best)
    gbest = functools.reduce(jnp.maximum, gscore)
    gsel = _first_true([gs == gbest for gs in gscore])
    zero = jnp.zeros_like(gbest)
    bsel = [functools.reduce(lambda x, y: x + y, [jnp.where(gsel[g], bi[g * npg + j], zero)
                                                  for g in range(N_EXPERT_GROUPS)]) for j in range(npg)]
    ssel = [functools.reduce(lambda x, y: x + y, [jnp.where(gsel[g], sc[g * npg + j], zero)
                                                  for g in range(N_EXPERT_GROUPS)]) for j in range(npg)]
    t1 = functools.reduce(jnp.maximum, bsel)
    i1 = _first_true([b == t1 for b in bsel])
    rest = [jnp.where(i1[j], -jnp.inf, bsel[j]) for j in range(npg)]
    t2 = functools.reduce(jnp.maximum, rest)
    i2 = _first_true([rest[j] == t2 for j in range(npg)])
    w1 = functools.reduce(lambda x, y: x + y, [jnp.where(i1[j], ssel[j], zero) for j in range(npg)])
    w2 = functools.reduce(lambda x, y: x + y, [jnp.where(i2[j], ssel[j], zero) for j in range(npg)])
    den = w1 + w2
    local = [jnp.where(i1[j], w1 / den, zero) + jnp.where(i2[j], w2 / den, zero) for j in range(npg)]
    rows = [jnp.where(gsel[e // npg], local[e % npg], zero) for e in range(N_EXPERTS)]
    gate_ref[...] = jnp.concatenate(rows, axis=0)


def _router_call(xb, rw_t, rb, tm=1024):
    T, D = xb.shape
    return pl.pallas_call(
        _router_kernel, grid=(T // tm,),
        in_specs=[pl.BlockSpec((tm, D), lambda i: (i, 0)), pl.BlockSpec((N_EXPERTS, D), lambda i: (0, 0)),
                  pl.BlockSpec((N_EXPERTS, 1), lambda i: (0, 0))],
        out_specs=pl.BlockSpec((N_EXPERTS, tm), lambda i: (0, i)),
        out_shape=jax.ShapeDtypeStruct((N_EXPERTS, T), f32),
        compiler_params=_cparams(("parallel",)), name="router")(xb, rw_t, rb)


def _moe_kernel(x32_ref, xb_ref, gate_ref, wg_ref, wu_ref, wd_ref, wsg_ref, wsu_ref, wsd_ref, lg_ref, lb_ref,
                o32_ref, ob_ref, acc_sc, *, alpha, n_chunks, per_chunk):
    c = pl.program_id(1)
    x = xb_ref[...]

    @pl.when(c == 0)
    def _():
        hs = jax.nn.silu(_dot(x, wsg_ref[...])) * _dot(x, wsu_ref[...])
        acc_sc[...] = _dot(hs.astype(bf16), wsd_ref[...])

    gate = gate_ref[...]
    hid = []
    for e in range(per_chunk):
        sl = slice(e * EXPERT_HIDDEN, (e + 1) * EXPERT_HIDDEN)
        h = jax.nn.silu(_dot(x, wg_ref[:, sl])) * _dot(x, wu_ref[:, sl]) * gate[:, e:e + 1]
        hid.append(h.astype(bf16))
    acc_sc[...] += _dot(jnp.concatenate(hid, axis=1), wd_ref[...])

    @pl.when(c == n_chunks - 1)
    def _():
        out = _layer_norm(alpha * x32_ref[...] + acc_sc[...], lg_ref[...], lb_ref[...])
        o32_ref[...] = out
        ob_ref[...] = out.astype(bf16)


def _moe_call(x32, xb, gate_c, wg, wu, wd, wsg, wsu, wsd, lg, lb, alpha, tm=1024, per_chunk=4):
    T, D = x32.shape
    n_chunks = N_EXPERTS // per_chunk
    cw = per_chunk * EXPERT_HIDDEN
    row = pl.BlockSpec((tm, D), lambda i, c: (i, 0))
    full = lambda a: pl.BlockSpec(a.shape, lambda i, c: (0, 0))
    return pl.pallas_call(
        functools.partial(_moe_kernel, alpha=alpha, n_chunks=n_chunks, per_chunk=per_chunk),
        grid=(T // tm, n_chunks),
        in_specs=[row, row, pl.BlockSpec((None, tm, per_chunk), lambda i, c: (c, i, 0)),
                  pl.BlockSpec((D, cw), lambda i, c: (0, c)), pl.BlockSpec((D, cw), lambda i, c: (0, c)),
                  pl.BlockSpec((cw, D), lambda i, c: (c, 0)),
                  full(wsg), full(wsu), full(wsd), full(lg), full(lb)],
        out_specs=[row, row],
        out_shape=[jax.ShapeDtypeStruct((T, D), f32), jax.ShapeDtypeStruct((T, D), bf16)],
        scratch_shapes=[pltpu.VMEM((tm, D), f32)],
        compiler_params=_cparams(("parallel", "arbitrary")), name="moe")(
            x32, xb, gate_c, wg, wu, wd, wsg, wsu, wsd, lg, lb)


def _rope_tables(positions):
    pos = positions.astype(f32).reshape(-1, 1)
    T = pos.shape[0]
    ones, zeros = jnp.ones, jnp.zeros
    inv_p = ROPE_THETA ** (-jnp.arange(0, ROT_DIM, 2, dtype=f32) / ROT_DIM)
    ang = pos * inv_p
    c, s = jnp.cos(ang), jnp.sin(ang)
    pad = HEAD_DIM - ROT_DIM
    cos_p = jnp.tile(jnp.concatenate([c, c, ones((T, pad), f32)], axis=1), (1, 2))
    sin_p = jnp.tile(jnp.concatenate([-s, s, zeros((T, pad), f32)], axis=1), (1, 2))
    inv_m = ROPE_THETA ** (-jnp.arange(0, MLA_ROPE_DIM, 2, dtype=f32) / MLA_ROPE_DIM)
    ang = pos * inv_m
    c, s = jnp.cos(ang), jnp.sin(ang)
    tail = LANES - MLA_NOPE_DIM - MLA_ROPE_DIM
    cos_m = jnp.concatenate([ones((T, MLA_NOPE_DIM), f32), c, c, ones((T, tail), f32)], axis=1)
    sin_m = jnp.concatenate([zeros((T, MLA_NOPE_DIM), f32), -s, s, zeros((T, tail), f32)], axis=1)
    return cos_p, sin_p, cos_m, sin_m


def _pack_w_in(w):
    offs = [0]
    for sz in IN_SIZES:
        offs.append(offs[-1] + sz)
    a_q, a_k, a_v, b_cq, b_ckv, b_kr, c_q, c_k, c_v, g = [w[:, offs[i]:offs[i + 1]] for i in range(len(IN_SIZES))]
    D = w.shape[0]
    z = lambda n: jnp.zeros((D, n), w.dtype)
    kr_pad = jnp.concatenate([z(MLA_NOPE_DIM), b_kr, z(LANES - MLA_NOPE_DIM - MLA_ROPE_DIM)], axis=1)
    cols = [a_q, a_k, a_v, b_ckv, kr_pad, b_cq]
    for gi in range(DIL_GROUPS):
        cols += [c_q[:, 256 * gi:256 * (gi + 1)], c_k[:, 256 * gi:256 * (gi + 1)], c_v[:, 512 * gi:512 * (gi + 1)]]
    cols.append(g)
    return jnp.concatenate(cols, axis=1).astype(bf16)


def _pad_heads(w, n_heads, lo, hi):
    K = w.shape[0]
    per = w.shape[1] // n_heads
    wh = w.reshape(K, n_heads, per)[:, :, lo:hi]
    wh = jnp.pad(wh, ((0, 0), (0, 0), (0, LANES - (hi - lo))))
    return wh.reshape(K, n_heads * LANES).astype(bf16)


def kernel(x, positions, ln_in_g, ln_in_b, w_in, b_gate, lam_q1, lam_k1, lam_q2, lam_k2, diff_norm_g, mla_q_norm_g, mla_kv_norm_g, w_mla_qb, w_mla_kvb, w_branch_a, w_branch_b, w_branch_c, w_out, ln1_g, ln1_b, router_w, router_bias, w_exp_gate, w_exp_up, w_exp_down, w_sh_gate, w_sh_up, w_sh_down, ln2_g, ln2_b):
    B, S, D = x.shape
    T = B * S
    depth = w_in.shape[0]
    alpha = (2 * depth) ** 0.25
    cos_p, sin_p, cos_m, sin_m = _rope_tables(positions)
    x32, xb = _ln_call(x.reshape(T, D), ln_in_g.reshape(1, D), ln_in_b.reshape(1, D))
    rw_t = router_w.T.astype(bf16)
    rb = router_bias.reshape(N_EXPERTS, 1).astype(f32)
    vec = lambda v: v.reshape(1, -1).astype(f32)
    for l in range(depth):
        lam_init = 0.8 - 0.6 * math.exp(-0.3 * l)
        ha, hb, hc0, hc1, hc2, hg = _inproj_call(xb, _pack_w_in(w_in[l]), vec(b_gate[l]), cos_p, sin_p, B, S)
        lam_vec = jnp.pad(jnp.stack([lam_q1[l], lam_k1[l], lam_q2[l], lam_k2[l]]).astype(f32),
                          ((0, 4), (0, LANES - HEAD_DIM)))
        ha3 = ha.reshape(B, S, HA_W)
        oa = _flash_call(ha3, ha3, ha3, DIFF_HEADS, 0, DIFF_HEADS, 2 * DIFF_HEADS, n_maps=2, tq=FLASH_COLS // 2, tk=FLASH_TK,
                         name="diff_attn", lam_vec=lam_vec, norm_g=diff_norm_g[l].reshape(-1, 1).astype(f32),
                         lam_init=lam_init)
        per_q = MLA_NOPE_DIM + MLA_ROPE_DIM
        per_kv = MLA_NOPE_DIM + MLA_V_DIM
        qb, kb, vb = _mla_proj_calls(
            hb, vec(mla_q_norm_g[l]), vec(mla_kv_norm_g[l]), _pad_heads(w_mla_qb[l], MLA_HEADS, 0, per_q),
            _pad_heads(w_mla_kvb[l], MLA_HEADS, 0, MLA_NOPE_DIM), _pad_heads(w_mla_kvb[l], MLA_HEADS, MLA_NOPE_DIM, per_kv),
            cos_m, sin_m)
        ob = _flash_call(qb.reshape(B, S, -1), kb.reshape(B, S, -1), vb.reshape(B, S, -1), MLA_HEADS, 0, 0, 0,
                         n_maps=1, tq=FLASH_COLS, tk=FLASH_TK, name="mla_attn", dv=MLA_V_DIM)
        dil = [_dil_call(hc) for hc in (hc0, hc1, hc2)]
        wb_pad = jnp.pad(w_branch_b[l].reshape(MLA_HEADS, MLA_V_DIM, D),
                         ((0, 0), (0, LANES - MLA_V_DIM), (0, 0))).reshape(MLA_HEADS * LANES, D).astype(bf16)
        x32, xb = _merge_call(
            x32, oa.reshape(T, -1), ob.reshape(T, -1), [o for o, _ in dil], [lse for _, lse in dil],
            hg, w_branch_a[l].astype(bf16), wb_pad, w_branch_c[l].astype(bf16), w_out[l].astype(bf16),
            vec(ln1_g[l]), vec(ln1_b[l]), alpha, B, S)
        gate_t = _router_call(xb, rw_t, rb)
        per_chunk = 4
        gate_c = gate_t.reshape(N_EXPERTS // per_chunk, per_chunk, T).transpose(0, 2, 1)
        H = EXPERT_HIDDEN
        wg = w_exp_gate[l].transpose(1, 0, 2).reshape(D, N_EXPERTS * H).astype(bf16)
        wu = w_exp_up[l].transpose(1, 0, 2).reshape(D, N_EXPERTS * H).astype(bf16)
        wd = w_exp_down[l].reshape(N_EXPERTS * H, D).astype(bf16)
        x32, xb = _moe_call(x32, xb, gate_c, wg, wu, wd, w_sh_gate[l].astype(bf16), w_sh_up[l].astype(bf16),
                            w_sh_down[l].astype(bf16), vec(ln2_g[l]), vec(ln2_b[l]), alpha, per_chunk=per_chunk)
    return x32.reshape(B, S, D)
```

```python
import functools
import math

import jax
import jax.numpy as jnp
from jax import lax
from jax.experimental import pallas as pl
from jax.experimental.pallas import tpu as pltpu

f32 = jnp.float32
bf16 = jnp.bfloat16

D_MODEL = 1024
HEAD_DIM = 64
ROPE_THETA = 500000.0
ROT_DIM = HEAD_DIM // 4
DIFF_HEADS = 4
MLA_HEADS = 8
MLA_Q_RANK = 384
MLA_KV_RANK = 256
MLA_NOPE_DIM = 64
MLA_ROPE_DIM = 32
MLA_V_DIM = 64
DIL_PATTERNS = ((128, 1), (512, 4), (2048, 16))
DIL_GROUPS = 3
DIL_HEADS = 4
DIL_SIDE = 64
N_EXPERTS = 16
N_EXPERT_GROUPS = 4
EXPERTS_PER_GROUP = 4
EXPERT_HIDDEN = 256
IN_SIZES = (512, 512, 512, 384, 256, 32, 768, 768, 1536, 3072)
NEG_INF = -1e30
LOG2E = math.log2(math.e)

LANES = 128
VMEM_LIMIT = 52 * 1024 * 1024

HA_W = 1536
HA_OUT_W = 2048
HB_W = 768
HCG_W = 1024
HC_W = 3 * HCG_W
HG_W = 3072

FLASH_TK = 512
FLASH_COLS = 1024
FLASH_UNROLL = 8


def _cparams(sem):
    return pltpu.CompilerParams(dimension_semantics=sem, vmem_limit_bytes=VMEM_LIMIT)


def _layer_norm(z, g, b):
    mu = jnp.mean(z, axis=-1, keepdims=True)
    zc = z - mu
    var = jnp.mean(zc * zc, axis=-1, keepdims=True)
    return zc * lax.rsqrt(var + 1e-5) * g + b


def _rms_norm(z, g):
    return z * lax.rsqrt(jnp.mean(z * z, axis=-1, keepdims=True) + 1e-6) * g


def _dot(a, b):
    return jnp.dot(a, b, preferred_element_type=f32)


def _dot_nt(a, b):
    return lax.dot_general(a, b, (((1,), (1,)), ((), ())), preferred_element_type=f32)


def _rope_tile(xt, cos_t, sin_t, first_half, shift):
    xr = jnp.where(first_half, pltpu.roll(xt, LANES - shift, 1), pltpu.roll(xt, shift, 1))
    return xt * cos_t + xr * sin_t


def _ln_kernel(x_ref, g_ref, b_ref, o32_ref, ob_ref):
    y = _layer_norm(x_ref[...], g_ref[...], b_ref[...])
    o32_ref[...] = y
    ob_ref[...] = y.astype(bf16)


def _ln_call(x, g, b, tm=512):
    T, D = x.shape
    row = pl.BlockSpec((tm, D), lambda i: (i, 0))
    vec = pl.BlockSpec((1, D), lambda i: (0, 0))
    return pl.pallas_call(
        _ln_kernel, grid=(T // tm,), in_specs=[row, vec, vec], out_specs=[row, row],
        out_shape=[jax.ShapeDtypeStruct((T, D), f32), jax.ShapeDtypeStruct((T, D), bf16)],
        compiler_params=_cparams(("parallel",)), name="ln_in")(x, g, b)


def _inproj_plan():
    qscale = HEAD_DIM ** -0.5 * LOG2E
    plan = []
    for c in range(0, 512, 256):
        plan.append((c, 256, "rope", 0, c, qscale))
    for c in range(512, 1024, 256):
        plan.append((c, 256, "rope", 0, c, 1.0))
    plan.append((1024, 512, "value", 0, 1024, 1.0))
    plan.append((HA_W, HB_W, "plain", 1, 0, 1.0))
    base = HA_W + HB_W
    for g in range(DIL_GROUPS):
        o = HCG_W * g
        plan.append((base + o, 256, "rope", 2 + g, 0, qscale))
        plan.append((base + o + 256, 256, "rope", 2 + g, 256, 1.0))
        plan.append((base + o + 512, 512, "plain", 2 + g, 512, 1.0))
    base = HA_W + HB_W + HC_W
    for c in range(0, HG_W, 512):
        plan.append((base + c, 512, "gate", 2 + DIL_GROUPS, c, 1.0))
    return plan


def _inproj_kernel(x_ref, w_ref, bg_ref, cp_ref, sp_ref, ha_ref, hb_ref, hc0_ref, hc1_ref, hc2_ref, hg_ref, stage):
    outs = (ha_ref, hb_ref, hc0_ref, hc1_ref, hc2_ref, hg_ref)
    x = x_ref[...]
    tm = x.shape[0]
    cos_t = cp_ref[...]
    sin_t = sp_ref[...]
    lane = lax.broadcasted_iota(jnp.int32, cos_t.shape, 1)
    first_half = (lane % HEAD_DIM) < (ROT_DIM // 2)
    for (c0, w, kind, oi, o0, scale) in _inproj_plan():
        acc = _dot(x, w_ref[:, c0:c0 + w])
        if kind == "rope":
            tiles = []
            for t in range(w // LANES):
                y = _rope_tile(acc[:, t * LANES:(t + 1) * LANES], cos_t, sin_t, first_half, ROT_DIM // 2)
                tiles.append(y * scale if scale != 1.0 else y)
            acc = jnp.concatenate(tiles, axis=1)
        elif kind == "gate":
            acc = jax.nn.sigmoid(acc + bg_ref[:, o0:o0 + w])
        out = outs[oi]
        if kind == "value":
            ones_tile = (lane == 0).astype(out.dtype)
            for h in range(w // LANES):
                out[:, o0 + 2 * h * LANES:o0 + (2 * h + 1) * LANES] = acc[:, h * LANES:(h + 1) * LANES].astype(out.dtype)
                out[:, o0 + (2 * h + 1) * LANES:o0 + (2 * h + 2) * LANES] = ones_tile
        elif 2 <= oi < 2 + DIL_GROUPS:
            d = DIL_PATTERNS[oi - 2][1]
            if d == 1:
                out[0, :, o0:o0 + w] = acc.astype(out.dtype)
            else:
                nt = w // LANES
                for t in range(nt):
                    stage[t] = acc[:, t * LANES:(t + 1) * LANES]
                for r in range(d):
                    rows = [stage[t, pl.ds(r, tm // d, stride=d), :] for t in range(nt)]
                    out[r, :, o0:o0 + w] = jnp.concatenate(rows, axis=1).astype(out.dtype)
        else:
            out[:, o0:o0 + w] = acc.astype(out.dtype)


def _inproj_call(xb, w, bg, cos_t, sin_t, B, S, tm=256):
    T, D = xb.shape
    NW = w.shape[1]
    nb = S // tm
    row = lambda width: pl.BlockSpec((tm, width), lambda b, i: (b * nb + i, 0))
    const = lambda shape: pl.BlockSpec(shape, lambda b, i: (0, 0))
    hc_specs, hc_shapes = [], []
    for _, d in DIL_PATTERNS:
        hc_specs.append(pl.BlockSpec((None, d, tm // d, HCG_W), lambda b, i: (b, 0, i, 0)))
        hc_shapes.append(jax.ShapeDtypeStruct((B, d, S // d, HCG_W), bf16))
    return pl.pallas_call(
        _inproj_kernel, grid=(B, nb),
        in_specs=[row(D), const((D, NW)), const((1, HG_W)), row(LANES), row(LANES)],
        out_specs=[row(HA_OUT_W), row(HB_W)] + hc_specs + [row(HG_W)],
        out_shape=[jax.ShapeDtypeStruct((T, HA_OUT_W), bf16), jax.ShapeDtypeStruct((T, HB_W), f32)] + hc_shapes
        + [jax.ShapeDtypeStruct((T, HG_W), bf16)],
        scratch_shapes=[pltpu.VMEM((4, tm, LANES), f32)],
        compiler_params=_cparams(("parallel", "parallel")), name="inproj")(xb, w, bg, cos_t, sin_t)


def _flash_kernel(*refs, n_maps, tq, tk, n_kv, lam_init, dv):
    if n_maps == 2:
        q_ref, qn_ref, k_ref, v_ref, lam_ref, g_ref, o_ref, qt_sc, m_sc, l_sc, acc_sc, s_sc, mc_sc = refs
    else:
        q_ref, qn_ref, k_ref, v_ref, o_ref, qt_sc, m_sc, l_sc, acc_sc, s_sc, mc_sc = refs
    i = pl.program_id(2)
    cur = i % 2
    nxt = 1 - cur

    def load_queries(ref, slot):
        q = ref[...].astype(f32)
        if n_maps == 2:
            lane = lax.broadcasted_iota(jnp.int32, q.shape, 1)
            zero = jnp.zeros_like(q)
            q = jnp.concatenate([jnp.where(lane < HEAD_DIM, q, zero), jnp.where(lane >= HEAD_DIM, q, zero)], axis=0)
        qt_sc[slot] = q.T.astype(bf16)

    def scores(j, buf, slot):
        off = pl.multiple_of(j * tk, tk)
        s = _dot(k_ref[pl.ds(off, tk), :], qt_sc[slot])
        s_sc[buf] = s
        mc_sc[buf] = jnp.max(s, axis=0, keepdims=True)

    def accumulate(j, buf):
        off = pl.multiple_of(j * tk, tk)
        m_prev = m_sc[...]
        m_new = jnp.maximum(m_prev, mc_sc[buf])
        p = jnp.exp2(s_sc[buf] - m_new)
        a = jnp.exp2(m_prev - m_new)
        rows = acc_sc.shape[0]
        if rows == dv:
            l_sc[...] = a * l_sc[...] + jnp.sum(p, axis=0, keepdims=True)
        pv = lax.dot_general(v_ref[pl.ds(off, tk), 0:rows], p.astype(bf16), (((0,), (0,)), ((), ())),
                             preferred_element_type=f32)
        acc_sc[...] = a * acc_sc[...] + pv
        m_sc[...] = m_new

    @pl.when(i == 0)
    def _():
        load_queries(q_ref, cur)
        scores(0, 0, cur)

    load_queries(qn_ref, nxt)
    m_sc[...] = jnp.full(m_sc.shape, -jnp.inf, f32)
    l_sc[...] = jnp.zeros(l_sc.shape, f32)
    acc_sc[...] = jnp.zeros(acc_sc.shape, f32)

    unroll = min(FLASH_UNROLL, n_kv)
    assert unroll % 2 == 0 and n_kv % unroll == 0

    def body(jj, carry):
        for u in range(unroll):
            j = unroll * jj + u
            scores(j + 1, (u + 1) % 2, cur)
            accumulate(j, u % 2)
        return carry

    lax.fori_loop(0, n_kv // unroll - 1, body, 0)
    for j in range(n_kv - unroll, n_kv):
        if j + 1 < n_kv:
            scores(j + 1, (j + 1) % 2, cur)
        else:
            scores(0, 0, nxt)
        accumulate(j, j % 2)
    o = acc_sc[0:dv, :] / (l_sc[...] if acc_sc.shape[0] == dv else acc_sc[dv:dv + 1, :])
    if n_maps == 2:
        lv = lam_ref[...]
        lam = (jnp.exp(jnp.sum(lv[0:1] * lv[1:2], axis=-1, keepdims=True))
               - jnp.exp(jnp.sum(lv[2:3] * lv[3:4], axis=-1, keepdims=True)) + lam_init)
        o = o[:, :tq] - lam * o[:, tq:]
        ms = jnp.mean(o * o, axis=0, keepdims=True)
        o = o * lax.rsqrt(ms + 1e-6) * g_ref[...] * (1.0 - lam_init)
    if dv < LANES:
        o = jnp.concatenate([o, jnp.zeros((LANES - dv, o.shape[1]), f32)], axis=0)
    o_ref[...] = o.T.astype(o_ref.dtype)


def _flash_call(q_arr, k_arr, v_arr, n_heads, q_blk0, k_blk0, v_blk0, n_maps, tq, tk, name,
                lam_vec=None, norm_g=None, lam_init=0.0, dv=LANES, sum_rows=0, vw=LANES):
    B, S, _ = q_arr.shape
    cols = n_maps * tq
    last = S // tq - 1
    in_specs = [pl.BlockSpec((None, tq, LANES), lambda b, h, i: (b, i, q_blk0 + h)),
                pl.BlockSpec((None, tq, LANES), lambda b, h, i: (b, jnp.minimum(i + 1, last), q_blk0 + h)),
                pl.BlockSpec((None, S, LANES), lambda b, h, i: (b, 0, k_blk0 + h)),
                pl.BlockSpec((None, S, vw), lambda b, h, i: (b, 0, v_blk0 + h))]
    args = [q_arr, q_arr, k_arr, v_arr]
    if n_maps == 2:
        in_specs += [pl.BlockSpec((8, LANES), lambda b, h, i: (0, 0)),
                     pl.BlockSpec((LANES, 1), lambda b, h, i: (0, 0))]
        args += [lam_vec, norm_g]
    kern = functools.partial(_flash_kernel, n_maps=n_maps, tq=tq, tk=tk, n_kv=S // tk, lam_init=lam_init,
                             dv=dv)
    return pl.pallas_call(
        kern, grid=(B, n_heads, S // tq), in_specs=in_specs,
        out_specs=pl.BlockSpec((None, tq, LANES), lambda b, h, i: (b, i, h)),
        out_shape=jax.ShapeDtypeStruct((B, S, n_heads * LANES), bf16),
        scratch_shapes=[pltpu.VMEM((2, LANES, cols), bf16), pltpu.VMEM((1, cols), f32), pltpu.VMEM((1, cols), f32),
                        pltpu.VMEM((dv + sum_rows, cols), f32), pltpu.VMEM((2, tk, cols), f32),
                        pltpu.VMEM((2, 1, cols), f32)],
        compiler_params=_cparams(("parallel", "parallel", "arbitrary")), name=name)(*args)


def _mla_q_kernel(cq_ref, g_ref, w_ref, cm_ref, sm_ref, o_ref, *, scale):
    qn = _rms_norm(cq_ref[...], g_ref[...]).astype(bf16)
    cos_t = cm_ref[...]
    sin_t = sm_ref[...]
    lane = lax.broadcasted_iota(jnp.int32, cos_t.shape, 1)
    first_half = lane < MLA_NOPE_DIM + MLA_ROPE_DIM // 2
    for h in range(MLA_HEADS):
        acc = _dot(qn, w_ref[:, h * LANES:(h + 1) * LANES])
        y = _rope_tile(acc, cos_t, sin_t, first_half, MLA_ROPE_DIM // 2) * scale
        o_ref[:, h * LANES:(h + 1) * LANES] = y.astype(o_ref.dtype)


def _mla_kv_kernel(ckv_ref, kr_ref, g_ref, wk_ref, wv_ref, cm_ref, sm_ref, k_ref, v_ref):
    cn = _rms_norm(ckv_ref[...], g_ref[...]).astype(bf16)
    cos_t = cm_ref[...]
    lane = lax.broadcasted_iota(jnp.int32, cos_t.shape, 1)
    first_half = lane < MLA_NOPE_DIM + MLA_ROPE_DIM // 2
    kr = _rope_tile(kr_ref[...], cos_t, sm_ref[...], first_half, MLA_ROPE_DIM // 2)
    for h in range(MLA_HEADS):
        sl = slice(h * LANES, (h + 1) * LANES)
        k_ref[:, sl] = (_dot(cn, wk_ref[:, sl]) + kr).astype(k_ref.dtype)
        v = _dot(cn, wv_ref[:, sl])
        v_ref[:, sl] = jnp.where(lane == MLA_V_DIM, 1.0, v).astype(v_ref.dtype)


def _mla_proj_calls(hb, qg, kvg, wq, wk, wv, cos_m, sin_m, tm=512):
    T = hb.shape[0]
    W = MLA_HEADS * LANES
    row = lambda width, blk=0: pl.BlockSpec((tm, width), lambda i: (i, blk))
    full = lambda a: pl.BlockSpec(a.shape, lambda i: (0, 0))
    scale = (MLA_NOPE_DIM + MLA_ROPE_DIM) ** -0.5 * LOG2E
    qb = pl.pallas_call(
        functools.partial(_mla_q_kernel, scale=scale), grid=(T // tm,),
        in_specs=[row(MLA_Q_RANK, 1), full(qg), full(wq), row(LANES), row(LANES)],
        out_specs=row(W), out_shape=jax.ShapeDtypeStruct((T, W), bf16),
        compiler_params=_cparams(("parallel",)), name="mla_q")(hb, qg, wq, cos_m, sin_m)
    kb, vb = pl.pallas_call(
        _mla_kv_kernel, grid=(T // tm,),
        in_specs=[row(MLA_KV_RANK, 0), row(LANES, 2), full(kvg), full(wk), full(wv), row(LANES), row(LANES)],
        out_specs=[row(W), row(W)],
        out_shape=[jax.ShapeDtypeStruct((T, W), bf16), jax.ShapeDtypeStruct((T, W), bf16)],
        compiler_params=_cparams(("parallel",)), name="mla_kv")(hb, hb, kvg, wk, wv, cos_m, sin_m)
    return qb, kb, vb


def _dil_kernel(q_ref, kp_ref, kc_ref, kn_ref, vp_ref, vc_ref, vn_ref, o_ref, lse_ref, kcat, vcat, *, tq, sub_len):
    i = pl.program_id(2)
    side = DIL_SIDE
    kcat[0:side, :] = kp_ref[...]
    kcat[side:side + tq, :] = kc_ref[...]
    kcat[side + tq:, :] = kn_ref[...]
    vcat[0:side, :] = vp_ref[...]
    vcat[side:side + tq, :] = vc_ref[...]
    vcat[side + tq:, :] = vn_ref[...]
    qs, ks = 2 * side, 4 * side
    lane = lax.broadcasted_iota(jnp.int32, (qs, LANES), 1)
    ii = lax.broadcasted_iota(jnp.int32, (qs, ks), 0)
    jj = lax.broadcasted_iota(jnp.int32, (qs, ks), 1)
    band = (jj - ii >= 0) & (jj - ii <= 2 * side)
    for sb in range(tq // qs):
        q0 = sb * qs
        pos = jj + (i * tq + q0 - side)
        mask = band & (pos >= 0) & (pos < sub_len)
        lse_tile = jnp.zeros((qs, LANES), f32)
        for h in range(DIL_HEADS):
            t = h // 2
            qt = q_ref[q0:q0 + qs, t * LANES:(t + 1) * LANES]
            keep = (lane < HEAD_DIM) if h % 2 == 0 else (lane >= HEAD_DIM)
            qm = jnp.where(keep, qt, jnp.zeros_like(qt))
            s = _dot_nt(qm, kcat[q0:q0 + ks, t * LANES:(t + 1) * LANES])
            s = jnp.where(mask, s, NEG_INF)
            m = jnp.max(s, axis=-1, keepdims=True)
            p = jnp.exp2(s - m)
            l = jnp.sum(p, axis=-1, keepdims=True)
            o = _dot(p.astype(bf16), vcat[q0:q0 + ks, h * LANES:(h + 1) * LANES]) / l
            o_ref[q0:q0 + qs, h * LANES:(h + 1) * LANES] = o.astype(o_ref.dtype)
            lse_tile = jnp.where(lane == h, m + jnp.log2(l), lse_tile)
        lse_ref[q0:q0 + qs, :] = lse_tile


def _dil_call(hc):
    B, d, L, _ = hc.shape
    tq = min(512, L)
    side = DIL_SIDE
    nblk = tq // side
    last = L // side - 1
    qw, vw = 2 * LANES, 4 * LANES
    prev = lambda i: jnp.maximum(i * nblk - 1, 0)
    nxt = lambda i: jnp.minimum((i + 1) * nblk, last)
    in_specs = [
        pl.BlockSpec((None, None, tq, qw), lambda b, r, i: (b, r, i, 0)),
        pl.BlockSpec((None, None, side, qw), lambda b, r, i: (b, r, prev(i), 1)),
        pl.BlockSpec((None, None, tq, qw), lambda b, r, i: (b, r, i, 1)),
        pl.BlockSpec((None, None, side, qw), lambda b, r, i: (b, r, nxt(i), 1)),
        pl.BlockSpec((None, None, side, vw), lambda b, r, i: (b, r, prev(i), 1)),
        pl.BlockSpec((None, None, tq, vw), lambda b, r, i: (b, r, i, 1)),
        pl.BlockSpec((None, None, side, vw), lambda b, r, i: (b, r, nxt(i), 1)),
    ]
    return pl.pallas_call(
        functools.partial(_dil_kernel, tq=tq, sub_len=L), grid=(B, d, L // tq), in_specs=in_specs,
        out_specs=[pl.BlockSpec((None, None, tq, vw), lambda b, r, i: (b, r, i, 0)),
                   pl.BlockSpec((None, None, tq, LANES), lambda b, r, i: (b, r, i, 0))],
        out_shape=[jax.ShapeDtypeStruct((B, d, L, vw), bf16), jax.ShapeDtypeStruct((B, d, L, LANES), f32)],
        scratch_shapes=[pltpu.VMEM((tq + 2 * side, qw), bf16), pltpu.VMEM((tq + 2 * side, vw), bf16)],
        compiler_params=_cparams(("parallel", "parallel", "arbitrary")),
        name=f"dilated_d{d}")(hc, hc, hc, hc, hc, hc, hc)


def _token_order(ref, stage):
    d, n, w = ref.shape
    if d == 1:
        return ref[0].astype(f32)
    nt = w // LANES
    for r in range(d):
        v = ref[r].astype(f32)
        for t in range(nt):
            stage[t, pl.ds(r, n, stride=d), :] = v[:, t * LANES:(t + 1) * LANES]
    return jnp.concatenate([stage[t] for t in range(nt)], axis=1)


def _merge_kernel(x_ref, oa_ref, ob_ref, o1_ref, o2_ref, o3_ref, l1_ref, l2_ref, l3_ref, g_ref,
                  wa_ref, wb_ref, wc_ref, wo_ref, lg_ref, lb_ref, x32_ref, xb_ref,
                  so2, so3, sl2, sl3, *, alpha):
    l1, l2, l3 = _token_order(l1_ref, None), _token_order(l2_ref, sl2), _token_order(l3_ref, sl3)
    o1, o2, o3 = _token_order(o1_ref, None), _token_order(o2_ref, so2), _token_order(o3_ref, so3)
    mx = jnp.maximum(jnp.maximum(l1, l2), l3)
    e1, e2, e3 = jnp.exp2(l1 - mx), jnp.exp2(l2 - mx), jnp.exp2(l3 - mx)
    den = e1 + e2 + e3
    w1, w2, w3 = e1 / den, e2 / den, e3 / den
    tiles = []
    for h in range(DIL_HEADS):
        sl = slice(h * LANES, (h + 1) * LANES)
        oc = w1[:, h:h + 1] * o1[:, sl] + w2[:, h:h + 1] * o2[:, sl] + w3[:, h:h + 1] * o3[:, sl]
        tiles.append(oc.astype(bf16))
    oc = jnp.concatenate(tiles, axis=1)
    D = D_MODEL
    y = (g_ref[:, 0:D].astype(f32) * _dot(oa_ref[...], wa_ref[...])
         + g_ref[:, D:2 * D].astype(f32) * _dot(ob_ref[...], wb_ref[...])
         + g_ref[:, 2 * D:3 * D].astype(f32) * _dot(oc, wc_ref[...]))
    mix = _dot(y.astype(bf16), wo_ref[...])
    out = _layer_norm(alpha * x_ref[...] + mix, lg_ref[...], lb_ref[...])
    x32_ref[...] = out
    xb_ref[...] = out.astype(bf16)


def _merge_call(x32, oa, ob, dil_o, dil_l, hg, wa, wb, wc, wo, lg, lb, alpha, B, S, tm=256):
    T, D = x32.shape
    nb = S // tm
    row = lambda a: pl.BlockSpec((tm, a.shape[1]), lambda b, i: (b * nb + i, 0))
    cls = lambda a: pl.BlockSpec((None, a.shape[1], tm // a.shape[1], a.shape[3]), lambda b, i: (b, 0, i, 0))
    full = lambda a: pl.BlockSpec(a.shape, lambda b, i: (0, 0))
    consts = [wa, wb, wc, wo, lg, lb]
    ow, lw = dil_o[0].shape[3], dil_l[0].shape[3]
    return pl.pallas_call(
        functools.partial(_merge_kernel, alpha=alpha), grid=(B, nb),
        in_specs=[row(x32), row(oa), row(ob)] + [cls(a) for a in dil_o] + [cls(a) for a in dil_l] + [row(hg)]
        + [full(a) for a in consts],
        out_specs=[pl.BlockSpec((tm, D), lambda b, i: (b * nb + i, 0))] * 2,
        out_shape=[jax.ShapeDtypeStruct((T, D), f32), jax.ShapeDtypeStruct((T, D), bf16)],
        scratch_shapes=[pltpu.VMEM((ow // LANES, tm, LANES), f32), pltpu.VMEM((ow // LANES, tm, LANES), f32),
                        pltpu.VMEM((lw // LANES, tm, LANES), f32), pltpu.VMEM((lw // LANES, tm, LANES), f32)],
        compiler_params=_cparams(("parallel", "parallel")), name="merge")(
            x32, oa, ob, *dil_o, *dil_l, hg, *consts)


def _first_true(flags):
    out, seen = [], None
    for f in flags:
        out.append(f if seen is None else f & ~seen)
        seen = f if seen is None else seen | f
    return out


def _router_kernel(x_ref, rw_ref, rb_ref, gate_ref):
    logits = _dot_nt(rw_ref[...], x_ref[...])
    scores = jax.nn.sigmoid(logits)
    biased = scores + rb_ref[...]
    sc = [scores[e:e + 1, :] for e in range(N_EXPERTS)]
    bi = [biased[e:e + 1, :] for e in range(N_EXPERTS)]
    npg = EXPERTS_PER_GROUP
    gscore = []
    for g in range(N_EXPERT_GROUPS):
        v = bi[g * npg:(g + 1) * npg]
        best = None
        for a in range(npg):
            for b in range(a + 1, npg):
                best = v[a] + v[b] if best is None else jnp.maximum(best, v[a] + v[b])
        gscore.append(best)
    gbest = functools.reduce(jnp.maximum, gscore)
    gsel = _first_true([gs == gbest for gs in gscore])
    zero = jnp.zeros_like(gbest)
    bsel = [functools.reduce(lambda x, y: x + y, [jnp.where(gsel[g], bi[g * npg + j], zero)
                                                  for g in range(N_EXPERT_GROUPS)]) for j in range(npg)]
    ssel = [functools.reduce(lambda x, y: x + y, [jnp.where(gsel[g], sc[g * npg + j], zero)
                                                  for g in range(N_EXPERT_GROUPS)]) for j in range(npg)]
    t1 = functools.reduce(jnp.maximum, bsel)
    i1 = _first_true([b == t1 for b in bsel])
    rest = [jnp.where(i1[j], -jnp.inf, bsel[j]) for j in range(npg)]
    t2 = functools.reduce(jnp.maximum, rest)
    i2 = _first_true([rest[j] == t2 for j in range(npg)])
    w1 = functools.reduce(lambda x, y: x + y, [jnp.where(i1[j], ssel[j], zero) for j in range(npg)])
    w2 = functools.reduce(lambda x, y: x + y, [jnp.where(i2[j], ssel[j], zero) for j in range(npg)])
    den = w1 + w2
    local = [jnp.where(i1[j], w1 / den, zero) + jnp.where(i2[j], w2 / den, zero) for j in range(npg)]
    rows = [jnp.where(gsel[e // npg], local[e % npg], zero) for e in range(N_EXPERTS)]
    gate_ref[...] = jnp.concatenate(rows, axis=0)


def _router_call(xb, rw_t, rb, tm=1024):
    T, D = xb.shape
    return pl.pallas_call(
        _router_kernel, grid=(T // tm,),
        in_specs=[pl.BlockSpec((tm, D), lambda i: (i, 0)), pl.BlockSpec((N_EXPERTS, D), lambda i: (0, 0)),
                  pl.BlockSpec((N_EXPERTS, 1), lambda i: (0, 0))],
        out_specs=pl.BlockSpec((N_EXPERTS, tm), lambda i: (0, i)),
        out_shape=jax.ShapeDtypeStruct((N_EXPERTS, T), f32),
        compiler_params=_cparams(("parallel",)), name="router")(xb, rw_t, rb)


def _moe_kernel(x32_ref, xb_ref, gate_ref, wg_ref, wu_ref, wd_ref, wsg_ref, wsu_ref, wsd_ref, lg_ref, lb_ref,
                o32_ref, ob_ref, acc_sc, *, alpha, n_chunks, per_chunk):
    c = pl.program_id(1)
    x = xb_ref[...]

    @pl.when(c == 0)
    def _():
        hs = jax.nn.silu(_dot(x, wsg_ref[...])) * _dot(x, wsu_ref[...])
        acc_sc[...] = _dot(hs.astype(bf16), wsd_ref[...])

    gate = gate_ref[...]
    hid = []
    for e in range(per_chunk):
        sl = slice(e * EXPERT_HIDDEN, (e + 1) * EXPERT_HIDDEN)
        h = jax.nn.silu(_dot(x, wg_ref[:, sl])) * _dot(x, wu_ref[:, sl]) * gate[:, e:e + 1]
        hid.append(h.astype(bf16))
    acc_sc[...] += _dot(jnp.concatenate(hid, axis=1), wd_ref[...])

    @pl.when(c == n_chunks - 1)
    def _():
        out = _layer_norm(alpha * x32_ref[...] + acc_sc[...], lg_ref[...], lb_ref[...])
        o32_ref[...] = out
        ob_ref[...] = out.astype(bf16)


def _moe_call(x32, xb, gate_c, wg, wu, wd, wsg, wsu, wsd, lg, lb, alpha, tm=1024, per_chunk=4):
    T, D = x32.shape
    n_chunks = N_EXPERTS // per_chunk
    cw = per_chunk * EXPERT_HIDDEN
    row = pl.BlockSpec((tm, D), lambda i, c: (i, 0))
    full = lambda a: pl.BlockSpec(a.shape, lambda i, c: (0, 0))
    return pl.pallas_call(
        functools.partial(_moe_kernel, alpha=alpha, n_chunks=n_chunks, per_chunk=per_chunk),
        grid=(T // tm, n_chunks),
        in_specs=[row, row, pl.BlockSpec((None, tm, per_chunk), lambda i, c: (c, i, 0)),
                  pl.BlockSpec((D, cw), lambda i, c: (0, c)), pl.BlockSpec((D, cw), lambda i, c: (0, c)),
                  pl.BlockSpec((cw, D), lambda i, c: (c, 0)),
                  full(wsg), full(wsu), full(wsd), full(lg), full(lb)],
        out_specs=[row, row],
        out_shape=[jax.ShapeDtypeStruct((T, D), f32), jax.ShapeDtypeStruct((T, D), bf16)],
        scratch_shapes=[pltpu.VMEM((tm, D), f32)],
        compiler_params=_cparams(("parallel", "arbitrary")), name="moe")(
            x32, xb, gate_c, wg, wu, wd, wsg, wsu, wsd, lg, lb)


def _rope_tables(positions):
    pos = positions.astype(f32).reshape(-1, 1)
    T = pos.shape[0]
    ones, zeros = jnp.ones, jnp.zeros
    inv_p = ROPE_THETA ** (-jnp.arange(0, ROT_DIM, 2, dtype=f32) / ROT_DIM)
    ang = pos * inv_p
    c, s = jnp.cos(ang), jnp.sin(ang)
    pad = HEAD_DIM - ROT_DIM
    cos_p = jnp.tile(jnp.concatenate([c, c, ones((T, pad), f32)], axis=1), (1, 2))
    sin_p = jnp.tile(jnp.concatenate([-s, s, zeros((T, pad), f32)], axis=1), (1, 2))
    inv_m = ROPE_THETA ** (-jnp.arange(0, MLA_ROPE_DIM, 2, dtype=f32) / MLA_ROPE_DIM)
    ang = pos * inv_m
    c, s = jnp.cos(ang), jnp.sin(ang)
    tail = LANES - MLA_NOPE_DIM - MLA_ROPE_DIM
    cos_m = jnp.concatenate([ones((T, MLA_NOPE_DIM), f32), c, c, ones((T, tail), f32)], axis=1)
    sin_m = jnp.concatenate([zeros((T, MLA_NOPE_DIM), f32), -s, s, zeros((T, tail), f32)], axis=1)
    return cos_p, sin_p, cos_m, sin_m


def _pack_w_in(w):
    offs = [0]
    for sz in IN_SIZES:
        offs.append(offs[-1] + sz)
    a_q, a_k, a_v, b_cq, b_ckv, b_kr, c_q, c_k, c_v, g = [w[:, offs[i]:offs[i + 1]] for i in range(len(IN_SIZES))]
    D = w.shape[0]
    z = lambda n: jnp.zeros((D, n), w.dtype)
    kr_pad = jnp.concatenate([z(MLA_NOPE_DIM), b_kr, z(LANES - MLA_NOPE_DIM - MLA_ROPE_DIM)], axis=1)
    cols = [a_q, a_k, a_v, b_ckv, kr_pad, b_cq]
    for gi in range(DIL_GROUPS):
        cols += [c_q[:, 256 * gi:256 * (gi + 1)], c_k[:, 256 * gi:256 * (gi + 1)], c_v[:, 512 * gi:512 * (gi + 1)]]
    cols.append(g)
    return jnp.concatenate(cols, axis=1).astype(bf16)


def _pad_heads(w, n_heads, lo, hi):
    K = w.shape[0]
    per = w.shape[1] // n_heads
    wh = w.reshape(K, n_heads, per)[:, :, lo:hi]
    wh = jnp.pad(wh, ((0, 0), (0, 0), (0, LANES - (hi - lo))))
    return wh.reshape(K, n_heads * LANES).astype(bf16)


def kernel(x, positions, ln_in_g, ln_in_b, w_in, b_gate, lam_q1, lam_k1, lam_q2, lam_k2, diff_norm_g, mla_q_norm_g, mla_kv_norm_g, w_mla_qb, w_mla_kvb, w_branch_a, w_branch_b, w_branch_c, w_out, ln1_g, ln1_b, router_w, router_bias, w_exp_gate, w_exp_up, w_exp_down, w_sh_gate, w_sh_up, w_sh_down, ln2_g, ln2_b):
    B, S, D = x.shape
    T = B * S
    depth = w_in.shape[0]
    alpha = (2 * depth) ** 0.25
    cos_p, sin_p, cos_m, sin_m = _rope_tables(positions)
    x32, xb = _ln_call(x.reshape(T, D), ln_in_g.reshape(1, D), ln_in_b.reshape(1, D))
    rw_t = router_w.T.astype(bf16)
    rb = router_bias.reshape(N_EXPERTS, 1).astype(f32)
    vec = lambda v: v.reshape(1, -1).astype(f32)
    for l in range(depth):
        lam_init = 0.8 - 0.6 * math.exp(-0.3 * l)
        ha, hb, hc0, hc1, hc2, hg = _inproj_call(xb, _pack_w_in(w_in[l]), vec(b_gate[l]), cos_p, sin_p, B, S)
        lam_vec = jnp.pad(jnp.stack([lam_q1[l], lam_k1[l], lam_q2[l], lam_k2[l]]).astype(f32),
                          ((0, 4), (0, LANES - HEAD_DIM)))
        ha3 = ha.reshape(B, S, HA_OUT_W)
        oa = _flash_call(ha3, ha3, ha3, DIFF_HEADS, 0, DIFF_HEADS, DIFF_HEADS, n_maps=2, tq=FLASH_COLS // 2, tk=FLASH_TK,
                         name="diff_attn", lam_vec=lam_vec, norm_g=diff_norm_g[l].reshape(-1, 1).astype(f32),
                         lam_init=lam_init, sum_rows=16, vw=2 * LANES)
        per_q = MLA_NOPE_DIM + MLA_ROPE_DIM
        per_kv = MLA_NOPE_DIM + MLA_V_DIM
        qb, kb, vb = _mla_proj_calls(
            hb, vec(mla_q_norm_g[l]), vec(mla_kv_norm_g[l]), _pad_heads(w_mla_qb[l], MLA_HEADS, 0, per_q),
            _pad_heads(w_mla_kvb[l], MLA_HEADS, 0, MLA_NOPE_DIM), _pad_heads(w_mla_kvb[l], MLA_HEADS, MLA_NOPE_DIM, per_kv),
            cos_m, sin_m)
        ob = _flash_call(qb.reshape(B, S, -1), kb.reshape(B, S, -1), vb.reshape(B, S, -1), MLA_HEADS, 0, 0, 0,
                         n_maps=1, tq=FLASH_COLS, tk=FLASH_TK, name="mla_attn", dv=MLA_V_DIM, sum_rows=16)
        dil = [_dil_call(hc) for hc in (hc0, hc1, hc2)]
        wb_pad = jnp.pad(w_branch_b[l].reshape(MLA_HEADS, MLA_V_DIM, D),
                         ((0, 0), (0, LANES - MLA_V_DIM), (0, 0))).reshape(MLA_HEADS * LANES, D).astype(bf16)
        x32, xb = _merge_call(
            x32, oa.reshape(T, -1), ob.reshape(T, -1), [o for o, _ in dil], [lse for _, lse in dil],
            hg, w_branch_a[l].astype(bf16), wb_pad, w_branch_c[l].astype(bf16), w_out[l].astype(bf16),
            vec(ln1_g[l]), vec(ln1_b[l]), alpha, B, S)
        gate_t = _router_call(xb, rw_t, rb)
        per_chunk = 4
        gate_c = gate_t.reshape(N_EXPERTS // per_chunk, per_chunk, T).transpose(0, 2, 1)
        H = EXPERT_HIDDEN
        wg = w_exp_gate[l].transpose(1, 0, 2).reshape(D, N_EXPERTS * H).astype(bf16)
        wu = w_exp_up[l].transpose(1, 0, 2).reshape(D, N_EXPERTS * H).astype(bf16)
        wd = w_exp_down[l].reshape(N_EXPERTS * H, D).astype(bf16)
        x32, xb = _moe_call(x32, xb, gate_c, wg, wu, wd, w_sh_gate[l].astype(bf16), w_sh_up[l].astype(bf16),
                            w_sh_down[l].astype(bf16), vec(ln2_g[l]), vec(ln2_b[l]), alpha, per_chunk=per_chunk)
    return x32.reshape(B, S, D)
```

```python
import functools
import math

import jax
import jax.numpy as jnp
from jax import lax
from jax.experimental import pallas as pl
from jax.experimental.pallas import tpu as pltpu

f32 = jnp.float32
bf16 = jnp.bfloat16

D_MODEL = 1024
HEAD_DIM = 64
ROPE_THETA = 500000.0
ROT_DIM = HEAD_DIM // 4
DIFF_HEADS = 4
MLA_HEADS = 8
MLA_Q_RANK = 384
MLA_KV_RANK = 256
MLA_NOPE_DIM = 64
MLA_ROPE_DIM = 32
MLA_V_DIM = 64
DIL_PATTERNS = ((128, 1), (512, 4), (2048, 16))
DIL_GROUPS = 3
DIL_HEADS = 4
DIL_SIDE = 64
N_EXPERTS = 16
N_EXPERT_GROUPS = 4
EXPERTS_PER_GROUP = 4
EXPERT_HIDDEN = 256
IN_SIZES = (512, 512, 512, 384, 256, 32, 768, 768, 1536, 3072)
NEG_INF = -1e30
LOG2E = math.log2(math.e)

LANES = 128
VMEM_LIMIT = 52 * 1024 * 1024

HA_W = 1536
HA_OUT_W = 2048
HB_W = 768
HCG_W = 1024
HC_W = 3 * HCG_W
HG_W = 3072

FLASH_TK = 512
FLASH_COLS = 1024
FLASH_UNROLL = 8
SUM_ROWS = 16


def _cparams(sem):
    return pltpu.CompilerParams(dimension_semantics=sem, vmem_limit_bytes=VMEM_LIMIT)


def _layer_norm(z, g, b):
    mu = jnp.mean(z, axis=-1, keepdims=True)
    zc = z - mu
    var = jnp.mean(zc * zc, axis=-1, keepdims=True)
    return zc * lax.rsqrt(var + 1e-5) * g + b


def _rms_norm(z, g):
    return z * lax.rsqrt(jnp.mean(z * z, axis=-1, keepdims=True) + 1e-6) * g


def _dot(a, b):
    return jnp.dot(a, b, preferred_element_type=f32)


def _dot_nt(a, b):
    return lax.dot_general(a, b, (((1,), (1,)), ((), ())), preferred_element_type=f32)


def _rope_tile(xt, cos_t, sin_t, first_half, shift):
    xr = jnp.where(first_half, pltpu.roll(xt, LANES - shift, 1), pltpu.roll(xt, shift, 1))
    return xt * cos_t + xr * sin_t


def _ln_kernel(x_ref, g_ref, b_ref, o32_ref, ob_ref):
    y = _layer_norm(x_ref[...], g_ref[...], b_ref[...])
    o32_ref[...] = y
    ob_ref[...] = y.astype(bf16)


def _ln_call(x, g, b, tm=512):
    T, D = x.shape
    row = pl.BlockSpec((tm, D), lambda i: (i, 0))
    vec = pl.BlockSpec((1, D), lambda i: (0, 0))
    return pl.pallas_call(
        _ln_kernel, grid=(T // tm,), in_specs=[row, vec, vec], out_specs=[row, row],
        out_shape=[jax.ShapeDtypeStruct((T, D), f32), jax.ShapeDtypeStruct((T, D), bf16)],
        compiler_params=_cparams(("parallel",)), name="ln_in")(x, g, b)


def _inproj_plan():
    qscale = HEAD_DIM ** -0.5 * LOG2E
    plan = []
    for c in range(0, 512, 256):
        plan.append((c, 256, "rope", 0, c, qscale))
    for c in range(512, 1024, 256):
        plan.append((c, 256, "rope", 0, c, 1.0))
    plan.append((1024, 512, "value", 0, 1024, 1.0))
    plan.append((HA_W, HB_W, "mla", 1, 0, 1.0))
    base = HA_W + HB_W
    for g in range(DIL_GROUPS):
        o = HCG_W * g
        plan.append((base + o, 256, "rope", 2 + g, 0, qscale))
        plan.append((base + o + 256, 256, "rope", 2 + g, 256, 1.0))
        plan.append((base + o + 512, 512, "plain", 2 + g, 512, 1.0))
    base = HA_W + HB_W + HC_W
    for c in range(0, HG_W, 512):
        plan.append((base + c, 512, "gate", 2 + DIL_GROUPS, c, 1.0))
    return plan


def _mla_up_projections(acc, lane, cm_ref, sm_ref, qg_ref, kvg_ref, wq_ref, wk_ref, wv_ref, qb_ref, kb_ref, vb_ref):
    cos_t, sin_t = cm_ref[...], sm_ref[...]
    first_half = lane < MLA_NOPE_DIM + MLA_ROPE_DIM // 2
    shift = MLA_ROPE_DIM // 2
    scale = (MLA_NOPE_DIM + MLA_ROPE_DIM) ** -0.5 * LOG2E
    cn = _rms_norm(acc[:, 0:MLA_KV_RANK], kvg_ref[...]).astype(bf16)
    kr = _rope_tile(acc[:, MLA_KV_RANK:MLA_KV_RANK + LANES], cos_t, sin_t, first_half, shift)
    qn = _rms_norm(acc[:, MLA_KV_RANK + LANES:], qg_ref[...]).astype(bf16)
    q_all = _dot(qn, wq_ref[...])
    k_all = _dot(cn, wk_ref[...])
    v_all = _dot(cn, wv_ref[...])
    for h in range(MLA_HEADS):
        sl = slice(h * LANES, (h + 1) * LANES)
        q = _rope_tile(q_all[:, sl], cos_t, sin_t, first_half, shift) * scale
        qb_ref[:, sl] = q.astype(qb_ref.dtype)
        kb_ref[:, sl] = (k_all[:, sl] + kr).astype(kb_ref.dtype)
        vb_ref[:, sl] = jnp.where(lane == MLA_V_DIM, 1.0, v_all[:, sl]).astype(vb_ref.dtype)


def _inproj_kernel(x_ref, w_ref, bg_ref, cp_ref, sp_ref, cm_ref, sm_ref, qg_ref, kvg_ref, wq_ref, wk_ref, wv_ref,
                   ha_ref, qb_ref, kb_ref, vb_ref, hc0_ref, hc1_ref, hc2_ref, hg_ref, stage):
    outs = (ha_ref, None, hc0_ref, hc1_ref, hc2_ref, hg_ref)
    x = x_ref[...]
    tm = x.shape[0]
    cos_t = cp_ref[...]
    sin_t = sp_ref[...]
    lane = lax.broadcasted_iota(jnp.int32, cos_t.shape, 1)
    first_half = (lane % HEAD_DIM) < (ROT_DIM // 2)
    for (c0, w, kind, oi, o0, scale) in _inproj_plan():
        acc = _dot(x, w_ref[:, c0:c0 + w])
        if kind == "rope":
            tiles = []
            for t in range(w // LANES):
                y = _rope_tile(acc[:, t * LANES:(t + 1) * LANES], cos_t, sin_t, first_half, ROT_DIM // 2)
                tiles.append(y * scale if scale != 1.0 else y)
            acc = jnp.concatenate(tiles, axis=1)
        elif kind == "gate":
            acc = jax.nn.sigmoid(acc + bg_ref[:, o0:o0 + w])
        if kind == "mla":
            _mla_up_projections(acc, lane, cm_ref, sm_ref, qg_ref, kvg_ref, wq_ref, wk_ref, wv_ref,
                                qb_ref, kb_ref, vb_ref)
            continue
        out = outs[oi]
        if kind == "value":
            ones_tile = (lane == 0).astype(out.dtype)
            for h in range(w // LANES):
                out[:, o0 + 2 * h * LANES:o0 + (2 * h + 1) * LANES] = acc[:, h * LANES:(h + 1) * LANES].astype(out.dtype)
                out[:, o0 + (2 * h + 1) * LANES:o0 + (2 * h + 2) * LANES] = ones_tile
        elif 2 <= oi < 2 + DIL_GROUPS:
            d = DIL_PATTERNS[oi - 2][1]
            if d == 1:
                out[0, :, o0:o0 + w] = acc.astype(out.dtype)
            else:
                nt = w // LANES
                for t in range(nt):
                    stage[t] = acc[:, t * LANES:(t + 1) * LANES]
                for r in range(d):
                    rows = [stage[t, pl.ds(r, tm // d, stride=d), :] for t in range(nt)]
                    out[r, :, o0:o0 + w] = jnp.concatenate(rows, axis=1).astype(out.dtype)
        else:
            out[:, o0:o0 + w] = acc.astype(out.dtype)


def _inproj_call(xb, w, bg, cos_t, sin_t, cos_m, sin_m, qg, kvg, wq, wk, wv, B, S, tm=256):
    T, D = xb.shape
    NW = w.shape[1]
    nb = S // tm
    MW = MLA_HEADS * LANES
    row = lambda width: pl.BlockSpec((tm, width), lambda b, i: (b * nb + i, 0))
    const = lambda shape: pl.BlockSpec(shape, lambda b, i: (0, 0))
    consts = [qg, kvg, wq, wk, wv]
    hc_specs, hc_shapes = [], []
    for _, d in DIL_PATTERNS:
        hc_specs.append(pl.BlockSpec((None, d, tm // d, HCG_W), lambda b, i: (b, 0, i, 0)))
        hc_shapes.append(jax.ShapeDtypeStruct((B, d, S // d, HCG_W), bf16))
    return pl.pallas_call(
        _inproj_kernel, grid=(B, nb),
        in_specs=[row(D), const((D, NW)), const((1, HG_W)), row(LANES), row(LANES), row(LANES), row(LANES)]
        + [const(a.shape) for a in consts],
        out_specs=[row(HA_OUT_W), row(MW), row(MW), row(MW)] + hc_specs + [row(HG_W)],
        out_shape=[jax.ShapeDtypeStruct((T, HA_OUT_W), bf16)] + [jax.ShapeDtypeStruct((T, MW), bf16)] * 3 + hc_shapes
        + [jax.ShapeDtypeStruct((T, HG_W), bf16)],
        scratch_shapes=[pltpu.VMEM((4, tm, LANES), f32)],
        compiler_params=_cparams(("parallel", "parallel")), name="inproj")(
            xb, w, bg, cos_t, sin_t, cos_m, sin_m, *consts)


def _flash_kernel(*refs, n_maps, tq, tk, n_kv, lam_init, dv):
    if n_maps == 2:
        q_ref, qn_ref, k_ref, v_ref, lam_ref, g_ref, o_ref, qt_sc, m_sc, acc_sc, s_sc, mc_sc = refs
    else:
        q_ref, qn_ref, k_ref, v_ref, o_ref, qt_sc, m_sc, acc_sc, s_sc, mc_sc = refs
    i = pl.program_id(2)
    cur = i % 2
    nxt = 1 - cur

    def load_queries(ref, slot):
        q = ref[...].astype(f32)
        if n_maps == 2:
            lane = lax.broadcasted_iota(jnp.int32, q.shape, 1)
            zero = jnp.zeros_like(q)
            q = jnp.concatenate([jnp.where(lane < HEAD_DIM, q, zero), jnp.where(lane >= HEAD_DIM, q, zero)], axis=0)
        qt_sc[slot] = q.T.astype(bf16)

    def scores(j, buf, slot):
        off = pl.multiple_of(j * tk, tk)
        s = _dot(k_ref[pl.ds(off, tk), :], qt_sc[slot])
        s_sc[buf] = s
        mc_sc[buf] = jnp.max(s, axis=0, keepdims=True)

    def accumulate(j, buf):
        off = pl.multiple_of(j * tk, tk)
        m_prev = m_sc[...]
        m_new = jnp.maximum(m_prev, mc_sc[buf])
        p = jnp.exp2(s_sc[buf] - m_new)
        a = jnp.exp2(m_prev - m_new)
        pv = lax.dot_general(v_ref[pl.ds(off, tk), 0:dv + SUM_ROWS], p.astype(bf16), (((0,), (0,)), ((), ())),
                             preferred_element_type=f32)
        acc_sc[...] = a * acc_sc[...] + pv
        m_sc[...] = m_new

    @pl.when(i == 0)
    def _():
        load_queries(q_ref, cur)
        scores(0, 0, cur)

    load_queries(qn_ref, nxt)
    m_sc[...] = jnp.full(m_sc.shape, -jnp.inf, f32)
    acc_sc[...] = jnp.zeros(acc_sc.shape, f32)

    unroll = min(FLASH_UNROLL, n_kv)
    assert unroll % 2 == 0 and n_kv % unroll == 0

    def body(jj, carry):
        for u in range(unroll):
            j = unroll * jj + u
            scores(j + 1, (u + 1) % 2, cur)
            accumulate(j, u % 2)
        return carry

    lax.fori_loop(0, n_kv // unroll - 1, body, 0)
    for j in range(n_kv - unroll, n_kv):
        if j + 1 < n_kv:
            scores(j + 1, (j + 1) % 2, cur)
        else:
            scores(0, 0, nxt)
        accumulate(j, j % 2)
    o = acc_sc[0:dv, :] / acc_sc[dv:dv + 1, :]
    if n_maps == 2:
        lv = lam_ref[...]
        lam = (jnp.exp(jnp.sum(lv[0:1] * lv[1:2], axis=-1, keepdims=True))
               - jnp.exp(jnp.sum(lv[2:3] * lv[3:4], axis=-1, keepdims=True)) + lam_init)
        o = o[:, :tq] - lam * o[:, tq:]
        ms = jnp.mean(o * o, axis=0, keepdims=True)
        o = o * lax.rsqrt(ms + 1e-6) * g_ref[...] * (1.0 - lam_init)
    if dv < LANES:
        o = jnp.concatenate([o, jnp.zeros((LANES - dv, o.shape[1]), f32)], axis=0)
    o_ref[...] = o.T.astype(o_ref.dtype)


def _flash_call(q_arr, k_arr, v_arr, n_heads, q_blk0, k_blk0, v_blk0, n_maps, tq, tk, name,
                lam_vec=None, norm_g=None, lam_init=0.0, dv=LANES, vw=LANES):
    B, S, _ = q_arr.shape
    cols = n_maps * tq
    last = S // tq - 1
    assert dv + SUM_ROWS <= vw
    in_specs = [pl.BlockSpec((None, tq, LANES), lambda b, h, i: (b, i, q_blk0 + h)),
                pl.BlockSpec((None, tq, LANES), lambda b, h, i: (b, jnp.minimum(i + 1, last), q_blk0 + h)),
                pl.BlockSpec((None, S, LANES), lambda b, h, i: (b, 0, k_blk0 + h)),
                pl.BlockSpec((None, S, vw), lambda b, h, i: (b, 0, v_blk0 + h))]
    args = [q_arr, q_arr, k_arr, v_arr]
    if n_maps == 2:
        in_specs += [pl.BlockSpec((8, LANES), lambda b, h, i: (0, 0)),
                     pl.BlockSpec((LANES, 1), lambda b, h, i: (0, 0))]
        args += [lam_vec, norm_g]
    kern = functools.partial(_flash_kernel, n_maps=n_maps, tq=tq, tk=tk, n_kv=S // tk, lam_init=lam_init,
                             dv=dv)
    return pl.pallas_call(
        kern, grid=(B, n_heads, S // tq), in_specs=in_specs,
        out_specs=pl.BlockSpec((None, tq, LANES), lambda b, h, i: (b, i, h)),
        out_shape=jax.ShapeDtypeStruct((B, S, n_heads * LANES), bf16),
        scratch_shapes=[pltpu.VMEM((2, LANES, cols), bf16), pltpu.VMEM((1, cols), f32),
                        pltpu.VMEM((dv + SUM_ROWS, cols), f32), pltpu.VMEM((2, tk, cols), f32),
                        pltpu.VMEM((2, 1, cols), f32)],
        compiler_params=_cparams(("parallel", "parallel", "arbitrary")), name=name)(*args)


def _dil_kernel(q_ref, kp_ref, kc_ref, kn_ref, vp_ref, vc_ref, vn_ref, o_ref, lse_ref, kcat, vcat, *, tq, sub_len):
    i = pl.program_id(2)
    side = DIL_SIDE
    kcat[0:side, :] = kp_ref[...]
    kcat[side:side + tq, :] = kc_ref[...]
    kcat[side + tq:, :] = kn_ref[...]
    vcat[0:side, :] = vp_ref[...]
    vcat[side:side + tq, :] = vc_ref[...]
    vcat[side + tq:, :] = vn_ref[...]
    qs, ks = 2 * side, 4 * side
    lane = lax.broadcasted_iota(jnp.int32, (qs, LANES), 1)
    ii = lax.broadcasted_iota(jnp.int32, (qs, ks), 0)
    jj = lax.broadcasted_iota(jnp.int32, (qs, ks), 1)
    band = (jj - ii >= 0) & (jj - ii <= 2 * side)
    for sb in range(tq // qs):
        q0 = sb * qs
        pos = jj + (i * tq + q0 - side)
        mask = band & (pos >= 0) & (pos < sub_len)
        lse_tile = jnp.zeros((qs, LANES), f32)
        for h in range(DIL_HEADS):
            t = h // 2
            qt = q_ref[q0:q0 + qs, t * LANES:(t + 1) * LANES]
            keep = (lane < HEAD_DIM) if h % 2 == 0 else (lane >= HEAD_DIM)
            qm = jnp.where(keep, qt, jnp.zeros_like(qt))
            s = _dot_nt(qm, kcat[q0:q0 + ks, t * LANES:(t + 1) * LANES])
            s = jnp.where(mask, s, NEG_INF)
            m = jnp.max(s, axis=-1, keepdims=True)
            p = jnp.exp2(s - m)
            l = jnp.sum(p, axis=-1, keepdims=True)
            o = _dot(p.astype(bf16), vcat[q0:q0 + ks, h * LANES:(h + 1) * LANES]) / l
            o_ref[q0:q0 + qs, h * LANES:(h + 1) * LANES] = o.astype(o_ref.dtype)
            lse_tile = jnp.where(lane == h, m + jnp.log2(l), lse_tile)
        lse_ref[q0:q0 + qs, :] = lse_tile


def _dil_call(hc):
    B, d, L, _ = hc.shape
    tq = min(512, L)
    side = DIL_SIDE
    nblk = tq // side
    last = L // side - 1
    qw, vw = 2 * LANES, 4 * LANES
    prev = lambda i: jnp.maximum(i * nblk - 1, 0)
    nxt = lambda i: jnp.minimum((i + 1) * nblk, last)
    in_specs = [
        pl.BlockSpec((None, None, tq, qw), lambda b, r, i: (b, r, i, 0)),
        pl.BlockSpec((None, None, side, qw), lambda b, r, i: (b, r, prev(i), 1)),
        pl.BlockSpec((None, None, tq, qw), lambda b, r, i: (b, r, i, 1)),
        pl.BlockSpec((None, None, side, qw), lambda b, r, i: (b, r, nxt(i), 1)),
        pl.BlockSpec((None, None, side, vw), lambda b, r, i: (b, r, prev(i), 1)),
        pl.BlockSpec((None, None, tq, vw), lambda b, r, i: (b, r, i, 1)),
        pl.BlockSpec((None, None, side, vw), lambda b, r, i: (b, r, nxt(i), 1)),
    ]
    return pl.pallas_call(
        functools.partial(_dil_kernel, tq=tq, sub_len=L), grid=(B, d, L // tq), in_specs=in_specs,
        out_specs=[pl.BlockSpec((None, None, tq, vw), lambda b, r, i: (b, r, i, 0)),
                   pl.BlockSpec((None, None, tq, LANES), lambda b, r, i: (b, r, i, 0))],
        out_shape=[jax.ShapeDtypeStruct((B, d, L, vw), bf16), jax.ShapeDtypeStruct((B, d, L, LANES), f32)],
        scratch_shapes=[pltpu.VMEM((tq + 2 * side, qw), bf16), pltpu.VMEM((tq + 2 * side, vw), bf16)],
        compiler_params=_cparams(("parallel", "parallel", "arbitrary")),
        name=f"dilated_d{d}")(hc, hc, hc, hc, hc, hc, hc)


def _token_order(ref, stage):
    d, n, w = ref.shape
    if d == 1:
        return ref[0].astype(f32)
    nt = w // LANES
    for r in range(d):
        v = ref[r].astype(f32)
        for t in range(nt):
            stage[t, pl.ds(r, n, stride=d), :] = v[:, t * LANES:(t + 1) * LANES]
    return jnp.concatenate([stage[t] for t in range(nt)], axis=1)


def _merge_kernel(x_ref, oa_ref, ob_ref, o1_ref, o2_ref, o3_ref, l1_ref, l2_ref, l3_ref, g_ref,
                  wa_ref, wb_ref, wc_ref, wo_ref, lg_ref, lb_ref, x32_ref, xb_ref,
                  so2, so3, sl2, sl3, *, alpha):
    l1, l2, l3 = _token_order(l1_ref, None), _token_order(l2_ref, sl2), _token_order(l3_ref, sl3)
    o1, o2, o3 = _token_order(o1_ref, None), _token_order(o2_ref, so2), _token_order(o3_ref, so3)
    mx = jnp.maximum(jnp.maximum(l1, l2), l3)
    e1, e2, e3 = jnp.exp2(l1 - mx), jnp.exp2(l2 - mx), jnp.exp2(l3 - mx)
    den = e1 + e2 + e3
    w1, w2, w3 = e1 / den, e2 / den, e3 / den
    tiles = []
    for h in range(DIL_HEADS):
        sl = slice(h * LANES, (h + 1) * LANES)
        oc = w1[:, h:h + 1] * o1[:, sl] + w2[:, h:h + 1] * o2[:, sl] + w3[:, h:h + 1] * o3[:, sl]
        tiles.append(oc.astype(bf16))
    oc = jnp.concatenate(tiles, axis=1)
    obf = ob_ref[...].astype(f32)
    half = LANES // 2
    ob = jnp.concatenate([obf[:, 2 * j * LANES:(2 * j + 1) * LANES]
                          + pltpu.roll(obf[:, (2 * j + 1) * LANES:(2 * j + 2) * LANES], half, 1)
                          for j in range(MLA_HEADS // 2)], axis=1).astype(bf16)
    D = D_MODEL
    y = (g_ref[:, 0:D].astype(f32) * _dot(oa_ref[...], wa_ref[...])
         + g_ref[:, D:2 * D].astype(f32) * _dot(ob, wb_ref[...])
         + g_ref[:, 2 * D:3 * D].astype(f32) * _dot(oc, wc_ref[...]))
    mix = _dot(y.astype(bf16), wo_ref[...])
    out = _layer_norm(alpha * x_ref[...] + mix, lg_ref[...], lb_ref[...])
    x32_ref[...] = out
    xb_ref[...] = out.astype(bf16)


def _merge_call(x32, oa, ob, dil_o, dil_l, hg, wa, wb, wc, wo, lg, lb, alpha, B, S, tm=256):
    T, D = x32.shape
    nb = S // tm
    row = lambda a: pl.BlockSpec((tm, a.shape[1]), lambda b, i: (b * nb + i, 0))
    cls = lambda a: pl.BlockSpec((None, a.shape[1], tm // a.shape[1], a.shape[3]), lambda b, i: (b, 0, i, 0))
    full = lambda a: pl.BlockSpec(a.shape, lambda b, i: (0, 0))
    consts = [wa, wb, wc, wo, lg, lb]
    ow, lw = dil_o[0].shape[3], dil_l[0].shape[3]
    return pl.pallas_call(
        functools.partial(_merge_kernel, alpha=alpha), grid=(B, nb),
        in_specs=[row(x32), row(oa), row(ob)] + [cls(a) for a in dil_o] + [cls(a) for a in dil_l] + [row(hg)]
        + [full(a) for a in consts],
        out_specs=[pl.BlockSpec((tm, D), lambda b, i: (b * nb + i, 0))] * 2,
        out_shape=[jax.ShapeDtypeStruct((T, D), f32), jax.ShapeDtypeStruct((T, D), bf16)],
        scratch_shapes=[pltpu.VMEM((ow // LANES, tm, LANES), f32), pltpu.VMEM((ow // LANES, tm, LANES), f32),
                        pltpu.VMEM((lw // LANES, tm, LANES), f32), pltpu.VMEM((lw // LANES, tm, LANES), f32)],
        compiler_params=_cparams(("parallel", "parallel")), name="merge")(
            x32, oa, ob, *dil_o, *dil_l, hg, *consts)


def _first_true(flags):
    out, seen = [], None
    for f in flags:
        out.append(f if seen is None else f & ~seen)
        seen = f if seen is None else seen | f
    return out


def _router_kernel(x_ref, rw_ref, rb_ref, gate_ref):
    logits = _dot_nt(rw_ref[...], x_ref[...])
    scores = jax.nn.sigmoid(logits)
    biased = scores + rb_ref[...]
    sc = [scores[e:e + 1, :] for e in range(N_EXPERTS)]
    bi = [biased[e:e + 1, :] for e in range(N_EXPERTS)]
    npg = EXPERTS_PER_GROUP
    gscore = []
    for g in range(N_EXPERT_GROUPS):
        v = bi[g * npg:(g + 1) * npg]
        best = None
        for a in range(npg):
            for b in range(a + 1, npg):
                best = v[a] + v[b] if best is None else jnp.maximum(best, v[a] + v[b])
        gscore.append(best)
    gbest = functools.reduce(jnp.maximum, gscore)
    gsel = _first_true([gs == gbest for gs in gscore])
    zero = jnp.zeros_like(gbest)
    bsel = [functools.reduce(lambda x, y: x + y, [jnp.where(gsel[g], bi[g * npg + j], zero)
                                                  for g in range(N_EXPERT_GROUPS)]) for j in range(npg)]
    ssel = [functools.reduce(lambda x, y: x + y, [jnp.where(gsel[g], sc[g * npg + j], zero)
                                                  for g in range(N_EXPERT_GROUPS)]) for j in range(npg)]
    t1 = functools.reduce(jnp.maximum, bsel)
    i1 = _first_true([b == t1 for b in bsel])
    rest = [jnp.where(i1[j], -jnp.inf, bsel[j]) for j in range(npg)]
    t2 = functools.reduce(jnp.maximum, rest)
    i2 = _first_true([rest[j] == t2 for j in range(npg)])
    w1 = functools.reduce(lambda x, y: x + y, [jnp.where(i1[j], ssel[j], zero) for j in range(npg)])
    w2 = functools.reduce(lambda x, y: x + y, [jnp.where(i2[j], ssel[j], zero) for j in range(npg)])
    den = w1 + w2
    local = [jnp.where(i1[j], w1 / den, zero) + jnp.where(i2[j], w2 / den, zero) for j in range(npg)]
    rows = [jnp.where(gsel[e // npg], local[e % npg], zero) for e in range(N_EXPERTS)]
    gate_ref[...] = jnp.concatenate(rows, axis=0)


def _router_call(xb, rw_t, rb, tm=1024):
    T, D = xb.shape
    return pl.pallas_call(
        _router_kernel, grid=(T // tm,),
        in_specs=[pl.BlockSpec((tm, D), lambda i: (i, 0)), pl.BlockSpec((N_EXPERTS, D), lambda i: (0, 0)),
                  pl.BlockSpec((N_EXPERTS, 1), lambda i: (0, 0))],
        out_specs=pl.BlockSpec((N_EXPERTS, tm), lambda i: (0, i)),
        out_shape=jax.ShapeDtypeStruct((N_EXPERTS, T), f32),
        compiler_params=_cparams(("parallel",)), name="router")(xb, rw_t, rb)


def _moe_kernel(x32_ref, xb_ref, gate_ref, wg_ref, wu_ref, wd_ref, wsg_ref, wsu_ref, wsd_ref, lg_ref, lb_ref,
                o32_ref, ob_ref, acc_sc, *, alpha, n_chunks, per_chunk):
    c = pl.program_id(1)
    x = xb_ref[...]

    @pl.when(c == 0)
    def _():
        hs = jax.nn.silu(_dot(x, wsg_ref[...])) * _dot(x, wsu_ref[...])
        acc_sc[...] = _dot(hs.astype(bf16), wsd_ref[...])

    gate = gate_ref[...]
    hid = []
    for e in range(per_chunk):
        sl = slice(e * EXPERT_HIDDEN, (e + 1) * EXPERT_HIDDEN)
        h = jax.nn.silu(_dot(x, wg_ref[:, sl])) * _dot(x, wu_ref[:, sl]) * gate[:, e:e + 1]
        hid.append(h.astype(bf16))
    acc_sc[...] += _dot(jnp.concatenate(hid, axis=1), wd_ref[...])

    @pl.when(c == n_chunks - 1)
    def _():
        out = _layer_norm(alpha * x32_ref[...] + acc_sc[...], lg_ref[...], lb_ref[...])
        o32_ref[...] = out
        ob_ref[...] = out.astype(bf16)


def _moe_call(x32, xb, gate_c, wg, wu, wd, wsg, wsu, wsd, lg, lb, alpha, tm=1024, per_chunk=4):
    T, D = x32.shape
    n_chunks = N_EXPERTS // per_chunk
    cw = per_chunk * EXPERT_HIDDEN
    row = pl.BlockSpec((tm, D), lambda i, c: (i, 0))
    full = lambda a: pl.BlockSpec(a.shape, lambda i, c: (0, 0))
    return pl.pallas_call(
        functools.partial(_moe_kernel, alpha=alpha, n_chunks=n_chunks, per_chunk=per_chunk),
        grid=(T // tm, n_chunks),
        in_specs=[row, row, pl.BlockSpec((None, tm, per_chunk), lambda i, c: (c, i, 0)),
                  pl.BlockSpec((D, cw), lambda i, c: (0, c)), pl.BlockSpec((D, cw), lambda i, c: (0, c)),
                  pl.BlockSpec((cw, D), lambda i, c: (c, 0)),
                  full(wsg), full(wsu), full(wsd), full(lg), full(lb)],
        out_specs=[row, row],
        out_shape=[jax.ShapeDtypeStruct((T, D), f32), jax.ShapeDtypeStruct((T, D), bf16)],
        scratch_shapes=[pltpu.VMEM((tm, D), f32)],
        compiler_params=_cparams(("parallel", "arbitrary")), name="moe")(
            x32, xb, gate_c, wg, wu, wd, wsg, wsu, wsd, lg, lb)


def _rope_tables(positions):
    pos = positions.astype(f32).reshape(-1, 1)
    T = pos.shape[0]
    ones, zeros = jnp.ones, jnp.zeros
    inv_p = ROPE_THETA ** (-jnp.arange(0, ROT_DIM, 2, dtype=f32) / ROT_DIM)
    ang = pos * inv_p
    c, s = jnp.cos(ang), jnp.sin(ang)
    pad = HEAD_DIM - ROT_DIM
    cos_p = jnp.tile(jnp.concatenate([c, c, ones((T, pad), f32)], axis=1), (1, 2))
    sin_p = jnp.tile(jnp.concatenate([-s, s, zeros((T, pad), f32)], axis=1), (1, 2))
    inv_m = ROPE_THETA ** (-jnp.arange(0, MLA_ROPE_DIM, 2, dtype=f32) / MLA_ROPE_DIM)
    ang = pos * inv_m
    c, s = jnp.cos(ang), jnp.sin(ang)
    tail = LANES - MLA_NOPE_DIM - MLA_ROPE_DIM
    cos_m = jnp.concatenate([ones((T, MLA_NOPE_DIM), f32), c, c, ones((T, tail), f32)], axis=1)
    sin_m = jnp.concatenate([zeros((T, MLA_NOPE_DIM), f32), -s, s, zeros((T, tail), f32)], axis=1)
    return cos_p, sin_p, cos_m, sin_m


def _pack_w_in(w):
    offs = [0]
    for sz in IN_SIZES:
        offs.append(offs[-1] + sz)
    a_q, a_k, a_v, b_cq, b_ckv, b_kr, c_q, c_k, c_v, g = [w[:, offs[i]:offs[i + 1]] for i in range(len(IN_SIZES))]
    D = w.shape[0]
    z = lambda n: jnp.zeros((D, n), w.dtype)
    kr_pad = jnp.concatenate([z(MLA_NOPE_DIM), b_kr, z(LANES - MLA_NOPE_DIM - MLA_ROPE_DIM)], axis=1)
    cols = [a_q, a_k, a_v, b_ckv, kr_pad, b_cq]
    for gi in range(DIL_GROUPS):
        cols += [c_q[:, 256 * gi:256 * (gi + 1)], c_k[:, 256 * gi:256 * (gi + 1)], c_v[:, 512 * gi:512 * (gi + 1)]]
    cols.append(g)
    return jnp.concatenate(cols, axis=1).astype(bf16)


def _pad_heads(w, n_heads, lo, hi):
    K = w.shape[0]
    per = w.shape[1] // n_heads
    wh = w.reshape(K, n_heads, per)[:, :, lo:hi]
    wh = jnp.pad(wh, ((0, 0), (0, 0), (0, LANES - (hi - lo))))
    return wh.reshape(K, n_heads * LANES).astype(bf16)


def kernel(x, positions, ln_in_g, ln_in_b, w_in, b_gate, lam_q1, lam_k1, lam_q2, lam_k2, diff_norm_g, mla_q_norm_g, mla_kv_norm_g, w_mla_qb, w_mla_kvb, w_branch_a, w_branch_b, w_branch_c, w_out, ln1_g, ln1_b, router_w, router_bias, w_exp_gate, w_exp_up, w_exp_down, w_sh_gate, w_sh_up, w_sh_down, ln2_g, ln2_b):
    B, S, D = x.shape
    T = B * S
    depth = w_in.shape[0]
    alpha = (2 * depth) ** 0.25
    cos_p, sin_p, cos_m, sin_m = _rope_tables(positions)
    x32, xb = _ln_call(x.reshape(T, D), ln_in_g.reshape(1, D), ln_in_b.reshape(1, D))
    rw_t = router_w.T.astype(bf16)
    rb = router_bias.reshape(N_EXPERTS, 1).astype(f32)
    vec = lambda v: v.reshape(1, -1).astype(f32)
    per_q = MLA_NOPE_DIM + MLA_ROPE_DIM
    per_kv = MLA_NOPE_DIM + MLA_V_DIM
    for l in range(depth):
        lam_init = 0.8 - 0.6 * math.exp(-0.3 * l)
        ha, qb, kb, vb, hc0, hc1, hc2, hg = _inproj_call(
            xb, _pack_w_in(w_in[l]), vec(b_gate[l]), cos_p, sin_p, cos_m, sin_m,
            vec(mla_q_norm_g[l]), vec(mla_kv_norm_g[l]), _pad_heads(w_mla_qb[l], MLA_HEADS, 0, per_q),
            _pad_heads(w_mla_kvb[l], MLA_HEADS, 0, MLA_NOPE_DIM), _pad_heads(w_mla_kvb[l], MLA_HEADS, MLA_NOPE_DIM, per_kv),
            B, S)
        lam_vec = jnp.pad(jnp.stack([lam_q1[l], lam_k1[l], lam_q2[l], lam_k2[l]]).astype(f32),
                          ((0, 4), (0, LANES - HEAD_DIM)))
        ha3 = ha.reshape(B, S, HA_OUT_W)
        oa = _flash_call(ha3, ha3, ha3, DIFF_HEADS, 0, DIFF_HEADS, DIFF_HEADS, n_maps=2, tq=FLASH_COLS // 2, tk=FLASH_TK,
                         name="diff_attn", lam_vec=lam_vec, norm_g=diff_norm_g[l].reshape(-1, 1).astype(f32),
                         lam_init=lam_init, vw=2 * LANES)
        ob = _flash_call(qb.reshape(B, S, -1), kb.reshape(B, S, -1), vb.reshape(B, S, -1), MLA_HEADS, 0, 0, 0,
                         n_maps=1, tq=FLASH_COLS, tk=FLASH_TK, name="mla_attn", dv=MLA_V_DIM)
        dil = [_dil_call(hc) for hc in (hc0, hc1, hc2)]
        x32, xb = _merge_call(
            x32, oa.reshape(T, -1), ob.reshape(T, -1), [o for o, _ in dil], [lse for _, lse in dil],
            hg, w_branch_a[l].astype(bf16), w_branch_b[l].astype(bf16), w_branch_c[l].astype(bf16),
            w_out[l].astype(bf16), vec(ln1_g[l]), vec(ln1_b[l]), alpha, B, S)
        gate_t = _router_call(xb, rw_t, rb)
        per_chunk = 4
        gate_c = gate_t.reshape(N_EXPERTS // per_chunk, per_chunk, T).transpose(0, 2, 1)
        H = EXPERT_HIDDEN
        wg = w_exp_gate[l].transpose(1, 0, 2).reshape(D, N_EXPERTS * H).astype(bf16)
        wu = w_exp_up[l].transpose(1, 0, 2).reshape(D, N_EXPERTS * H).astype(bf16)
        wd = w_exp_down[l].reshape(N_EXPERTS * H, D).astype(bf16)
        x32, xb = _moe_call(x32, xb, gate_c, wg, wu, wd, w_sh_gate[l].astype(bf16), w_sh_up[l].astype(bf16),
                            w_sh_down[l].astype(bf16), vec(ln2_g[l]), vec(ln2_b[l]), alpha, per_chunk=per_chunk)
    return x32.reshape(B, S, D)
```

```python
import functools
import math

import jax
import jax.numpy as jnp
from jax import lax
from jax.experimental import pallas as pl
from jax.experimental.pallas import tpu as pltpu

f32 = jnp.float32
bf16 = jnp.bfloat16

D_MODEL = 1024
HEAD_DIM = 64
ROPE_THETA = 500000.0
ROT_DIM = HEAD_DIM // 4
DIFF_HEADS = 4
MLA_HEADS = 8
MLA_Q_RANK = 384
MLA_KV_RANK = 256
MLA_NOPE_DIM = 64
MLA_ROPE_DIM = 32
MLA_V_DIM = 64
DIL_PATTERNS = ((128, 1), (512, 4), (2048, 16))
DIL_GROUPS = 3
DIL_HEADS = 4
DIL_SIDE = 64
N_EXPERTS = 16
N_EXPERT_GROUPS = 4
EXPERTS_PER_GROUP = 4
EXPERT_HIDDEN = 256
IN_SIZES = (512, 512, 512, 384, 256, 32, 768, 768, 1536, 3072)
NEG_INF = -1e30
LOG2E = math.log2(math.e)

LANES = 128
VMEM_LIMIT = 52 * 1024 * 1024

HA_W = 1536
HA_OUT_W = 2048
HB_W = 768
HCG_W = 1024
HC_W = 3 * HCG_W
HG_W = 3072

FLASH_TK = 512
FLASH_COLS = 1024
FLASH_UNROLL = 8
SUM_ROWS = 16


def _cparams(sem):
    return pltpu.CompilerParams(dimension_semantics=sem, vmem_limit_bytes=VMEM_LIMIT)


def _layer_norm(z, g, b):
    mu = jnp.mean(z, axis=-1, keepdims=True)
    zc = z - mu
    var = jnp.mean(zc * zc, axis=-1, keepdims=True)
    return zc * lax.rsqrt(var + 1e-5) * g + b


def _rms_norm(z, g):
    return z * lax.rsqrt(jnp.mean(z * z, axis=-1, keepdims=True) + 1e-6) * g


def _dot(a, b):
    return jnp.dot(a, b, preferred_element_type=f32)


def _dot_nt(a, b):
    return lax.dot_general(a, b, (((1,), (1,)), ((), ())), preferred_element_type=f32)


def _rope_tile(xt, cos_t, sin_t, first_half, shift):
    xr = jnp.where(first_half, pltpu.roll(xt, LANES - shift, 1), pltpu.roll(xt, shift, 1))
    return xt * cos_t + xr * sin_t


def _ln_kernel(x_ref, g_ref, b_ref, o32_ref, ob_ref):
    y = _layer_norm(x_ref[...], g_ref[...], b_ref[...])
    o32_ref[...] = y
    ob_ref[...] = y.astype(bf16)


def _ln_call(x, g, b, tm=512):
    T, D = x.shape
    row = pl.BlockSpec((tm, D), lambda i: (i, 0))
    vec = pl.BlockSpec((1, D), lambda i: (0, 0))
    return pl.pallas_call(
        _ln_kernel, grid=(T // tm,), in_specs=[row, vec, vec], out_specs=[row, row],
        out_shape=[jax.ShapeDtypeStruct((T, D), f32), jax.ShapeDtypeStruct((T, D), bf16)],
        compiler_params=_cparams(("parallel",)), name="ln_in")(x, g, b)


def _inproj_plan():
    qscale = HEAD_DIM ** -0.5 * LOG2E
    plan = []
    for c in range(0, 512, 256):
        plan.append((c, 256, "rope", 0, c, qscale))
    for c in range(512, 1024, 256):
        plan.append((c, 256, "rope", 0, c, 1.0))
    plan.append((1024, 512, "value", 0, 1024, 1.0))
    plan.append((HA_W, HB_W, "mla", 1, 0, 1.0))
    base = HA_W + HB_W
    for g in range(DIL_GROUPS):
        o = HCG_W * g
        plan.append((base + o, 256, "rope", 2 + g, 0, qscale))
        plan.append((base + o + 256, 256, "rope", 2 + g, 256, 1.0))
        plan.append((base + o + 512, 512, "plain", 2 + g, 512, 1.0))
    base = HA_W + HB_W + HC_W
    for c in range(0, HG_W, 512):
        plan.append((base + c, 512, "gate", 2 + DIL_GROUPS, c, 1.0))
    return plan


def _mla_up_projections(acc, lane, cm_ref, sm_ref, qg_ref, kvg_ref, wq_ref, wk_ref, wv_ref, qb_ref, kb_ref, vb_ref):
    cos_t, sin_t = cm_ref[...], sm_ref[...]
    first_half = lane < MLA_NOPE_DIM + MLA_ROPE_DIM // 2
    shift = MLA_ROPE_DIM // 2
    scale = (MLA_NOPE_DIM + MLA_ROPE_DIM) ** -0.5 * LOG2E
    cn = _rms_norm(acc[:, 0:MLA_KV_RANK], kvg_ref[...]).astype(bf16)
    kr = _rope_tile(acc[:, MLA_KV_RANK:MLA_KV_RANK + LANES], cos_t, sin_t, first_half, shift)
    qn = _rms_norm(acc[:, MLA_KV_RANK + LANES:], qg_ref[...]).astype(bf16)
    q_all = _dot(qn, wq_ref[...])
    k_all = _dot(cn, wk_ref[...])
    v_all = _dot(cn, wv_ref[...])
    for h in range(MLA_HEADS):
        sl = slice(h * LANES, (h + 1) * LANES)
        q = _rope_tile(q_all[:, sl], cos_t, sin_t, first_half, shift) * scale
        qb_ref[:, sl] = q.astype(qb_ref.dtype)
        kb_ref[:, sl] = (k_all[:, sl] + kr).astype(kb_ref.dtype)
        vb_ref[:, sl] = jnp.where(lane == MLA_V_DIM, 1.0, v_all[:, sl]).astype(vb_ref.dtype)


def _inproj_kernel(x_ref, w_ref, bg_ref, cp_ref, sp_ref, cm_ref, sm_ref, qg_ref, kvg_ref, wq_ref, wk_ref, wv_ref,
                   ha_ref, qb_ref, kb_ref, vb_ref, hc0_ref, hc1_ref, hc2_ref, hg_ref, stage):
    outs = (ha_ref, None, hc0_ref, hc1_ref, hc2_ref, hg_ref)
    x = x_ref[...]
    tm = x.shape[0]
    cos_t = cp_ref[...]
    sin_t = sp_ref[...]
    lane = lax.broadcasted_iota(jnp.int32, cos_t.shape, 1)
    first_half = (lane % HEAD_DIM) < (ROT_DIM // 2)
    for (c0, w, kind, oi, o0, scale) in _inproj_plan():
        acc = _dot(x, w_ref[:, c0:c0 + w])
        if kind == "rope":
            tiles = []
            for t in range(w // LANES):
                y = _rope_tile(acc[:, t * LANES:(t + 1) * LANES], cos_t, sin_t, first_half, ROT_DIM // 2)
                tiles.append(y * scale if scale != 1.0 else y)
            acc = jnp.concatenate(tiles, axis=1)
        elif kind == "gate":
            acc = jax.nn.sigmoid(acc + bg_ref[:, o0:o0 + w])
        if kind == "mla":
            _mla_up_projections(acc, lane, cm_ref, sm_ref, qg_ref, kvg_ref, wq_ref, wk_ref, wv_ref,
                                qb_ref, kb_ref, vb_ref)
            continue
        out = outs[oi]
        if kind == "value":
            ones_tile = (lane == 0).astype(out.dtype)
            for h in range(w // LANES):
                out[:, o0 + 2 * h * LANES:o0 + (2 * h + 1) * LANES] = acc[:, h * LANES:(h + 1) * LANES].astype(out.dtype)
                out[:, o0 + (2 * h + 1) * LANES:o0 + (2 * h + 2) * LANES] = ones_tile
        elif 2 <= oi < 2 + DIL_GROUPS:
            d = DIL_PATTERNS[oi - 2][1]
            if d == 1:
                out[0, :, o0:o0 + w] = acc.astype(out.dtype)
            else:
                nt = w // LANES
                for t in range(nt):
                    stage[t] = acc[:, t * LANES:(t + 1) * LANES]
                for r in range(d):
                    rows = [stage[t, pl.ds(r, tm // d, stride=d), :] for t in range(nt)]
                    out[r, :, o0:o0 + w] = jnp.concatenate(rows, axis=1).astype(out.dtype)
        else:
            out[:, o0:o0 + w] = acc.astype(out.dtype)


def _inproj_call(xb, w, bg, cos_t, sin_t, cos_m, sin_m, qg, kvg, wq, wk, wv, B, S, tm=256):
    T, D = xb.shape
    NW = w.shape[1]
    nb = S // tm
    MW = MLA_HEADS * LANES
    row = lambda width: pl.BlockSpec((tm, width), lambda b, i: (b * nb + i, 0))
    const = lambda shape: pl.BlockSpec(shape, lambda b, i: (0, 0))
    consts = [qg, kvg, wq, wk, wv]
    hc_specs, hc_shapes = [], []
    for _, d in DIL_PATTERNS:
        hc_specs.append(pl.BlockSpec((None, d, tm // d, HCG_W), lambda b, i: (b, 0, i, 0)))
        hc_shapes.append(jax.ShapeDtypeStruct((B, d, S // d, HCG_W), bf16))
    return pl.pallas_call(
        _inproj_kernel, grid=(B, nb),
        in_specs=[row(D), const((D, NW)), const((1, HG_W)), row(LANES), row(LANES), row(LANES), row(LANES)]
        + [const(a.shape) for a in consts],
        out_specs=[row(HA_OUT_W), row(MW), row(MW), row(MW)] + hc_specs + [row(HG_W)],
        out_shape=[jax.ShapeDtypeStruct((T, HA_OUT_W), bf16)] + [jax.ShapeDtypeStruct((T, MW), bf16)] * 3 + hc_shapes
        + [jax.ShapeDtypeStruct((T, HG_W), bf16)],
        scratch_shapes=[pltpu.VMEM((4, tm, LANES), f32)],
        compiler_params=_cparams(("parallel", "parallel")), name="inproj")(
            xb, w, bg, cos_t, sin_t, cos_m, sin_m, *consts)


def _flash_kernel(*refs, n_maps, tq, tk, n_kv, lam_init, dv):
    if n_maps == 2:
        q_ref, qn_ref, k_ref, v_ref, lam_ref, g_ref, o_ref, qt_sc, m_sc, acc_sc, s_sc, mc_sc = refs
    else:
        q_ref, qn_ref, k_ref, v_ref, o_ref, qt_sc, m_sc, acc_sc, s_sc, mc_sc = refs
    i = pl.program_id(2)
    cur = i % 2
    nxt = 1 - cur

    def load_queries(ref, slot):
        q = ref[...].astype(f32)
        if n_maps == 2:
            lane = lax.broadcasted_iota(jnp.int32, q.shape, 1)
            zero = jnp.zeros_like(q)
            q = jnp.concatenate([jnp.where(lane < HEAD_DIM, q, zero), jnp.where(lane >= HEAD_DIM, q, zero)], axis=0)
        qt_sc[slot] = q.T.astype(bf16)

    def scores(j, buf, slot):
        off = pl.multiple_of(j * tk, tk)
        s = _dot(k_ref[pl.ds(off, tk), :], qt_sc[slot])
        s_sc[buf] = s
        mc_sc[buf] = jnp.max(s, axis=0, keepdims=True)

    def accumulate(j, buf):
        off = pl.multiple_of(j * tk, tk)
        m_prev = m_sc[...]
        m_new = jnp.maximum(m_prev, mc_sc[buf])
        p = jnp.exp2(s_sc[buf] - m_new)
        a = jnp.exp2(m_prev - m_new)
        pv = lax.dot_general(v_ref[pl.ds(off, tk), 0:dv + SUM_ROWS], p.astype(bf16), (((0,), (0,)), ((), ())),
                             preferred_element_type=f32)
        acc_sc[...] = a * acc_sc[...] + pv
        m_sc[...] = m_new

    @pl.when(i == 0)
    def _():
        load_queries(q_ref, cur)
        scores(0, 0, cur)

    load_queries(qn_ref, nxt)
    m_sc[...] = jnp.full(m_sc.shape, -jnp.inf, f32)
    acc_sc[...] = jnp.zeros(acc_sc.shape, f32)

    unroll = min(FLASH_UNROLL, n_kv)
    assert unroll % 2 == 0 and n_kv % unroll == 0

    def body(jj, carry):
        for u in range(unroll):
            j = unroll * jj + u
            scores(j + 1, (u + 1) % 2, cur)
            accumulate(j, u % 2)
        return carry

    lax.fori_loop(0, n_kv // unroll - 1, body, 0)
    for j in range(n_kv - unroll, n_kv):
        if j + 1 < n_kv:
            scores(j + 1, (j + 1) % 2, cur)
        else:
            scores(0, 0, nxt)
        accumulate(j, j % 2)
    o = acc_sc[0:dv, :] / acc_sc[dv:dv + 1, :]
    if n_maps == 2:
        lv = lam_ref[...]
        lam = (jnp.exp(jnp.sum(lv[0:1] * lv[1:2], axis=-1, keepdims=True))
               - jnp.exp(jnp.sum(lv[2:3] * lv[3:4], axis=-1, keepdims=True)) + lam_init)
        o = o[:, :tq] - lam * o[:, tq:]
        ms = jnp.mean(o * o, axis=0, keepdims=True)
        o = o * lax.rsqrt(ms + 1e-6) * g_ref[...] * (1.0 - lam_init)
    if dv < LANES:
        o = jnp.concatenate([o, jnp.zeros((LANES - dv, o.shape[1]), f32)], axis=0)
    o_ref[...] = o.T.astype(o_ref.dtype)


def _flash_call(q_arr, k_arr, v_arr, n_heads, q_blk0, k_blk0, v_blk0, n_maps, tq, tk, name,
                lam_vec=None, norm_g=None, lam_init=0.0, dv=LANES, vw=LANES):
    B, S, _ = q_arr.shape
    cols = n_maps * tq
    last = S // tq - 1
    assert dv + SUM_ROWS <= vw
    in_specs = [pl.BlockSpec((None, tq, LANES), lambda b, h, i: (b, i, q_blk0 + h)),
                pl.BlockSpec((None, tq, LANES), lambda b, h, i: (b, jnp.minimum(i + 1, last), q_blk0 + h)),
                pl.BlockSpec((None, S, LANES), lambda b, h, i: (b, 0, k_blk0 + h)),
                pl.BlockSpec((None, S, vw), lambda b, h, i: (b, 0, v_blk0 + h))]
    args = [q_arr, q_arr, k_arr, v_arr]
    if n_maps == 2:
        in_specs += [pl.BlockSpec((8, LANES), lambda b, h, i: (0, 0)),
                     pl.BlockSpec((LANES, 1), lambda b, h, i: (0, 0))]
        args += [lam_vec, norm_g]
    kern = functools.partial(_flash_kernel, n_maps=n_maps, tq=tq, tk=tk, n_kv=S // tk, lam_init=lam_init,
                             dv=dv)
    return pl.pallas_call(
        kern, grid=(B, n_heads, S // tq), in_specs=in_specs,
        out_specs=pl.BlockSpec((None, tq, LANES), lambda b, h, i: (b, i, h)),
        out_shape=jax.ShapeDtypeStruct((B, S, n_heads * LANES), bf16),
        scratch_shapes=[pltpu.VMEM((2, LANES, cols), bf16), pltpu.VMEM((1, cols), f32),
                        pltpu.VMEM((dv + SUM_ROWS, cols), f32), pltpu.VMEM((2, tk, cols), f32),
                        pltpu.VMEM((2, 1, cols), f32)],
        compiler_params=_cparams(("parallel", "parallel", "arbitrary")), name=name)(*args)


def _dil_kernel(q_ref, kp_ref, kc_ref, kn_ref, vp_ref, vc_ref, vn_ref, o_ref, lse_ref, kcat, vcat, *, tq, sub_len):
    i = pl.program_id(2)
    side = DIL_SIDE
    kcat[0:side, :] = kp_ref[...]
    kcat[side:side + tq, :] = kc_ref[...]
    kcat[side + tq:, :] = kn_ref[...]
    vcat[0:side, :] = vp_ref[...]
    vcat[side:side + tq, :] = vc_ref[...]
    vcat[side + tq:, :] = vn_ref[...]
    qs, ks = 2 * side, 4 * side
    lane = lax.broadcasted_iota(jnp.int32, (qs, LANES), 1)
    ii = lax.broadcasted_iota(jnp.int32, (qs, ks), 0)
    jj = lax.broadcasted_iota(jnp.int32, (qs, ks), 1)
    band = (jj - ii >= 0) & (jj - ii <= 2 * side)
    for sb in range(tq // qs):
        q0 = sb * qs
        pos = jj + (i * tq + q0 - side)
        mask = band & (pos >= 0) & (pos < sub_len)
        lse_tile = jnp.zeros((qs, LANES), f32)
        for h in range(DIL_HEADS):
            t = h // 2
            qt = q_ref[q0:q0 + qs, t * LANES:(t + 1) * LANES]
            keep = (lane < HEAD_DIM) if h % 2 == 0 else (lane >= HEAD_DIM)
            qm = jnp.where(keep, qt, jnp.zeros_like(qt))
            s = _dot_nt(qm, kcat[q0:q0 + ks, t * LANES:(t + 1) * LANES])
            s = jnp.where(mask, s, NEG_INF)
            m = jnp.max(s, axis=-1, keepdims=True)
            p = jnp.exp2(s - m)
            l = jnp.sum(p, axis=-1, keepdims=True)
            o = _dot(p.astype(bf16), vcat[q0:q0 + ks, h * LANES:(h + 1) * LANES]) / l
            o_ref[q0:q0 + qs, h * LANES:(h + 1) * LANES] = o.astype(o_ref.dtype)
            lse_tile = jnp.where(lane == h, m + jnp.log2(l), lse_tile)
        lse_ref[q0:q0 + qs, :] = lse_tile


def _dil_call(hc):
    B, d, L, _ = hc.shape
    tq = min(512, L)
    side = DIL_SIDE
    nblk = tq // side
    last = L // side - 1
    qw, vw = 2 * LANES, 4 * LANES
    prev = lambda i: jnp.maximum(i * nblk - 1, 0)
    nxt = lambda i: jnp.minimum((i + 1) * nblk, last)
    in_specs = [
        pl.BlockSpec((None, None, tq, qw), lambda b, r, i: (b, r, i, 0)),
        pl.BlockSpec((None, None, side, qw), lambda b, r, i: (b, r, prev(i), 1)),
        pl.BlockSpec((None, None, tq, qw), lambda b, r, i: (b, r, i, 1)),
        pl.BlockSpec((None, None, side, qw), lambda b, r, i: (b, r, nxt(i), 1)),
        pl.BlockSpec((None, None, side, vw), lambda b, r, i: (b, r, prev(i), 1)),
        pl.BlockSpec((None, None, tq, vw), lambda b, r, i: (b, r, i, 1)),
        pl.BlockSpec((None, None, side, vw), lambda b, r, i: (b, r, nxt(i), 1)),
    ]
    return pl.pallas_call(
        functools.partial(_dil_kernel, tq=tq, sub_len=L), grid=(B, d, L // tq), in_specs=in_specs,
        out_specs=[pl.BlockSpec((None, None, tq, vw), lambda b, r, i: (b, r, i, 0)),
                   pl.BlockSpec((None, None, tq, LANES), lambda b, r, i: (b, r, i, 0))],
        out_shape=[jax.ShapeDtypeStruct((B, d, L, vw), bf16), jax.ShapeDtypeStruct((B, d, L, LANES), f32)],
        scratch_shapes=[pltpu.VMEM((tq + 2 * side, qw), bf16), pltpu.VMEM((tq + 2 * side, vw), bf16)],
        compiler_params=_cparams(("parallel", "parallel", "arbitrary")),
        name=f"dilated_d{d}")(hc, hc, hc, hc, hc, hc, hc)


def _token_order(ref, stage):
    d, n, w = ref.shape
    if d == 1:
        return ref[0].astype(f32)
    nt = w // LANES
    for r in range(d):
        v = ref[r].astype(f32)
        for t in range(nt):
            stage[t, pl.ds(r, n, stride=d), :] = v[:, t * LANES:(t + 1) * LANES]
    return jnp.concatenate([stage[t] for t in range(nt)], axis=1)


def _merge_kernel(x_ref, oa_ref, ob_ref, o1_ref, o2_ref, o3_ref, l1_ref, l2_ref, l3_ref, g_ref,
                  wa_ref, wb_ref, wc_ref, wo_ref, lg_ref, lb_ref, rw_ref, rb_ref, x32_ref, xb_ref, gate_ref,
                  so2, so3, sl2, sl3, *, alpha):
    l1, l2, l3 = _token_order(l1_ref, None), _token_order(l2_ref, sl2), _token_order(l3_ref, sl3)
    o1, o2, o3 = _token_order(o1_ref, None), _token_order(o2_ref, so2), _token_order(o3_ref, so3)
    mx = jnp.maximum(jnp.maximum(l1, l2), l3)
    e1, e2, e3 = jnp.exp2(l1 - mx), jnp.exp2(l2 - mx), jnp.exp2(l3 - mx)
    den = e1 + e2 + e3
    w1, w2, w3 = e1 / den, e2 / den, e3 / den
    tiles = []
    for h in range(DIL_HEADS):
        sl = slice(h * LANES, (h + 1) * LANES)
        oc = w1[:, h:h + 1] * o1[:, sl] + w2[:, h:h + 1] * o2[:, sl] + w3[:, h:h + 1] * o3[:, sl]
        tiles.append(oc.astype(bf16))
    oc = jnp.concatenate(tiles, axis=1)
    obf = ob_ref[...].astype(f32)
    half = LANES // 2
    ob = jnp.concatenate([obf[:, 2 * j * LANES:(2 * j + 1) * LANES]
                          + pltpu.roll(obf[:, (2 * j + 1) * LANES:(2 * j + 2) * LANES], half, 1)
                          for j in range(MLA_HEADS // 2)], axis=1).astype(bf16)
    D = D_MODEL
    y = (g_ref[:, 0:D].astype(f32) * _dot(oa_ref[...], wa_ref[...])
         + g_ref[:, D:2 * D].astype(f32) * _dot(ob, wb_ref[...])
         + g_ref[:, 2 * D:3 * D].astype(f32) * _dot(oc, wc_ref[...]))
    mix = _dot(y.astype(bf16), wo_ref[...])
    out = _layer_norm(alpha * x_ref[...] + mix, lg_ref[...], lb_ref[...])
    x32_ref[...] = out
    xb = out.astype(bf16)
    xb_ref[...] = xb
    gate_ref[...] = _route(_dot_nt(rw_ref[...], xb), rb_ref[...])


def _merge_call(x32, oa, ob, dil_o, dil_l, hg, wa, wb, wc, wo, lg, lb, rw_t, rb, alpha, B, S, tm=256):
    T, D = x32.shape
    nb = S // tm
    row = lambda a: pl.BlockSpec((tm, a.shape[1]), lambda b, i: (b * nb + i, 0))
    cls = lambda a: pl.BlockSpec((None, a.shape[1], tm // a.shape[1], a.shape[3]), lambda b, i: (b, 0, i, 0))
    full = lambda a: pl.BlockSpec(a.shape, lambda b, i: (0, 0))
    consts = [wa, wb, wc, wo, lg, lb, rw_t, rb]
    ow, lw = dil_o[0].shape[3], dil_l[0].shape[3]
    return pl.pallas_call(
        functools.partial(_merge_kernel, alpha=alpha), grid=(B, nb),
        in_specs=[row(x32), row(oa), row(ob)] + [cls(a) for a in dil_o] + [cls(a) for a in dil_l] + [row(hg)]
        + [full(a) for a in consts],
        out_specs=[pl.BlockSpec((tm, D), lambda b, i: (b * nb + i, 0))] * 2
        + [pl.BlockSpec((N_EXPERTS, tm), lambda b, i: (0, b * nb + i))],
        out_shape=[jax.ShapeDtypeStruct((T, D), f32), jax.ShapeDtypeStruct((T, D), bf16),
                   jax.ShapeDtypeStruct((N_EXPERTS, T), f32)],
        scratch_shapes=[pltpu.VMEM((ow // LANES, tm, LANES), f32), pltpu.VMEM((ow // LANES, tm, LANES), f32),
                        pltpu.VMEM((lw // LANES, tm, LANES), f32), pltpu.VMEM((lw // LANES, tm, LANES), f32)],
        compiler_params=_cparams(("parallel", "parallel")), name="merge")(
            x32, oa, ob, *dil_o, *dil_l, hg, *consts)


def _first_true(flags):
    out, seen = [], None
    for f in flags:
        out.append(f if seen is None else f & ~seen)
        seen = f if seen is None else seen | f
    return out


def _route(logits, bias):
    scores = jax.nn.sigmoid(logits)
    biased = scores + bias
    sc = [scores[e:e + 1, :] for e in range(N_EXPERTS)]
    bi = [biased[e:e + 1, :] for e in range(N_EXPERTS)]
    npg = EXPERTS_PER_GROUP
    gscore = []
    for g in range(N_EXPERT_GROUPS):
        v = bi[g * npg:(g + 1) * npg]
        best = None
        for a in range(npg):
            for b in range(a + 1, npg):
                best = v[a] + v[b] if best is None else jnp.maximum(best, v[a] + v[b])
        gscore.append(best)
    gbest = functools.reduce(jnp.maximum, gscore)
    gsel = _first_true([gs == gbest for gs in gscore])
    zero = jnp.zeros_like(gbest)
    bsel = [functools.reduce(lambda x, y: x + y, [jnp.where(gsel[g], bi[g * npg + j], zero)
                                                  for g in range(N_EXPERT_GROUPS)]) for j in range(npg)]
    ssel = [functools.reduce(lambda x, y: x + y, [jnp.where(gsel[g], sc[g * npg + j], zero)
                                                  for g in range(N_EXPERT_GROUPS)]) for j in range(npg)]
    t1 = functools.reduce(jnp.maximum, bsel)
    i1 = _first_true([b == t1 for b in bsel])
    rest = [jnp.where(i1[j], -jnp.inf, bsel[j]) for j in range(npg)]
    t2 = functools.reduce(jnp.maximum, rest)
    i2 = _first_true([rest[j] == t2 for j in range(npg)])
    w1 = functools.reduce(lambda x, y: x + y, [jnp.where(i1[j], ssel[j], zero) for j in range(npg)])
    w2 = functools.reduce(lambda x, y: x + y, [jnp.where(i2[j], ssel[j], zero) for j in range(npg)])
    den = w1 + w2
    local = [jnp.where(i1[j], w1 / den, zero) + jnp.where(i2[j], w2 / den, zero) for j in range(npg)]
    rows = [jnp.where(gsel[e // npg], local[e % npg], zero) for e in range(N_EXPERTS)]
    return jnp.concatenate(rows, axis=0)


def _moe_kernel(x32_ref, xb_ref, gate_ref, wg_ref, wu_ref, wd_ref, wsg_ref, wsu_ref, wsd_ref, lg_ref, lb_ref,
                o32_ref, ob_ref, acc_sc, *, alpha, n_chunks, per_chunk):
    c = pl.program_id(1)
    x = xb_ref[...]

    @pl.when(c == 0)
    def _():
        hs = jax.nn.silu(_dot(x, wsg_ref[...])) * _dot(x, wsu_ref[...])
        acc_sc[...] = _dot(hs.astype(bf16), wsd_ref[...])

    gate = gate_ref[...]
    hid = []
    for e in range(per_chunk):
        h = jax.nn.silu(_dot(x, wg_ref[e])) * _dot(x, wu_ref[e]) * gate[:, e:e + 1]
        hid.append(h.astype(bf16))
    acc_sc[...] += _dot(jnp.concatenate(hid, axis=1), wd_ref[...])

    @pl.when(c == n_chunks - 1)
    def _():
        out = _layer_norm(alpha * x32_ref[...] + acc_sc[...], lg_ref[...], lb_ref[...])
        o32_ref[...] = out
        ob_ref[...] = out.astype(bf16)


def _moe_call(x32, xb, gate_c, wg, wu, wd, wsg, wsu, wsd, lg, lb, alpha, tm=1024, per_chunk=4):
    T, D = x32.shape
    n_chunks = N_EXPERTS // per_chunk
    cw = per_chunk * EXPERT_HIDDEN
    row = pl.BlockSpec((tm, D), lambda i, c: (i, 0))
    full = lambda a: pl.BlockSpec(a.shape, lambda i, c: (0, 0))
    return pl.pallas_call(
        functools.partial(_moe_kernel, alpha=alpha, n_chunks=n_chunks, per_chunk=per_chunk),
        grid=(T // tm, n_chunks),
        in_specs=[row, row, pl.BlockSpec((None, tm, per_chunk), lambda i, c: (c, i, 0)),
                  pl.BlockSpec((per_chunk, D, EXPERT_HIDDEN), lambda i, c: (c, 0, 0)),
                  pl.BlockSpec((per_chunk, D, EXPERT_HIDDEN), lambda i, c: (c, 0, 0)),
                  pl.BlockSpec((cw, D), lambda i, c: (c, 0)),
                  full(wsg), full(wsu), full(wsd), full(lg), full(lb)],
        out_specs=[row, row],
        out_shape=[jax.ShapeDtypeStruct((T, D), f32), jax.ShapeDtypeStruct((T, D), bf16)],
        scratch_shapes=[pltpu.VMEM((tm, D), f32)],
        compiler_params=_cparams(("parallel", "arbitrary")), name="moe")(
            x32, xb, gate_c, wg, wu, wd, wsg, wsu, wsd, lg, lb)


def _rope_tables(positions):
    pos = positions.astype(f32).reshape(-1, 1)
    T = pos.shape[0]
    ones, zeros = jnp.ones, jnp.zeros
    inv_p = ROPE_THETA ** (-jnp.arange(0, ROT_DIM, 2, dtype=f32) / ROT_DIM)
    ang = pos * inv_p
    c, s = jnp.cos(ang), jnp.sin(ang)
    pad = HEAD_DIM - ROT_DIM
    cos_p = jnp.tile(jnp.concatenate([c, c, ones((T, pad), f32)], axis=1), (1, 2))
    sin_p = jnp.tile(jnp.concatenate([-s, s, zeros((T, pad), f32)], axis=1), (1, 2))
    inv_m = ROPE_THETA ** (-jnp.arange(0, MLA_ROPE_DIM, 2, dtype=f32) / MLA_ROPE_DIM)
    ang = pos * inv_m
    c, s = jnp.cos(ang), jnp.sin(ang)
    tail = LANES - MLA_NOPE_DIM - MLA_ROPE_DIM
    cos_m = jnp.concatenate([ones((T, MLA_NOPE_DIM), f32), c, c, ones((T, tail), f32)], axis=1)
    sin_m = jnp.concatenate([zeros((T, MLA_NOPE_DIM), f32), -s, s, zeros((T, tail), f32)], axis=1)
    return cos_p, sin_p, cos_m, sin_m


def _pack_w_in(w):
    offs = [0]
    for sz in IN_SIZES:
        offs.append(offs[-1] + sz)
    a_q, a_k, a_v, b_cq, b_ckv, b_kr, c_q, c_k, c_v, g = [w[:, offs[i]:offs[i + 1]] for i in range(len(IN_SIZES))]
    D = w.shape[0]
    z = lambda n: jnp.zeros((D, n), w.dtype)
    kr_pad = jnp.concatenate([z(MLA_NOPE_DIM), b_kr, z(LANES - MLA_NOPE_DIM - MLA_ROPE_DIM)], axis=1)
    cols = [a_q, a_k, a_v, b_ckv, kr_pad, b_cq]
    for gi in range(DIL_GROUPS):
        cols += [c_q[:, 256 * gi:256 * (gi + 1)], c_k[:, 256 * gi:256 * (gi + 1)], c_v[:, 512 * gi:512 * (gi + 1)]]
    cols.append(g)
    return jnp.concatenate(cols, axis=1).astype(bf16)


def _pad_heads(w, n_heads, lo, hi):
    K = w.shape[0]
    per = w.shape[1] // n_heads
    wh = w.reshape(K, n_heads, per)[:, :, lo:hi]
    wh = jnp.pad(wh, ((0, 0), (0, 0), (0, LANES - (hi - lo))))
    return wh.reshape(K, n_heads * LANES).astype(bf16)


def kernel(x, positions, ln_in_g, ln_in_b, w_in, b_gate, lam_q1, lam_k1, lam_q2, lam_k2, diff_norm_g, mla_q_norm_g, mla_kv_norm_g, w_mla_qb, w_mla_kvb, w_branch_a, w_branch_b, w_branch_c, w_out, ln1_g, ln1_b, router_w, router_bias, w_exp_gate, w_exp_up, w_exp_down, w_sh_gate, w_sh_up, w_sh_down, ln2_g, ln2_b):
    B, S, D = x.shape
    T = B * S
    depth = w_in.shape[0]
    alpha = (2 * depth) ** 0.25
    cos_p, sin_p, cos_m, sin_m = _rope_tables(positions)
    x32, xb = _ln_call(x.reshape(T, D), ln_in_g.reshape(1, D), ln_in_b.reshape(1, D))
    rw_t = router_w.T.astype(bf16)
    rb = router_bias.reshape(N_EXPERTS, 1).astype(f32)
    vec = lambda v: v.reshape(1, -1).astype(f32)
    per_q = MLA_NOPE_DIM + MLA_ROPE_DIM
    per_kv = MLA_NOPE_DIM + MLA_V_DIM
    for l in range(depth):
        lam_init = 0.8 - 0.6 * math.exp(-0.3 * l)
        ha, qb, kb, vb, hc0, hc1, hc2, hg = _inproj_call(
            xb, _pack_w_in(w_in[l]), vec(b_gate[l]), cos_p, sin_p, cos_m, sin_m,
            vec(mla_q_norm_g[l]), vec(mla_kv_norm_g[l]), _pad_heads(w_mla_qb[l], MLA_HEADS, 0, per_q),
            _pad_heads(w_mla_kvb[l], MLA_HEADS, 0, MLA_NOPE_DIM), _pad_heads(w_mla_kvb[l], MLA_HEADS, MLA_NOPE_DIM, per_kv),
            B, S)
        lam_vec = jnp.pad(jnp.stack([lam_q1[l], lam_k1[l], lam_q2[l], lam_k2[l]]).astype(f32),
                          ((0, 4), (0, LANES - HEAD_DIM)))
        ha3 = ha.reshape(B, S, HA_OUT_W)
        oa = _flash_call(ha3, ha3, ha3, DIFF_HEADS, 0, DIFF_HEADS, DIFF_HEADS, n_maps=2, tq=FLASH_COLS // 2, tk=FLASH_TK,
                         name="diff_attn", lam_vec=lam_vec, norm_g=diff_norm_g[l].reshape(-1, 1).astype(f32),
                         lam_init=lam_init, vw=2 * LANES)
        ob = _flash_call(qb.reshape(B, S, -1), kb.reshape(B, S, -1), vb.reshape(B, S, -1), MLA_HEADS, 0, 0, 0,
                         n_maps=1, tq=FLASH_COLS, tk=FLASH_TK, name="mla_attn", dv=MLA_V_DIM)
        dil = [_dil_call(hc) for hc in (hc0, hc1, hc2)]
        x32, xb, gate_t = _merge_call(
            x32, oa.reshape(T, -1), ob.reshape(T, -1), [o for o, _ in dil], [lse for _, lse in dil],
            hg, w_branch_a[l].astype(bf16), w_branch_b[l].astype(bf16), w_branch_c[l].astype(bf16),
            w_out[l].astype(bf16), vec(ln1_g[l]), vec(ln1_b[l]), rw_t, rb, alpha, B, S)
        per_chunk = 4
        gate_c = gate_t.reshape(N_EXPERTS // per_chunk, per_chunk, T).transpose(0, 2, 1)
        wg = w_exp_gate[l].astype(bf16)
        wu = w_exp_up[l].astype(bf16)
        wd = w_exp_down[l].reshape(N_EXPERTS * EXPERT_HIDDEN, D).astype(bf16)
        x32, xb = _moe_call(x32, xb, gate_c, wg, wu, wd, w_sh_gate[l].astype(bf16), w_sh_up[l].astype(bf16),
                            w_sh_down[l].astype(bf16), vec(ln2_g[l]), vec(ln2_b[l]), alpha, per_chunk=per_chunk)
    return x32.reshape(B, S, D)
```

```python
import functools
import math

import jax
import jax.numpy as jnp
from jax import lax
from jax.experimental import pallas as pl
from jax.experimental.pallas import tpu as pltpu

f32 = jnp.float32
bf16 = jnp.bfloat16

D_MODEL = 1024
HEAD_DIM = 64
ROPE_THETA = 500000.0
ROT_DIM = HEAD_DIM // 4
DIFF_HEADS = 4
MLA_HEADS = 8
MLA_Q_RANK = 384
MLA_KV_RANK = 256
MLA_NOPE_DIM = 64
MLA_ROPE_DIM = 32
MLA_V_DIM = 64
DIL_PATTERNS = ((128, 1), (512, 4), (2048, 16))
DIL_GROUPS = 3
DIL_HEADS = 4
DIL_SIDE = 64
N_EXPERTS = 16
N_EXPERT_GROUPS = 4
EXPERTS_PER_GROUP = 4
EXPERT_HIDDEN = 256
IN_SIZES = (512, 512, 512, 384, 256, 32, 768, 768, 1536, 3072)
NEG_INF = -1e30
LOG2E = math.log2(math.e)

LANES = 128
VMEM_LIMIT = 52 * 1024 * 1024

HA_W = 1536
HA_OUT_W = 2048
HB_W = 768
HCG_W = 1024
HC_W = 3 * HCG_W
HG_W = 3072

FLASH_TK = 512
FLASH_COLS = 1024
FLASH_UNROLL = 8
SUM_ROWS = 16


def _cparams(sem):
    return pltpu.CompilerParams(dimension_semantics=sem, vmem_limit_bytes=VMEM_LIMIT)


def _layer_norm(z, g, b):
    mu = jnp.mean(z, axis=-1, keepdims=True)
    zc = z - mu
    var = jnp.mean(zc * zc, axis=-1, keepdims=True)
    return zc * lax.rsqrt(var + 1e-5) * g + b


def _rms_norm(z, g):
    return z * lax.rsqrt(jnp.mean(z * z, axis=-1, keepdims=True) + 1e-6) * g


def _dot(a, b):
    return jnp.dot(a, b, preferred_element_type=f32)


def _dot_nt(a, b):
    return lax.dot_general(a, b, (((1,), (1,)), ((), ())), preferred_element_type=f32)


def _rope_tile(xt, cos_t, sin_t, first_half, shift):
    xr = jnp.where(first_half, pltpu.roll(xt, LANES - shift, 1), pltpu.roll(xt, shift, 1))
    return xt * cos_t + xr * sin_t


def _ln_kernel(x_ref, g_ref, b_ref, o32_ref, ob_ref):
    y = _layer_norm(x_ref[...], g_ref[...], b_ref[...])
    o32_ref[...] = y
    ob_ref[...] = y.astype(bf16)


def _ln_call(x, g, b, tm=512):
    T, D = x.shape
    row = pl.BlockSpec((tm, D), lambda i: (i, 0))
    vec = pl.BlockSpec((1, D), lambda i: (0, 0))
    return pl.pallas_call(
        _ln_kernel, grid=(T // tm,), in_specs=[row, vec, vec], out_specs=[row, row],
        out_shape=[jax.ShapeDtypeStruct((T, D), f32), jax.ShapeDtypeStruct((T, D), bf16)],
        compiler_params=_cparams(("parallel",)), name="ln_in")(x, g, b)


def _inproj_plan():
    qscale = HEAD_DIM ** -0.5 * LOG2E
    plan = []
    for c in range(0, 512, 256):
        plan.append((c, 256, "rope", 0, c, qscale))
    for c in range(512, 1024, 256):
        plan.append((c, 256, "rope", 0, c, 1.0))
    plan.append((1024, 512, "value", 0, 1024, 1.0))
    plan.append((HA_W, HB_W, "mla", 1, 0, 1.0))
    base = HA_W + HB_W
    for g in range(DIL_GROUPS):
        o = HCG_W * g
        plan.append((base + o, 256, "rope", 2 + g, 0, qscale))
        plan.append((base + o + 256, 256, "rope", 2 + g, 256, 1.0))
        plan.append((base + o + 512, 512, "plain", 2 + g, 512, 1.0))
    base = HA_W + HB_W + HC_W
    for c in range(0, HG_W, 512):
        plan.append((base + c, 512, "gate", 2 + DIL_GROUPS, c, 1.0))
    return plan


def _mla_up_projections(acc, lane, cm_ref, sm_ref, qg_ref, kvg_ref, wq_ref, wk_ref, wv_ref, qb_ref, kb_ref, vb_ref):
    cos_t, sin_t = cm_ref[...], sm_ref[...]
    first_half = lane < MLA_NOPE_DIM + MLA_ROPE_DIM // 2
    shift = MLA_ROPE_DIM // 2
    scale = (MLA_NOPE_DIM + MLA_ROPE_DIM) ** -0.5 * LOG2E
    cn = _rms_norm(acc[:, 0:MLA_KV_RANK], kvg_ref[...]).astype(bf16)
    kr = _rope_tile(acc[:, MLA_KV_RANK:MLA_KV_RANK + LANES], cos_t, sin_t, first_half, shift)
    qn = _rms_norm(acc[:, MLA_KV_RANK + LANES:], qg_ref[...]).astype(bf16)
    q_all = _dot(qn, wq_ref[...])
    k_all = _dot(cn, wk_ref[...])
    v_all = _dot(cn, wv_ref[...])
    for h in range(MLA_HEADS):
        sl = slice(h * LANES, (h + 1) * LANES)
        q = _rope_tile(q_all[:, sl], cos_t, sin_t, first_half, shift) * scale
        qb_ref[:, sl] = q.astype(qb_ref.dtype)
        kb_ref[:, sl] = (k_all[:, sl] + kr).astype(kb_ref.dtype)
        vb_ref[:, sl] = jnp.where(lane == MLA_V_DIM, 1.0, v_all[:, sl]).astype(vb_ref.dtype)


def _inproj_kernel(x_ref, w_ref, bg_ref, cp_ref, sp_ref, cm_ref, sm_ref, qg_ref, kvg_ref, wq_ref, wk_ref, wv_ref,
                   ha_ref, qb_ref, kb_ref, vb_ref, hc0_ref, hc1_ref, hc2_ref, hg_ref, stage):
    outs = (ha_ref, None, hc0_ref, hc1_ref, hc2_ref, hg_ref)
    x = x_ref[...]
    tm = x.shape[0]
    cos_t = cp_ref[...]
    sin_t = sp_ref[...]
    lane = lax.broadcasted_iota(jnp.int32, cos_t.shape, 1)
    first_half = (lane % HEAD_DIM) < (ROT_DIM // 2)
    for (c0, w, kind, oi, o0, scale) in _inproj_plan():
        acc = _dot(x, w_ref[:, c0:c0 + w])
        if kind == "rope":
            tiles = []
            for t in range(w // LANES):
                y = _rope_tile(acc[:, t * LANES:(t + 1) * LANES], cos_t, sin_t, first_half, ROT_DIM // 2)
                tiles.append(y * scale if scale != 1.0 else y)
            acc = jnp.concatenate(tiles, axis=1)
        elif kind == "gate":
            acc = jax.nn.sigmoid(acc + bg_ref[:, o0:o0 + w])
        if kind == "mla":
            _mla_up_projections(acc, lane, cm_ref, sm_ref, qg_ref, kvg_ref, wq_ref, wk_ref, wv_ref,
                                qb_ref, kb_ref, vb_ref)
            continue
        out = outs[oi]
        if kind == "value":
            ones_tile = (lane == 0).astype(out.dtype)
            for h in range(w // LANES):
                out[:, o0 + 2 * h * LANES:o0 + (2 * h + 1) * LANES] = acc[:, h * LANES:(h + 1) * LANES].astype(out.dtype)
                out[:, o0 + (2 * h + 1) * LANES:o0 + (2 * h + 2) * LANES] = ones_tile
        elif 2 <= oi < 2 + DIL_GROUPS:
            d = DIL_PATTERNS[oi - 2][1]
            if d == 1:
                out[0, :, o0:o0 + w] = acc.astype(out.dtype)
            else:
                nt = w // LANES
                for t in range(nt):
                    stage[t] = acc[:, t * LANES:(t + 1) * LANES]
                for r in range(d):
                    rows = [stage[t, pl.ds(r, tm // d, stride=d), :] for t in range(nt)]
                    out[r, :, o0:o0 + w] = jnp.concatenate(rows, axis=1).astype(out.dtype)
        else:
            out[:, o0:o0 + w] = acc.astype(out.dtype)


def _inproj_call(xb, w, bg, cos_t, sin_t, cos_m, sin_m, qg, kvg, wq, wk, wv, B, S, tm=256):
    T, D = xb.shape
    NW = w.shape[1]
    nb = S // tm
    MW = MLA_HEADS * LANES
    row = lambda width: pl.BlockSpec((tm, width), lambda b, i: (b * nb + i, 0))
    const = lambda shape: pl.BlockSpec(shape, lambda b, i: (0, 0))
    consts = [qg, kvg, wq, wk, wv]
    hc_specs, hc_shapes = [], []
    for _, d in DIL_PATTERNS:
        hc_specs.append(pl.BlockSpec((None, d, tm // d, HCG_W), lambda b, i: (b, 0, i, 0)))
        hc_shapes.append(jax.ShapeDtypeStruct((B, d, S // d, HCG_W), bf16))
    return pl.pallas_call(
        _inproj_kernel, grid=(B, nb),
        in_specs=[row(D), const((D, NW)), const((1, HG_W)), row(LANES), row(LANES), row(LANES), row(LANES)]
        + [const(a.shape) for a in consts],
        out_specs=[row(HA_OUT_W), row(MW), row(MW), row(MW)] + hc_specs + [row(HG_W)],
        out_shape=[jax.ShapeDtypeStruct((T, HA_OUT_W), bf16)] + [jax.ShapeDtypeStruct((T, MW), bf16)] * 3 + hc_shapes
        + [jax.ShapeDtypeStruct((T, HG_W), bf16)],
        scratch_shapes=[pltpu.VMEM((4, tm, LANES), f32)],
        compiler_params=_cparams(("parallel", "parallel")), name="inproj")(
            xb, w, bg, cos_t, sin_t, cos_m, sin_m, *consts)


def _flash_kernel(*refs, n_maps, tq, tk, n_kv, lam_init, dv):
    if n_maps == 2:
        q_ref, qn_ref, k_ref, v_ref, lam_ref, g_ref, o_ref, qt_sc, m_sc, acc_sc, s_sc, mc_sc = refs
    else:
        q_ref, qn_ref, k_ref, v_ref, o_ref, qt_sc, m_sc, acc_sc, s_sc, mc_sc = refs
    i = pl.program_id(2)
    cur = i % 2
    nxt = 1 - cur

    def load_queries(ref, slot):
        q = ref[...].astype(f32)
        if n_maps == 2:
            lane = lax.broadcasted_iota(jnp.int32, q.shape, 1)
            zero = jnp.zeros_like(q)
            q = jnp.concatenate([jnp.where(lane < HEAD_DIM, q, zero), jnp.where(lane >= HEAD_DIM, q, zero)], axis=0)
        qt_sc[slot] = q.T.astype(bf16)

    def scores(j, buf, slot):
        off = pl.multiple_of(j * tk, tk)
        s = _dot(k_ref[pl.ds(off, tk), :], qt_sc[slot])
        s_sc[buf] = s
        mc_sc[buf] = jnp.max(s, axis=0, keepdims=True)

    def accumulate(j, buf):
        off = pl.multiple_of(j * tk, tk)
        m_prev = m_sc[...]
        m_new = jnp.maximum(m_prev, mc_sc[buf])
        p = jnp.exp2(s_sc[buf] - m_new)
        a = jnp.exp2(m_prev - m_new)
        pv = lax.dot_general(v_ref[pl.ds(off, tk), 0:dv + SUM_ROWS], p.astype(bf16), (((0,), (0,)), ((), ())),
                             preferred_element_type=f32)
        acc_sc[...] = a * acc_sc[...] + pv
        m_sc[...] = m_new

    @pl.when(i == 0)
    def _():
        load_queries(q_ref, cur)
        scores(0, 0, cur)

    load_queries(qn_ref, nxt)
    m_sc[...] = jnp.full(m_sc.shape, -jnp.inf, f32)
    acc_sc[...] = jnp.zeros(acc_sc.shape, f32)

    unroll = min(FLASH_UNROLL, n_kv)
    assert unroll % 2 == 0 and n_kv % unroll == 0

    def body(jj, carry):
        for u in range(unroll):
            j = unroll * jj + u
            scores(j + 1, (u + 1) % 2, cur)
            accumulate(j, u % 2)
        return carry

    lax.fori_loop(0, n_kv // unroll - 1, body, 0)
    for j in range(n_kv - unroll, n_kv):
        if j + 1 < n_kv:
            scores(j + 1, (j + 1) % 2, cur)
        else:
            scores(0, 0, nxt)
        accumulate(j, j % 2)
    o = acc_sc[0:dv, :] / acc_sc[dv:dv + 1, :]
    if n_maps == 2:
        lv = lam_ref[...]
        lam = (jnp.exp(jnp.sum(lv[0:1] * lv[1:2], axis=-1, keepdims=True))
               - jnp.exp(jnp.sum(lv[2:3] * lv[3:4], axis=-1, keepdims=True)) + lam_init)
        o = o[:, :tq] - lam * o[:, tq:]
        ms = jnp.mean(o * o, axis=0, keepdims=True)
        o = o * lax.rsqrt(ms + 1e-6) * g_ref[...] * (1.0 - lam_init)
    if dv < LANES:
        o = jnp.concatenate([o, jnp.zeros((LANES - dv, o.shape[1]), f32)], axis=0)
    o_ref[...] = o.T.astype(o_ref.dtype)


def _flash_call(q_arr, k_arr, v_arr, n_heads, q_blk0, k_blk0, v_blk0, n_maps, tq, tk, name,
                lam_vec=None, norm_g=None, lam_init=0.0, dv=LANES, vw=LANES):
    B, S, _ = q_arr.shape
    cols = n_maps * tq
    last = S // tq - 1
    assert dv + SUM_ROWS <= vw
    in_specs = [pl.BlockSpec((None, tq, LANES), lambda b, h, i: (b, i, q_blk0 + h)),
                pl.BlockSpec((None, tq, LANES), lambda b, h, i: (b, jnp.minimum(i + 1, last), q_blk0 + h)),
                pl.BlockSpec((None, S, LANES), lambda b, h, i: (b, 0, k_blk0 + h)),
                pl.BlockSpec((None, S, vw), lambda b, h, i: (b, 0, v_blk0 + h))]
    args = [q_arr, q_arr, k_arr, v_arr]
    if n_maps == 2:
        in_specs += [pl.BlockSpec((8, LANES), lambda b, h, i: (0, 0)),
                     pl.BlockSpec((LANES, 1), lambda b, h, i: (0, 0))]
        args += [lam_vec, norm_g]
    kern = functools.partial(_flash_kernel, n_maps=n_maps, tq=tq, tk=tk, n_kv=S // tk, lam_init=lam_init,
                             dv=dv)
    return pl.pallas_call(
        kern, grid=(B, n_heads, S // tq), in_specs=in_specs,
        out_specs=pl.BlockSpec((None, tq, LANES), lambda b, h, i: (b, i, h)),
        out_shape=jax.ShapeDtypeStruct((B, S, n_heads * LANES), bf16),
        scratch_shapes=[pltpu.VMEM((2, LANES, cols), bf16), pltpu.VMEM((1, cols), f32),
                        pltpu.VMEM((dv + SUM_ROWS, cols), f32), pltpu.VMEM((2, tk, cols), f32),
                        pltpu.VMEM((2, 1, cols), f32)],
        compiler_params=_cparams(("parallel", "parallel", "arbitrary")), name=name)(*args)


def _dil_kernel(q_ref, kp_ref, kc_ref, kn_ref, vp_ref, vc_ref, vn_ref, o_ref, lse_ref, kcat, vcat, *, tq, sub_len):
    i = pl.program_id(2)
    side = DIL_SIDE
    kcat[0:side, :] = kp_ref[...]
    kcat[side:side + tq, :] = kc_ref[...]
    kcat[side + tq:, :] = kn_ref[...]
    vcat[0:side, :] = vp_ref[...]
    vcat[side:side + tq, :] = vc_ref[...]
    vcat[side + tq:, :] = vn_ref[...]
    qs, ks = 2 * side, 4 * side
    lane = lax.broadcasted_iota(jnp.int32, (qs, LANES), 1)
    ii = lax.broadcasted_iota(jnp.int32, (qs, ks), 0)
    jj = lax.broadcasted_iota(jnp.int32, (qs, ks), 1)
    band = (jj - ii >= 0) & (jj - ii <= 2 * side)
    for sb in range(tq // qs):
        q0 = sb * qs
        pos = jj + (i * tq + q0 - side)
        mask = band & (pos >= 0) & (pos < sub_len)
        lse_tile = jnp.zeros((qs, LANES), f32)
        for h in range(DIL_HEADS):
            t = h // 2
            qt = q_ref[q0:q0 + qs, t * LANES:(t + 1) * LANES]
            keep = (lane < HEAD_DIM) if h % 2 == 0 else (lane >= HEAD_DIM)
            qm = jnp.where(keep, qt, jnp.zeros_like(qt))
            s = _dot_nt(qm, kcat[q0:q0 + ks, t * LANES:(t + 1) * LANES])
            s = jnp.where(mask, s, NEG_INF)
            m = jnp.max(s, axis=-1, keepdims=True)
            p = jnp.exp2(s - m)
            l = jnp.sum(p, axis=-1, keepdims=True)
            o = _dot(p.astype(bf16), vcat[q0:q0 + ks, h * LANES:(h + 1) * LANES]) / l
            o_ref[q0:q0 + qs, h * LANES:(h + 1) * LANES] = o.astype(o_ref.dtype)
            lse_tile = jnp.where(lane == h, m + jnp.log2(l), lse_tile)
        lse_ref[q0:q0 + qs, :] = lse_tile


def _dil_call(hc):
    B, d, L, _ = hc.shape
    tq = min(512, L)
    side = DIL_SIDE
    nblk = tq // side
    last = L // side - 1
    qw, vw = 2 * LANES, 4 * LANES
    prev = lambda i: jnp.maximum(i * nblk - 1, 0)
    nxt = lambda i: jnp.minimum((i + 1) * nblk, last)
    in_specs = [
        pl.BlockSpec((None, None, tq, qw), lambda b, r, i: (b, r, i, 0)),
        pl.BlockSpec((None, None, side, qw), lambda b, r, i: (b, r, prev(i), 1)),
        pl.BlockSpec((None, None, tq, qw), lambda b, r, i: (b, r, i, 1)),
        pl.BlockSpec((None, None, side, qw), lambda b, r, i: (b, r, nxt(i), 1)),
        pl.BlockSpec((None, None, side, vw), lambda b, r, i: (b, r, prev(i), 1)),
        pl.BlockSpec((None, None, tq, vw), lambda b, r, i: (b, r, i, 1)),
        pl.BlockSpec((None, None, side, vw), lambda b, r, i: (b, r, nxt(i), 1)),
    ]
    return pl.pallas_call(
        functools.partial(_dil_kernel, tq=tq, sub_len=L), grid=(B, d, L // tq), in_specs=in_specs,
        out_specs=[pl.BlockSpec((None, None, tq, vw), lambda b, r, i: (b, r, i, 0)),
                   pl.BlockSpec((None, None, tq, LANES), lambda b, r, i: (b, r, i, 0))],
        out_shape=[jax.ShapeDtypeStruct((B, d, L, vw), bf16), jax.ShapeDtypeStruct((B, d, L, LANES), f32)],
        scratch_shapes=[pltpu.VMEM((tq + 2 * side, qw), bf16), pltpu.VMEM((tq + 2 * side, vw), bf16)],
        compiler_params=_cparams(("parallel", "parallel", "arbitrary")),
        name=f"dilated_d{d}")(hc, hc, hc, hc, hc, hc, hc)


def _token_order(ref, stage):
    d, n, w = ref.shape
    if d == 1:
        return ref[0].astype(f32)
    nt = w // LANES
    for r in range(d):
        v = ref[r].astype(f32)
        for t in range(nt):
            stage[t, pl.ds(r, n, stride=d), :] = v[:, t * LANES:(t + 1) * LANES]
    return jnp.concatenate([stage[t] for t in range(nt)], axis=1)


def _merge_kernel(x_ref, oa_ref, ob_ref, o1_ref, o2_ref, o3_ref, l1_ref, l2_ref, l3_ref, g_ref,
                  wa_ref, wb_ref, wc_ref, wo_ref, lg_ref, lb_ref, rw_ref, rb_ref, x32_ref, xb_ref, gate_ref,
                  so2, so3, sl2, sl3, *, alpha):
    l1, l2, l3 = _token_order(l1_ref, None), _token_order(l2_ref, sl2), _token_order(l3_ref, sl3)
    o1, o2, o3 = _token_order(o1_ref, None), _token_order(o2_ref, so2), _token_order(o3_ref, so3)
    mx = jnp.maximum(jnp.maximum(l1, l2), l3)
    e1, e2, e3 = jnp.exp2(l1 - mx), jnp.exp2(l2 - mx), jnp.exp2(l3 - mx)
    den = e1 + e2 + e3
    w1, w2, w3 = e1 / den, e2 / den, e3 / den
    tiles = []
    for h in range(DIL_HEADS):
        sl = slice(h * LANES, (h + 1) * LANES)
        oc = w1[:, h:h + 1] * o1[:, sl] + w2[:, h:h + 1] * o2[:, sl] + w3[:, h:h + 1] * o3[:, sl]
        tiles.append(oc.astype(bf16))
    oc = jnp.concatenate(tiles, axis=1)
    obf = ob_ref[...].astype(f32)
    half = LANES // 2
    ob = jnp.concatenate([obf[:, 2 * j * LANES:(2 * j + 1) * LANES]
                          + pltpu.roll(obf[:, (2 * j + 1) * LANES:(2 * j + 2) * LANES], half, 1)
                          for j in range(MLA_HEADS // 2)], axis=1).astype(bf16)
    D = D_MODEL
    y = (g_ref[:, 0:D].astype(f32) * _dot(oa_ref[...], wa_ref[...])
         + g_ref[:, D:2 * D].astype(f32) * _dot(ob, wb_ref[...])
         + g_ref[:, 2 * D:3 * D].astype(f32) * _dot(oc, wc_ref[...]))
    mix = _dot(y.astype(bf16), wo_ref[...])
    out = _layer_norm(alpha * x_ref[...] + mix, lg_ref[...], lb_ref[...])
    x32_ref[...] = out
    xb = out.astype(bf16)
    xb_ref[...] = xb
    gate_ref[...] = _route(_dot_nt(rw_ref[...], xb), rb_ref[...])


def _merge_call(x32, oa, ob, dil_o, dil_l, hg, wa, wb, wc, wo, lg, lb, rw_t, rb, alpha, B, S, tm=512):
    T, D = x32.shape
    nb = S // tm
    row = lambda a: pl.BlockSpec((tm, a.shape[1]), lambda b, i: (b * nb + i, 0))
    cls = lambda a: pl.BlockSpec((None, a.shape[1], tm // a.shape[1], a.shape[3]), lambda b, i: (b, 0, i, 0))
    full = lambda a: pl.BlockSpec(a.shape, lambda b, i: (0, 0))
    consts = [wa, wb, wc, wo, lg, lb, rw_t, rb]
    ow, lw = dil_o[0].shape[3], dil_l[0].shape[3]
    return pl.pallas_call(
        functools.partial(_merge_kernel, alpha=alpha), grid=(B, nb),
        in_specs=[row(x32), row(oa), row(ob)] + [cls(a) for a in dil_o] + [cls(a) for a in dil_l] + [row(hg)]
        + [full(a) for a in consts],
        out_specs=[pl.BlockSpec((tm, D), lambda b, i: (b * nb + i, 0))] * 2
        + [pl.BlockSpec((N_EXPERTS, tm), lambda b, i: (0, b * nb + i))],
        out_shape=[jax.ShapeDtypeStruct((T, D), f32), jax.ShapeDtypeStruct((T, D), bf16),
                   jax.ShapeDtypeStruct((N_EXPERTS, T), f32)],
        scratch_shapes=[pltpu.VMEM((ow // LANES, tm, LANES), f32), pltpu.VMEM((ow // LANES, tm, LANES), f32),
                        pltpu.VMEM((lw // LANES, tm, LANES), f32), pltpu.VMEM((lw // LANES, tm, LANES), f32)],
        compiler_params=_cparams(("parallel", "parallel")), name="merge")(
            x32, oa, ob, *dil_o, *dil_l, hg, *consts)


def _first_true(flags):
    out, seen = [], None
    for f in flags:
        out.append(f if seen is None else f & ~seen)
        seen = f if seen is None else seen | f
    return out


def _route(logits, bias):
    scores = jax.nn.sigmoid(logits)
    biased = scores + bias
    sc = [scores[e:e + 1, :] for e in range(N_EXPERTS)]
    bi = [biased[e:e + 1, :] for e in range(N_EXPERTS)]
    npg = EXPERTS_PER_GROUP
    gscore = []
    for g in range(N_EXPERT_GROUPS):
        v = bi[g * npg:(g + 1) * npg]
        best = None
        for a in range(npg):
            for b in range(a + 1, npg):
                best = v[a] + v[b] if best is None else jnp.maximum(best, v[a] + v[b])
        gscore.append(best)
    gbest = functools.reduce(jnp.maximum, gscore)
    gsel = _first_true([gs == gbest for gs in gscore])
    zero = jnp.zeros_like(gbest)
    bsel = [functools.reduce(lambda x, y: x + y, [jnp.where(gsel[g], bi[g * npg + j], zero)
                                                  for g in range(N_EXPERT_GROUPS)]) for j in range(npg)]
    ssel = [functools.reduce(lambda x, y: x + y, [jnp.where(gsel[g], sc[g * npg + j], zero)
                                                  for g in range(N_EXPERT_GROUPS)]) for j in range(npg)]
    t1 = functools.reduce(jnp.maximum, bsel)
    i1 = _first_true([b == t1 for b in bsel])
    rest = [jnp.where(i1[j], -jnp.inf, bsel[j]) for j in range(npg)]
    t2 = functools.reduce(jnp.maximum, rest)
    i2 = _first_true([rest[j] == t2 for j in range(npg)])
    w1 = functools.reduce(lambda x, y: x + y, [jnp.where(i1[j], ssel[j], zero) for j in range(npg)])
    w2 = functools.reduce(lambda x, y: x + y, [jnp.where(i2[j], ssel[j], zero) for j in range(npg)])
    den = w1 + w2
    local = [jnp.where(i1[j], w1 / den, zero) + jnp.where(i2[j], w2 / den, zero) for j in range(npg)]
    rows = [jnp.where(gsel[e // npg], local[e % npg], zero) for e in range(N_EXPERTS)]
    return jnp.concatenate(rows, axis=0)


def _moe_kernel(x32_ref, xb_ref, gate_ref, wg_ref, wu_ref, wd_ref, wsg_ref, wsu_ref, wsd_ref, lg_ref, lb_ref,
                o32_ref, ob_ref, acc_sc, *, alpha, n_chunks, per_chunk):
    c = pl.program_id(1)
    x = xb_ref[...]

    @pl.when(c == 0)
    def _():
        hs = jax.nn.silu(_dot(x, wsg_ref[...])) * _dot(x, wsu_ref[...])
        acc_sc[...] = _dot(hs.astype(bf16), wsd_ref[...])

    gate = gate_ref[...]
    hid = []
    for e in range(per_chunk):
        h = jax.nn.silu(_dot(x, wg_ref[e])) * _dot(x, wu_ref[e]) * gate[:, e:e + 1]
        hid.append(h.astype(bf16))
    acc_sc[...] += _dot(jnp.concatenate(hid, axis=1), wd_ref[...])

    @pl.when(c == n_chunks - 1)
    def _():
        out = _layer_norm(alpha * x32_ref[...] + acc_sc[...], lg_ref[...], lb_ref[...])
        o32_ref[...] = out
        ob_ref[...] = out.astype(bf16)


def _moe_call(x32, xb, gate_c, wg, wu, wd, wsg, wsu, wsd, lg, lb, alpha, tm=1024, per_chunk=4):
    T, D = x32.shape
    n_chunks = N_EXPERTS // per_chunk
    cw = per_chunk * EXPERT_HIDDEN
    row = pl.BlockSpec((tm, D), lambda i, c: (i, 0))
    full = lambda a: pl.BlockSpec(a.shape, lambda i, c: (0, 0))
    return pl.pallas_call(
        functools.partial(_moe_kernel, alpha=alpha, n_chunks=n_chunks, per_chunk=per_chunk),
        grid=(T // tm, n_chunks),
        in_specs=[row, row, pl.BlockSpec((None, tm, per_chunk), lambda i, c: (c, i, 0)),
                  pl.BlockSpec((per_chunk, D, EXPERT_HIDDEN), lambda i, c: (c, 0, 0)),
                  pl.BlockSpec((per_chunk, D, EXPERT_HIDDEN), lambda i, c: (c, 0, 0)),
                  pl.BlockSpec((cw, D), lambda i, c: (c, 0)),
                  full(wsg), full(wsu), full(wsd), full(lg), full(lb)],
        out_specs=[row, row],
        out_shape=[jax.ShapeDtypeStruct((T, D), f32), jax.ShapeDtypeStruct((T, D), bf16)],
        scratch_shapes=[pltpu.VMEM((tm, D), f32)],
        compiler_params=_cparams(("parallel", "arbitrary")), name="moe")(
            x32, xb, gate_c, wg, wu, wd, wsg, wsu, wsd, lg, lb)


def _rope_tables(positions):
    pos = positions.astype(f32).reshape(-1, 1)
    T = pos.shape[0]
    ones, zeros = jnp.ones, jnp.zeros
    inv_p = ROPE_THETA ** (-jnp.arange(0, ROT_DIM, 2, dtype=f32) / ROT_DIM)
    ang = pos * inv_p
    c, s = jnp.cos(ang), jnp.sin(ang)
    pad = HEAD_DIM - ROT_DIM
    cos_p = jnp.tile(jnp.concatenate([c, c, ones((T, pad), f32)], axis=1), (1, 2))
    sin_p = jnp.tile(jnp.concatenate([-s, s, zeros((T, pad), f32)], axis=1), (1, 2))
    inv_m = ROPE_THETA ** (-jnp.arange(0, MLA_ROPE_DIM, 2, dtype=f32) / MLA_ROPE_DIM)
    ang = pos * inv_m
    c, s = jnp.cos(ang), jnp.sin(ang)
    tail = LANES - MLA_NOPE_DIM - MLA_ROPE_DIM
    cos_m = jnp.concatenate([ones((T, MLA_NOPE_DIM), f32), c, c, ones((T, tail), f32)], axis=1)
    sin_m = jnp.concatenate([zeros((T, MLA_NOPE_DIM), f32), -s, s, zeros((T, tail), f32)], axis=1)
    return cos_p, sin_p, cos_m, sin_m


def _pack_w_in(w):
    offs = [0]
    for sz in IN_SIZES:
        offs.append(offs[-1] + sz)
    a_q, a_k, a_v, b_cq, b_ckv, b_kr, c_q, c_k, c_v, g = [w[:, offs[i]:offs[i + 1]] for i in range(len(IN_SIZES))]
    D = w.shape[0]
    z = lambda n: jnp.zeros((D, n), w.dtype)
    kr_pad = jnp.concatenate([z(MLA_NOPE_DIM), b_kr, z(LANES - MLA_NOPE_DIM - MLA_ROPE_DIM)], axis=1)
    cols = [a_q, a_k, a_v, b_ckv, kr_pad, b_cq]
    for gi in range(DIL_GROUPS):
        cols += [c_q[:, 256 * gi:256 * (gi + 1)], c_k[:, 256 * gi:256 * (gi + 1)], c_v[:, 512 * gi:512 * (gi + 1)]]
    cols.append(g)
    return jnp.concatenate(cols, axis=1).astype(bf16)


def _pad_heads(w, n_heads, lo, hi):
    K = w.shape[0]
    per = w.shape[1] // n_heads
    wh = w.reshape(K, n_heads, per)[:, :, lo:hi]
    wh = jnp.pad(wh, ((0, 0), (0, 0), (0, LANES - (hi - lo))))
    return wh.reshape(K, n_heads * LANES).astype(bf16)


def kernel(x, positions, ln_in_g, ln_in_b, w_in, b_gate, lam_q1, lam_k1, lam_q2, lam_k2, diff_norm_g, mla_q_norm_g, mla_kv_norm_g, w_mla_qb, w_mla_kvb, w_branch_a, w_branch_b, w_branch_c, w_out, ln1_g, ln1_b, router_w, router_bias, w_exp_gate, w_exp_up, w_exp_down, w_sh_gate, w_sh_up, w_sh_down, ln2_g, ln2_b):
    B, S, D = x.shape
    T = B * S
    depth = w_in.shape[0]
    assert D == D_MODEL and S % (DIL_PATTERNS[-1][1] * 2 * DIL_SIDE) == 0 and S % FLASH_COLS == 0 and S % FLASH_TK == 0
    alpha = (2 * depth) ** 0.25
    cos_p, sin_p, cos_m, sin_m = _rope_tables(positions)
    x32, xb = _ln_call(x.reshape(T, D), ln_in_g.reshape(1, D), ln_in_b.reshape(1, D))
    rw_t = router_w.T.astype(bf16)
    rb = router_bias.reshape(N_EXPERTS, 1).astype(f32)
    vec = lambda v: v.reshape(1, -1).astype(f32)
    per_q = MLA_NOPE_DIM + MLA_ROPE_DIM
    per_kv = MLA_NOPE_DIM + MLA_V_DIM
    for l in range(depth):
        lam_init = 0.8 - 0.6 * math.exp(-0.3 * l)
        ha, qb, kb, vb, hc0, hc1, hc2, hg = _inproj_call(
            xb, _pack_w_in(w_in[l]), vec(b_gate[l]), cos_p, sin_p, cos_m, sin_m,
            vec(mla_q_norm_g[l]), vec(mla_kv_norm_g[l]), _pad_heads(w_mla_qb[l], MLA_HEADS, 0, per_q),
            _pad_heads(w_mla_kvb[l], MLA_HEADS, 0, MLA_NOPE_DIM), _pad_heads(w_mla_kvb[l], MLA_HEADS, MLA_NOPE_DIM, per_kv),
            B, S)
        lam_vec = jnp.pad(jnp.stack([lam_q1[l], lam_k1[l], lam_q2[l], lam_k2[l]]).astype(f32),
                          ((0, 4), (0, LANES - HEAD_DIM)))
        ha3 = ha.reshape(B, S, HA_OUT_W)
        oa = _flash_call(ha3, ha3, ha3, DIFF_HEADS, 0, DIFF_HEADS, DIFF_HEADS, n_maps=2, tq=FLASH_COLS // 2, tk=FLASH_TK,
                         name="diff_attn", lam_vec=lam_vec, norm_g=diff_norm_g[l].reshape(-1, 1).astype(f32),
                         lam_init=lam_init, vw=2 * LANES)
        ob = _flash_call(qb.reshape(B, S, -1), kb.reshape(B, S, -1), vb.reshape(B, S, -1), MLA_HEADS, 0, 0, 0,
                         n_maps=1, tq=FLASH_COLS, tk=FLASH_TK, name="mla_attn", dv=MLA_V_DIM)
        dil = [_dil_call(hc) for hc in (hc0, hc1, hc2)]
        x32, xb, gate_t = _merge_call(
            x32, oa.reshape(T, -1), ob.reshape(T, -1), [o for o, _ in dil], [lse for _, lse in dil],
            hg, w_branch_a[l].astype(bf16), w_branch_b[l].astype(bf16), w_branch_c[l].astype(bf16),
            w_out[l].astype(bf16), vec(ln1_g[l]), vec(ln1_b[l]), rw_t, rb, alpha, B, S)
        per_chunk = 4
        gate_c = gate_t.reshape(N_EXPERTS // per_chunk, per_chunk, T).transpose(0, 2, 1)
        wg = w_exp_gate[l].astype(bf16)
        wu = w_exp_up[l].astype(bf16)
        wd = w_exp_down[l].reshape(N_EXPERTS * EXPERT_HIDDEN, D).astype(bf16)
        x32, xb = _moe_call(x32, xb, gate_c, wg, wu, wd, w_sh_gate[l].astype(bf16), w_sh_up[l].astype(bf16),
                            w_sh_down[l].astype(bf16), vec(ln2_g[l]), vec(ln2_b[l]), alpha, per_chunk=per_chunk)
    return x32.reshape(B, S, D)
```

```python
import functools
import math

import jax
import jax.numpy as jnp
from jax import lax
from jax.experimental import pallas as pl
from jax.experimental.pallas import tpu as pltpu

f32 = jnp.float32
bf16 = jnp.bfloat16

D_MODEL = 1024
HEAD_DIM = 64
ROPE_THETA = 500000.0
ROT_DIM = HEAD_DIM // 4
DIFF_HEADS = 4
MLA_HEADS = 8
MLA_Q_RANK = 384
MLA_KV_RANK = 256
MLA_NOPE_DIM = 64
MLA_ROPE_DIM = 32
MLA_V_DIM = 64
DIL_PATTERNS = ((128, 1), (512, 4), (2048, 16))
DIL_GROUPS = 3
DIL_HEADS = 4
DIL_SIDE = 64
N_EXPERTS = 16
N_EXPERT_GROUPS = 4
EXPERTS_PER_GROUP = 4
EXPERT_HIDDEN = 256
IN_SIZES = (512, 512, 512, 384, 256, 32, 768, 768, 1536, 3072)
NEG_INF = -1e30
LOG2E = math.log2(math.e)

LANES = 128
VMEM_LIMIT = 52 * 1024 * 1024

HA_W = 1536
HA_OUT_W = 2048
HB_W = 768
HCG_W = 1024
HC_W = 3 * HCG_W
HG_W = 3072

FLASH_TK = 512
FLASH_COLS = 1024
FLASH_UNROLL = 8
SUM_ROWS = 16


def _cparams(sem):
    return pltpu.CompilerParams(dimension_semantics=sem, vmem_limit_bytes=VMEM_LIMIT)


def _layer_norm(z, g, b):
    mu = jnp.mean(z, axis=-1, keepdims=True)
    zc = z - mu
    var = jnp.mean(zc * zc, axis=-1, keepdims=True)
    return zc * lax.rsqrt(var + 1e-5) * g + b


def _rms_norm(z, g):
    return z * lax.rsqrt(jnp.mean(z * z, axis=-1, keepdims=True) + 1e-6) * g


def _dot(a, b):
    return jnp.dot(a, b, preferred_element_type=f32)


def _dot_nt(a, b):
    return lax.dot_general(a, b, (((1,), (1,)), ((), ())), preferred_element_type=f32)


def _rope_tile(xt, cos_t, sin_t, first_half, shift):
    xr = jnp.where(first_half, pltpu.roll(xt, LANES - shift, 1), pltpu.roll(xt, shift, 1))
    return xt * cos_t + xr * sin_t


def _inproj_plan():
    qscale = HEAD_DIM ** -0.5 * LOG2E
    plan = []
    for c in range(0, 512, 256):
        plan.append((c, 256, "rope", 0, c, qscale))
    for c in range(512, 1024, 256):
        plan.append((c, 256, "rope", 0, c, 1.0))
    plan.append((1024, 512, "value", 0, 1024, 1.0))
    plan.append((HA_W, HB_W, "mla", 1, 0, 1.0))
    base = HA_W + HB_W
    for g in range(DIL_GROUPS):
        o = HCG_W * g
        plan.append((base + o, 256, "rope", 2 + g, 0, qscale))
        plan.append((base + o + 256, 256, "rope", 2 + g, 256, 1.0))
        plan.append((base + o + 512, 512, "plain", 2 + g, 512, 1.0))
    base = HA_W + HB_W + HC_W
    for c in range(0, HG_W, 512):
        plan.append((base + c, 512, "gate", 2 + DIL_GROUPS, c, 1.0))
    return plan


def _mla_up_projections(acc, lane, cm_ref, sm_ref, qg_ref, kvg_ref, wq_ref, wk_ref, wv_ref, qb_ref, kb_ref, vb_ref):
    cos_t, sin_t = cm_ref[...], sm_ref[...]
    first_half = lane < MLA_NOPE_DIM + MLA_ROPE_DIM // 2
    shift = MLA_ROPE_DIM // 2
    scale = (MLA_NOPE_DIM + MLA_ROPE_DIM) ** -0.5 * LOG2E
    cn = _rms_norm(acc[:, 0:MLA_KV_RANK], kvg_ref[...]).astype(bf16)
    kr = _rope_tile(acc[:, MLA_KV_RANK:MLA_KV_RANK + LANES], cos_t, sin_t, first_half, shift)
    qn = _rms_norm(acc[:, MLA_KV_RANK + LANES:], qg_ref[...]).astype(bf16)
    q_all = _dot(qn, wq_ref[...])
    k_all = _dot(cn, wk_ref[...])
    v_all = _dot(cn, wv_ref[...])
    for h in range(MLA_HEADS):
        sl = slice(h * LANES, (h + 1) * LANES)
        q = _rope_tile(q_all[:, sl], cos_t, sin_t, first_half, shift) * scale
        qb_ref[:, sl] = q.astype(qb_ref.dtype)
        kb_ref[:, sl] = (k_all[:, sl] + kr).astype(kb_ref.dtype)
        vb_ref[:, sl] = jnp.where(lane == MLA_V_DIM, 1.0, v_all[:, sl]).astype(vb_ref.dtype)


def _inproj_kernel(x_ref, w_ref, bg_ref, cp_ref, sp_ref, cm_ref, sm_ref, qg_ref, kvg_ref, wq_ref, wk_ref, wv_ref,
                   *rest, ln_first):
    if ln_first:
        lng_ref, lnb_ref, ha_ref, qb_ref, kb_ref, vb_ref, hc0_ref, hc1_ref, hc2_ref, hg_ref, x32_ref, xb_ref, stage = rest
        y = _layer_norm(x_ref[...], lng_ref[...], lnb_ref[...])
        x32_ref[...] = y
        x = y.astype(bf16)
        xb_ref[...] = x
    else:
        ha_ref, qb_ref, kb_ref, vb_ref, hc0_ref, hc1_ref, hc2_ref, hg_ref, stage = rest
        x = x_ref[...]
    outs = (ha_ref, None, hc0_ref, hc1_ref, hc2_ref, hg_ref)
    tm = x.shape[0]
    cos_t = cp_ref[...]
    sin_t = sp_ref[...]
    lane = lax.broadcasted_iota(jnp.int32, cos_t.shape, 1)
    first_half = (lane % HEAD_DIM) < (ROT_DIM // 2)
    for (c0, w, kind, oi, o0, scale) in _inproj_plan():
        acc = _dot(x, w_ref[:, c0:c0 + w])
        if kind == "rope":
            tiles = []
            for t in range(w // LANES):
                y = _rope_tile(acc[:, t * LANES:(t + 1) * LANES], cos_t, sin_t, first_half, ROT_DIM // 2)
                tiles.append(y * scale if scale != 1.0 else y)
            acc = jnp.concatenate(tiles, axis=1)
        elif kind == "gate":
            acc = jax.nn.sigmoid(acc + bg_ref[:, o0:o0 + w])
        if kind == "mla":
            _mla_up_projections(acc, lane, cm_ref, sm_ref, qg_ref, kvg_ref, wq_ref, wk_ref, wv_ref,
                                qb_ref, kb_ref, vb_ref)
            continue
        out = outs[oi]
        if kind == "value":
            ones_tile = (lane == 0).astype(out.dtype)
            for h in range(w // LANES):
                out[:, o0 + 2 * h * LANES:o0 + (2 * h + 1) * LANES] = acc[:, h * LANES:(h + 1) * LANES].astype(out.dtype)
                out[:, o0 + (2 * h + 1) * LANES:o0 + (2 * h + 2) * LANES] = ones_tile
        elif 2 <= oi < 2 + DIL_GROUPS:
            d = DIL_PATTERNS[oi - 2][1]
            if d == 1:
                out[0, :, o0:o0 + w] = acc.astype(out.dtype)
            else:
                nt = w // LANES
                for t in range(nt):
                    stage[t] = acc[:, t * LANES:(t + 1) * LANES]
                for r in range(d):
                    rows = [stage[t, pl.ds(r, tm // d, stride=d), :] for t in range(nt)]
                    out[r, :, o0:o0 + w] = jnp.concatenate(rows, axis=1).astype(out.dtype)
        else:
            out[:, o0:o0 + w] = acc.astype(out.dtype)


def _inproj_call(x, w, bg, cos_t, sin_t, cos_m, sin_m, qg, kvg, wq, wk, wv, B, S, ln=None, tm=256):
    T, D = x.shape
    NW = w.shape[1]
    nb = S // tm
    MW = MLA_HEADS * LANES
    row = lambda width: pl.BlockSpec((tm, width), lambda b, i: (b * nb + i, 0))
    const = lambda shape: pl.BlockSpec(shape, lambda b, i: (0, 0))
    consts = [qg, kvg, wq, wk, wv] + (list(ln) if ln else [])
    extra_specs = [row(D), row(D)] if ln else []
    extra_shapes = [jax.ShapeDtypeStruct((T, D), f32), jax.ShapeDtypeStruct((T, D), bf16)] if ln else []
    hc_specs, hc_shapes = [], []
    for _, d in DIL_PATTERNS:
        hc_specs.append(pl.BlockSpec((None, d, tm // d, HCG_W), lambda b, i: (b, 0, i, 0)))
        hc_shapes.append(jax.ShapeDtypeStruct((B, d, S // d, HCG_W), bf16))
    return pl.pallas_call(
        functools.partial(_inproj_kernel, ln_first=bool(ln)), grid=(B, nb),
        in_specs=[row(D), const((D, NW)), const((1, HG_W)), row(LANES), row(LANES), row(LANES), row(LANES)]
        + [const(a.shape) for a in consts],
        out_specs=[row(HA_OUT_W), row(MW), row(MW), row(MW)] + hc_specs + [row(HG_W)] + extra_specs,
        out_shape=[jax.ShapeDtypeStruct((T, HA_OUT_W), bf16)] + [jax.ShapeDtypeStruct((T, MW), bf16)] * 3 + hc_shapes
        + [jax.ShapeDtypeStruct((T, HG_W), bf16)] + extra_shapes,
        scratch_shapes=[pltpu.VMEM((4, tm, LANES), f32)],
        compiler_params=_cparams(("parallel", "parallel")), name="inproj")(
            x, w, bg, cos_t, sin_t, cos_m, sin_m, *consts)


def _flash_kernel(*refs, n_maps, tq, tk, n_kv, lam_init, dv):
    if n_maps == 2:
        q_ref, qn_ref, k_ref, v_ref, lam_ref, g_ref, o_ref, qt_sc, m_sc, acc_sc, s_sc, mc_sc = refs
    else:
        q_ref, qn_ref, k_ref, v_ref, o_ref, qt_sc, m_sc, acc_sc, s_sc, mc_sc = refs
    i = pl.program_id(2)
    cur = i % 2
    nxt = 1 - cur

    def load_queries(ref, slot):
        q = ref[...].astype(f32)
        if n_maps == 2:
            lane = lax.broadcasted_iota(jnp.int32, q.shape, 1)
            zero = jnp.zeros_like(q)
            q = jnp.concatenate([jnp.where(lane < HEAD_DIM, q, zero), jnp.where(lane >= HEAD_DIM, q, zero)], axis=0)
        qt_sc[slot] = q.T.astype(bf16)

    def scores(j, buf, slot):
        off = pl.multiple_of(j * tk, tk)
        s = _dot(k_ref[pl.ds(off, tk), :], qt_sc[slot])
        s_sc[buf] = s
        mc_sc[buf] = jnp.max(s, axis=0, keepdims=True)

    def accumulate(j, buf):
        off = pl.multiple_of(j * tk, tk)
        m_prev = m_sc[...]
        m_new = jnp.maximum(m_prev, mc_sc[buf])
        p = jnp.exp2(s_sc[buf] - m_new)
        a = jnp.exp2(m_prev - m_new)
        pv = lax.dot_general(v_ref[pl.ds(off, tk), 0:dv + SUM_ROWS], p.astype(bf16), (((0,), (0,)), ((), ())),
                             preferred_element_type=f32)
        acc_sc[...] = a * acc_sc[...] + pv
        m_sc[...] = m_new

    @pl.when(i == 0)
    def _():
        load_queries(q_ref, cur)
        scores(0, 0, cur)

    load_queries(qn_ref, nxt)
    m_sc[...] = jnp.full(m_sc.shape, -jnp.inf, f32)
    acc_sc[...] = jnp.zeros(acc_sc.shape, f32)

    unroll = min(FLASH_UNROLL, n_kv)
    assert unroll % 2 == 0 and n_kv % unroll == 0

    def body(jj, carry):
        for u in range(unroll):
            j = unroll * jj + u
            scores(j + 1, (u + 1) % 2, cur)
            accumulate(j, u % 2)
        return carry

    lax.fori_loop(0, n_kv // unroll - 1, body, 0)
    for j in range(n_kv - unroll, n_kv):
        if j + 1 < n_kv:
            scores(j + 1, (j + 1) % 2, cur)
        else:
            scores(0, 0, nxt)
        accumulate(j, j % 2)
    o = acc_sc[0:dv, :] / acc_sc[dv:dv + 1, :]
    if n_maps == 2:
        lv = lam_ref[...]
        lam = (jnp.exp(jnp.sum(lv[0:1] * lv[1:2], axis=-1, keepdims=True))
               - jnp.exp(jnp.sum(lv[2:3] * lv[3:4], axis=-1, keepdims=True)) + lam_init)
        o = o[:, :tq] - lam * o[:, tq:]
        ms = jnp.mean(o * o, axis=0, keepdims=True)
        o = o * lax.rsqrt(ms + 1e-6) * g_ref[...] * (1.0 - lam_init)
    if dv < LANES:
        o = jnp.concatenate([o, jnp.zeros((LANES - dv, o.shape[1]), f32)], axis=0)
    o_ref[...] = o.T.astype(o_ref.dtype)


def _flash_call(q_arr, k_arr, v_arr, n_heads, q_blk0, k_blk0, v_blk0, n_maps, tq, tk, name,
                lam_vec=None, norm_g=None, lam_init=0.0, dv=LANES, vw=LANES):
    B, S, _ = q_arr.shape
    cols = n_maps * tq
    last = S // tq - 1
    assert dv + SUM_ROWS <= vw
    in_specs = [pl.BlockSpec((None, tq, LANES), lambda b, h, i: (b, i, q_blk0 + h)),
                pl.BlockSpec((None, tq, LANES), lambda b, h, i: (b, jnp.minimum(i + 1, last), q_blk0 + h)),
                pl.BlockSpec((None, S, LANES), lambda b, h, i: (b, 0, k_blk0 + h)),
                pl.BlockSpec((None, S, vw), lambda b, h, i: (b, 0, v_blk0 + h))]
    args = [q_arr, q_arr, k_arr, v_arr]
    if n_maps == 2:
        in_specs += [pl.BlockSpec((8, LANES), lambda b, h, i: (0, 0)),
                     pl.BlockSpec((LANES, 1), lambda b, h, i: (0, 0))]
        args += [lam_vec, norm_g]
    kern = functools.partial(_flash_kernel, n_maps=n_maps, tq=tq, tk=tk, n_kv=S // tk, lam_init=lam_init,
                             dv=dv)
    return pl.pallas_call(
        kern, grid=(B, n_heads, S // tq), in_specs=in_specs,
        out_specs=pl.BlockSpec((None, tq, LANES), lambda b, h, i: (b, i, h)),
        out_shape=jax.ShapeDtypeStruct((B, S, n_heads * LANES), bf16),
        scratch_shapes=[pltpu.VMEM((2, LANES, cols), bf16), pltpu.VMEM((1, cols), f32),
                        pltpu.VMEM((dv + SUM_ROWS, cols), f32), pltpu.VMEM((2, tk, cols), f32),
                        pltpu.VMEM((2, 1, cols), f32)],
        compiler_params=_cparams(("parallel", "parallel", "arbitrary")), name=name)(*args)


def _dil_kernel(q_ref, kp_ref, kc_ref, kn_ref, vp_ref, vc_ref, vn_ref, o_ref, lse_ref, kcat, vcat, *, tq, sub_len):
    i = pl.program_id(2)
    side = DIL_SIDE
    kcat[0:side, :] = kp_ref[...]
    kcat[side:side + tq, :] = kc_ref[...]
    kcat[side + tq:, :] = kn_ref[...]
    vcat[0:side, :] = vp_ref[...]
    vcat[side:side + tq, :] = vc_ref[...]
    vcat[side + tq:, :] = vn_ref[...]
    qs, ks = 2 * side, 4 * side
    lane = lax.broadcasted_iota(jnp.int32, (qs, LANES), 1)
    ii = lax.broadcasted_iota(jnp.int32, (qs, ks), 0)
    jj = lax.broadcasted_iota(jnp.int32, (qs, ks), 1)
    band = (jj - ii >= 0) & (jj - ii <= 2 * side)
    for sb in range(tq // qs):
        q0 = sb * qs
        pos = jj + (i * tq + q0 - side)
        mask = band & (pos >= 0) & (pos < sub_len)
        lse_tile = jnp.zeros((qs, LANES), f32)
        for h in range(DIL_HEADS):
            t = h // 2
            qt = q_ref[q0:q0 + qs, t * LANES:(t + 1) * LANES]
            keep = (lane < HEAD_DIM) if h % 2 == 0 else (lane >= HEAD_DIM)
            qm = jnp.where(keep, qt, jnp.zeros_like(qt))
            s = _dot_nt(qm, kcat[q0:q0 + ks, t * LANES:(t + 1) * LANES])
            s = jnp.where(mask, s, NEG_INF)
            m = jnp.max(s, axis=-1, keepdims=True)
            p = jnp.exp2(s - m)
            l = jnp.sum(p, axis=-1, keepdims=True)
            o = _dot(p.astype(bf16), vcat[q0:q0 + ks, h * LANES:(h + 1) * LANES]) / l
            o_ref[q0:q0 + qs, h * LANES:(h + 1) * LANES] = o.astype(o_ref.dtype)
            lse_tile = jnp.where(lane == h, m + jnp.log2(l), lse_tile)
        lse_ref[q0:q0 + qs, :] = lse_tile


def _dil_call(hc):
    B, d, L, _ = hc.shape
    tq = min(512, L)
    side = DIL_SIDE
    nblk = tq // side
    last = L // side - 1
    qw, vw = 2 * LANES, 4 * LANES
    prev = lambda i: jnp.maximum(i * nblk - 1, 0)
    nxt = lambda i: jnp.minimum((i + 1) * nblk, last)
    in_specs = [
        pl.BlockSpec((None, None, tq, qw), lambda b, r, i: (b, r, i, 0)),
        pl.BlockSpec((None, None, side, qw), lambda b, r, i: (b, r, prev(i), 1)),
        pl.BlockSpec((None, None, tq, qw), lambda b, r, i: (b, r, i, 1)),
        pl.BlockSpec((None, None, side, qw), lambda b, r, i: (b, r, nxt(i), 1)),
        pl.BlockSpec((None, None, side, vw), lambda b, r, i: (b, r, prev(i), 1)),
        pl.BlockSpec((None, None, tq, vw), lambda b, r, i: (b, r, i, 1)),
        pl.BlockSpec((None, None, side, vw), lambda b, r, i: (b, r, nxt(i), 1)),
    ]
    return pl.pallas_call(
        functools.partial(_dil_kernel, tq=tq, sub_len=L), grid=(B, d, L // tq), in_specs=in_specs,
        out_specs=[pl.BlockSpec((None, None, tq, vw), lambda b, r, i: (b, r, i, 0)),
                   pl.BlockSpec((None, None, tq, LANES), lambda b, r, i: (b, r, i, 0))],
        out_shape=[jax.ShapeDtypeStruct((B, d, L, vw), bf16), jax.ShapeDtypeStruct((B, d, L, LANES), f32)],
        scratch_shapes=[pltpu.VMEM((tq + 2 * side, qw), bf16), pltpu.VMEM((tq + 2 * side, vw), bf16)],
        compiler_params=_cparams(("parallel", "parallel", "arbitrary")),
        name=f"dilated_d{d}")(hc, hc, hc, hc, hc, hc, hc)


def _token_order(ref, stage):
    d, n, w = ref.shape
    if d == 1:
        return ref[0].astype(f32)
    nt = w // LANES
    for r in range(d):
        v = ref[r].astype(f32)
        for t in range(nt):
            stage[t, pl.ds(r, n, stride=d), :] = v[:, t * LANES:(t + 1) * LANES]
    return jnp.concatenate([stage[t] for t in range(nt)], axis=1)


def _merge_kernel(x_ref, oa_ref, ob_ref, o1_ref, o2_ref, o3_ref, l1_ref, l2_ref, l3_ref, g_ref,
                  wa_ref, wb_ref, wc_ref, wo_ref, lg_ref, lb_ref, rw_ref, rb_ref, x32_ref, xb_ref, gate_ref,
                  so2, so3, sl2, sl3, *, alpha):
    l1, l2, l3 = _token_order(l1_ref, None), _token_order(l2_ref, sl2), _token_order(l3_ref, sl3)
    o1, o2, o3 = _token_order(o1_ref, None), _token_order(o2_ref, so2), _token_order(o3_ref, so3)
    mx = jnp.maximum(jnp.maximum(l1, l2), l3)
    e1, e2, e3 = jnp.exp2(l1 - mx), jnp.exp2(l2 - mx), jnp.exp2(l3 - mx)
    den = e1 + e2 + e3
    w1, w2, w3 = e1 / den, e2 / den, e3 / den
    tiles = []
    for h in range(DIL_HEADS):
        sl = slice(h * LANES, (h + 1) * LANES)
        oc = w1[:, h:h + 1] * o1[:, sl] + w2[:, h:h + 1] * o2[:, sl] + w3[:, h:h + 1] * o3[:, sl]
        tiles.append(oc.astype(bf16))
    oc = jnp.concatenate(tiles, axis=1)
    obf = ob_ref[...].astype(f32)
    half = LANES // 2
    ob = jnp.concatenate([obf[:, 2 * j * LANES:(2 * j + 1) * LANES]
                          + pltpu.roll(obf[:, (2 * j + 1) * LANES:(2 * j + 2) * LANES], half, 1)
                          for j in range(MLA_HEADS // 2)], axis=1).astype(bf16)
    D = D_MODEL
    y = (g_ref[:, 0:D].astype(f32) * _dot(oa_ref[...], wa_ref[...])
         + g_ref[:, D:2 * D].astype(f32) * _dot(ob, wb_ref[...])
         + g_ref[:, 2 * D:3 * D].astype(f32) * _dot(oc, wc_ref[...]))
    mix = _dot(y.astype(bf16), wo_ref[...])
    out = _layer_norm(alpha * x_ref[...] + mix, lg_ref[...], lb_ref[...])
    x32_ref[...] = out
    xb = out.astype(bf16)
    xb_ref[...] = xb
    gate_ref[...] = _route(_dot_nt(rw_ref[...], xb), rb_ref[...])


def _merge_call(x32, oa, ob, dil_o, dil_l, hg, wa, wb, wc, wo, lg, lb, rw_t, rb, alpha, B, S, tm=512):
    T, D = x32.shape
    nb = S // tm
    row = lambda a: pl.BlockSpec((tm, a.shape[1]), lambda b, i: (b * nb + i, 0))
    cls = lambda a: pl.BlockSpec((None, a.shape[1], tm // a.shape[1], a.shape[3]), lambda b, i: (b, 0, i, 0))
    full = lambda a: pl.BlockSpec(a.shape, lambda b, i: (0, 0))
    consts = [wa, wb, wc, wo, lg, lb, rw_t, rb]
    ow, lw = dil_o[0].shape[3], dil_l[0].shape[3]
    return pl.pallas_call(
        functools.partial(_merge_kernel, alpha=alpha), grid=(B, nb),
        in_specs=[row(x32), row(oa), row(ob)] + [cls(a) for a in dil_o] + [cls(a) for a in dil_l] + [row(hg)]
        + [full(a) for a in consts],
        out_specs=[pl.BlockSpec((tm, D), lambda b, i: (b * nb + i, 0))] * 2
        + [pl.BlockSpec((N_EXPERTS, tm), lambda b, i: (0, b * nb + i))],
        out_shape=[jax.ShapeDtypeStruct((T, D), f32), jax.ShapeDtypeStruct((T, D), bf16),
                   jax.ShapeDtypeStruct((N_EXPERTS, T), f32)],
        scratch_shapes=[pltpu.VMEM((ow // LANES, tm, LANES), f32), pltpu.VMEM((ow // LANES, tm, LANES), f32),
                        pltpu.VMEM((lw // LANES, tm, LANES), f32), pltpu.VMEM((lw // LANES, tm, LANES), f32)],
        compiler_params=_cparams(("parallel", "parallel")), name="merge")(
            x32, oa, ob, *dil_o, *dil_l, hg, *consts)


def _first_true(flags):
    out, seen = [], None
    for f in flags:
        out.append(f if seen is None else f & ~seen)
        seen = f if seen is None else seen | f
    return out


def _route(logits, bias):
    scores = jax.nn.sigmoid(logits)
    biased = scores + bias
    sc = [scores[e:e + 1, :] for e in range(N_EXPERTS)]
    bi = [biased[e:e + 1, :] for e in range(N_EXPERTS)]
    npg = EXPERTS_PER_GROUP
    gscore = []
    for g in range(N_EXPERT_GROUPS):
        v = bi[g * npg:(g + 1) * npg]
        best = None
        for a in range(npg):
            for b in range(a + 1, npg):
                best = v[a] + v[b] if best is None else jnp.maximum(best, v[a] + v[b])
        gscore.append(best)
    gbest = functools.reduce(jnp.maximum, gscore)
    gsel = _first_true([gs == gbest for gs in gscore])
    zero = jnp.zeros_like(gbest)
    bsel = [functools.reduce(lambda x, y: x + y, [jnp.where(gsel[g], bi[g * npg + j], zero)
                                                  for g in range(N_EXPERT_GROUPS)]) for j in range(npg)]
    ssel = [functools.reduce(lambda x, y: x + y, [jnp.where(gsel[g], sc[g * npg + j], zero)
                                                  for g in range(N_EXPERT_GROUPS)]) for j in range(npg)]
    t1 = functools.reduce(jnp.maximum, bsel)
    i1 = _first_true([b == t1 for b in bsel])
    rest = [jnp.where(i1[j], -jnp.inf, bsel[j]) for j in range(npg)]
    t2 = functools.reduce(jnp.maximum, rest)
    i2 = _first_true([rest[j] == t2 for j in range(npg)])
    w1 = functools.reduce(lambda x, y: x + y, [jnp.where(i1[j], ssel[j], zero) for j in range(npg)])
    w2 = functools.reduce(lambda x, y: x + y, [jnp.where(i2[j], ssel[j], zero) for j in range(npg)])
    den = w1 + w2
    local = [jnp.where(i1[j], w1 / den, zero) + jnp.where(i2[j], w2 / den, zero) for j in range(npg)]
    rows = [jnp.where(gsel[e // npg], local[e % npg], zero) for e in range(N_EXPERTS)]
    return jnp.concatenate(rows, axis=0)


def _moe_kernel(x32_ref, xb_ref, gate_ref, wg_ref, wu_ref, wd_ref, wsg_ref, wsu_ref, wsd_ref, lg_ref, lb_ref,
                o32_ref, ob_ref, acc_sc, *, alpha, n_chunks, per_chunk):
    c = pl.program_id(1)
    x = xb_ref[...]

    @pl.when(c == 0)
    def _():
        hs = jax.nn.silu(_dot(x, wsg_ref[...])) * _dot(x, wsu_ref[...])
        acc_sc[...] = _dot(hs.astype(bf16), wsd_ref[...])

    gate = gate_ref[...]
    hid = []
    for e in range(per_chunk):
        h = jax.nn.silu(_dot(x, wg_ref[e])) * _dot(x, wu_ref[e]) * gate[:, e:e + 1]
        hid.append(h.astype(bf16))
    acc_sc[...] += _dot(jnp.concatenate(hid, axis=1), wd_ref[...])

    @pl.when(c == n_chunks - 1)
    def _():
        out = _layer_norm(alpha * x32_ref[...] + acc_sc[...], lg_ref[...], lb_ref[...])
        o32_ref[...] = out
        ob_ref[...] = out.astype(bf16)


def _moe_call(x32, xb, gate_c, wg, wu, wd, wsg, wsu, wsd, lg, lb, alpha, tm=1024, per_chunk=4):
    T, D = x32.shape
    n_chunks = N_EXPERTS // per_chunk
    cw = per_chunk * EXPERT_HIDDEN
    row = pl.BlockSpec((tm, D), lambda i, c: (i, 0))
    full = lambda a: pl.BlockSpec(a.shape, lambda i, c: (0, 0))
    return pl.pallas_call(
        functools.partial(_moe_kernel, alpha=alpha, n_chunks=n_chunks, per_chunk=per_chunk),
        grid=(T // tm, n_chunks),
        in_specs=[row, row, pl.BlockSpec((None, tm, per_chunk), lambda i, c: (c, i, 0)),
                  pl.BlockSpec((per_chunk, D, EXPERT_HIDDEN), lambda i, c: (c, 0, 0)),
                  pl.BlockSpec((per_chunk, D, EXPERT_HIDDEN), lambda i, c: (c, 0, 0)),
                  pl.BlockSpec((cw, D), lambda i, c: (c, 0)),
                  full(wsg), full(wsu), full(wsd), full(lg), full(lb)],
        out_specs=[row, row],
        out_shape=[jax.ShapeDtypeStruct((T, D), f32), jax.ShapeDtypeStruct((T, D), bf16)],
        scratch_shapes=[pltpu.VMEM((tm, D), f32)],
        compiler_params=_cparams(("parallel", "arbitrary")), name="moe")(
            x32, xb, gate_c, wg, wu, wd, wsg, wsu, wsd, lg, lb)


def _rope_tables(positions):
    pos = positions.astype(f32).reshape(-1, 1)
    T = pos.shape[0]
    ones, zeros = jnp.ones, jnp.zeros
    inv_p = ROPE_THETA ** (-jnp.arange(0, ROT_DIM, 2, dtype=f32) / ROT_DIM)
    ang = pos * inv_p
    c, s = jnp.cos(ang), jnp.sin(ang)
    pad = HEAD_DIM - ROT_DIM
    cos_p = jnp.tile(jnp.concatenate([c, c, ones((T, pad), f32)], axis=1), (1, 2))
    sin_p = jnp.tile(jnp.concatenate([-s, s, zeros((T, pad), f32)], axis=1), (1, 2))
    inv_m = ROPE_THETA ** (-jnp.arange(0, MLA_ROPE_DIM, 2, dtype=f32) / MLA_ROPE_DIM)
    ang = pos * inv_m
    c, s = jnp.cos(ang), jnp.sin(ang)
    tail = LANES - MLA_NOPE_DIM - MLA_ROPE_DIM
    cos_m = jnp.concatenate([ones((T, MLA_NOPE_DIM), f32), c, c, ones((T, tail), f32)], axis=1)
    sin_m = jnp.concatenate([zeros((T, MLA_NOPE_DIM), f32), -s, s, zeros((T, tail), f32)], axis=1)
    return cos_p, sin_p, cos_m, sin_m


def _pack_w_in(w):
    offs = [0]
    for sz in IN_SIZES:
        offs.append(offs[-1] + sz)
    a_q, a_k, a_v, b_cq, b_ckv, b_kr, c_q, c_k, c_v, g = [w[:, offs[i]:offs[i + 1]] for i in range(len(IN_SIZES))]
    D = w.shape[0]
    z = lambda n: jnp.zeros((D, n), w.dtype)
    kr_pad = jnp.concatenate([z(MLA_NOPE_DIM), b_kr, z(LANES - MLA_NOPE_DIM - MLA_ROPE_DIM)], axis=1)
    cols = [a_q, a_k, a_v, b_ckv, kr_pad, b_cq]
    for gi in range(DIL_GROUPS):
        cols += [c_q[:, 256 * gi:256 * (gi + 1)], c_k[:, 256 * gi:256 * (gi + 1)], c_v[:, 512 * gi:512 * (gi + 1)]]
    cols.append(g)
    return jnp.concatenate(cols, axis=1).astype(bf16)


def _pad_heads(w, n_heads, lo, hi):
    K = w.shape[0]
    per = w.shape[1] // n_heads
    wh = w.reshape(K, n_heads, per)[:, :, lo:hi]
    wh = jnp.pad(wh, ((0, 0), (0, 0), (0, LANES - (hi - lo))))
    return wh.reshape(K, n_heads * LANES).astype(bf16)


def kernel(x, positions, ln_in_g, ln_in_b, w_in, b_gate, lam_q1, lam_k1, lam_q2, lam_k2, diff_norm_g, mla_q_norm_g, mla_kv_norm_g, w_mla_qb, w_mla_kvb, w_branch_a, w_branch_b, w_branch_c, w_out, ln1_g, ln1_b, router_w, router_bias, w_exp_gate, w_exp_up, w_exp_down, w_sh_gate, w_sh_up, w_sh_down, ln2_g, ln2_b):
    B, S, D = x.shape
    T = B * S
    depth = w_in.shape[0]
    assert D == D_MODEL and S % (DIL_PATTERNS[-1][1] * 2 * DIL_SIDE) == 0 and S % FLASH_COLS == 0 and S % FLASH_TK == 0
    alpha = (2 * depth) ** 0.25
    cos_p, sin_p, cos_m, sin_m = _rope_tables(positions)
    rw_t = router_w.T.astype(bf16)
    rb = router_bias.reshape(N_EXPERTS, 1).astype(f32)
    vec = lambda v: v.reshape(1, -1).astype(f32)
    per_q = MLA_NOPE_DIM + MLA_ROPE_DIM
    per_kv = MLA_NOPE_DIM + MLA_V_DIM
    for l in range(depth):
        lam_init = 0.8 - 0.6 * math.exp(-0.3 * l)
        proj = _inproj_call(
            x.reshape(T, D) if l == 0 else xb, _pack_w_in(w_in[l]), vec(b_gate[l]), cos_p, sin_p, cos_m, sin_m,
            vec(mla_q_norm_g[l]), vec(mla_kv_norm_g[l]), _pad_heads(w_mla_qb[l], MLA_HEADS, 0, per_q),
            _pad_heads(w_mla_kvb[l], MLA_HEADS, 0, MLA_NOPE_DIM), _pad_heads(w_mla_kvb[l], MLA_HEADS, MLA_NOPE_DIM, per_kv),
            B, S, ln=(vec(ln_in_g), vec(ln_in_b)) if l == 0 else None)
        ha, qb, kb, vb, hc0, hc1, hc2, hg = proj[:8]
        if l == 0:
            x32, xb = proj[8:]
        lam_vec = jnp.pad(jnp.stack([lam_q1[l], lam_k1[l], lam_q2[l], lam_k2[l]]).astype(f32),
                          ((0, 4), (0, LANES - HEAD_DIM)))
        ha3 = ha.reshape(B, S, HA_OUT_W)
        oa = _flash_call(ha3, ha3, ha3, DIFF_HEADS, 0, DIFF_HEADS, DIFF_HEADS, n_maps=2, tq=FLASH_COLS // 2, tk=FLASH_TK,
                         name="diff_attn", lam_vec=lam_vec, norm_g=diff_norm_g[l].reshape(-1, 1).astype(f32),
                         lam_init=lam_init, vw=2 * LANES)
        ob = _flash_call(qb.reshape(B, S, -1), kb.reshape(B, S, -1), vb.reshape(B, S, -1), MLA_HEADS, 0, 0, 0,
                         n_maps=1, tq=FLASH_COLS, tk=FLASH_TK, name="mla_attn", dv=MLA_V_DIM)
        dil = [_dil_call(hc) for hc in (hc0, hc1, hc2)]
        x32, xb, gate_t = _merge_call(
            x32, oa.reshape(T, -1), ob.reshape(T, -1), [o for o, _ in dil], [lse for _, lse in dil],
            hg, w_branch_a[l].astype(bf16), w_branch_b[l].astype(bf16), w_branch_c[l].astype(bf16),
            w_out[l].astype(bf16), vec(ln1_g[l]), vec(ln1_b[l]), rw_t, rb, alpha, B, S)
        per_chunk = 4
        gate_c = gate_t.reshape(N_EXPERTS // per_chunk, per_chunk, T).transpose(0, 2, 1)
        wg = w_exp_gate[l].astype(bf16)
        wu = w_exp_up[l].astype(bf16)
        wd = w_exp_down[l].reshape(N_EXPERTS * EXPERT_HIDDEN, D).astype(bf16)
        x32, xb = _moe_call(x32, xb, gate_c, wg, wu, wd, w_sh_gate[l].astype(bf16), w_sh_up[l].astype(bf16),
                            w_sh_down[l].astype(bf16), vec(ln2_g[l]), vec(ln2_b[l]), alpha, per_chunk=per_chunk)
    return x32.reshape(B, S, D)
```

```python
import functools
import math

import jax
import jax.numpy as jnp
from jax import lax
from jax.experimental import pallas as pl
from jax.experimental.pallas import tpu as pltpu

f32 = jnp.float32
bf16 = jnp.bfloat16

D_MODEL = 1024
HEAD_DIM = 64
ROPE_THETA = 500000.0
ROT_DIM = HEAD_DIM // 4
DIFF_HEADS = 4
MLA_HEADS = 8
MLA_Q_RANK = 384
MLA_KV_RANK = 256
MLA_NOPE_DIM = 64
MLA_ROPE_DIM = 32
MLA_V_DIM = 64
DIL_PATTERNS = ((128, 1), (512, 4), (2048, 16))
DIL_GROUPS = 3
DIL_HEADS = 4
DIL_SIDE = 64
N_EXPERTS = 16
N_EXPERT_GROUPS = 4
EXPERTS_PER_GROUP = 4
EXPERT_HIDDEN = 256
MOE_CHUNK = EXPERTS_PER_GROUP
IN_SIZES = (512, 512, 512, 384, 256, 32, 768, 768, 1536, 3072)
NEG_INF = -1e30
LOG2E = math.log2(math.e)

LANES = 128
VMEM_LIMIT = 52 * 1024 * 1024

HA_W = 1536
HA_OUT_W = 2048
HB_W = 768
HCG_W = 1024
HC_W = 3 * HCG_W
HG_W = 3072

FLASH_TK = 512
FLASH_COLS = 1024
FLASH_UNROLL = 8
SUM_ROWS = 16


def _cparams(sem):
    return pltpu.CompilerParams(dimension_semantics=sem, vmem_limit_bytes=VMEM_LIMIT)


def _layer_norm(z, g, b):
    mu = jnp.mean(z, axis=-1, keepdims=True)
    zc = z - mu
    var = jnp.mean(zc * zc, axis=-1, keepdims=True)
    return zc * lax.rsqrt(var + 1e-5) * g + b


def _rms_norm(z, g):
    return z * lax.rsqrt(jnp.mean(z * z, axis=-1, keepdims=True) + 1e-6) * g


def _dot(a, b):
    return jnp.dot(a, b, preferred_element_type=f32)


def _dot_nt(a, b):
    return lax.dot_general(a, b, (((1,), (1,)), ((), ())), preferred_element_type=f32)


def _rope_tile(xt, cos_t, sin_t, first_half, shift):
    xr = jnp.where(first_half, pltpu.roll(xt, LANES - shift, 1), pltpu.roll(xt, shift, 1))
    return xt * cos_t + xr * sin_t


def _inproj_plan():
    qscale = HEAD_DIM ** -0.5 * LOG2E
    plan = []
    for c in range(0, 512, 256):
        plan.append((c, 256, "rope", 0, c, qscale))
    for c in range(512, 1024, 256):
        plan.append((c, 256, "rope", 0, c, 1.0))
    plan.append((1024, 512, "value", 0, 1024, 1.0))
    plan.append((HA_W, HB_W, "mla", 1, 0, 1.0))
    base = HA_W + HB_W
    for g in range(DIL_GROUPS):
        o = HCG_W * g
        plan.append((base + o, 256, "rope", 2 + g, 0, qscale))
        plan.append((base + o + 256, 256, "rope", 2 + g, 256, 1.0))
        plan.append((base + o + 512, 512, "plain", 2 + g, 512, 1.0))
    base = HA_W + HB_W + HC_W
    for c in range(0, HG_W, 512):
        plan.append((base + c, 512, "gate", 2 + DIL_GROUPS, c, 1.0))
    return plan


def _mla_up_projections(acc, lane, cm_ref, sm_ref, qg_ref, kvg_ref, wq_ref, wk_ref, wv_ref, qb_ref, kb_ref, vb_ref):
    cos_t, sin_t = cm_ref[...], sm_ref[...]
    first_half = lane < MLA_NOPE_DIM + MLA_ROPE_DIM // 2
    shift = MLA_ROPE_DIM // 2
    scale = (MLA_NOPE_DIM + MLA_ROPE_DIM) ** -0.5 * LOG2E
    cn = _rms_norm(acc[:, 0:MLA_KV_RANK], kvg_ref[...]).astype(bf16)
    kr = _rope_tile(acc[:, MLA_KV_RANK:MLA_KV_RANK + LANES], cos_t, sin_t, first_half, shift)
    qn = _rms_norm(acc[:, MLA_KV_RANK + LANES:], qg_ref[...]).astype(bf16)
    q_all = _dot(qn, wq_ref[...])
    k_all = _dot(cn, wk_ref[...])
    v_all = _dot(cn, wv_ref[...])
    for h in range(MLA_HEADS):
        sl = slice(h * LANES, (h + 1) * LANES)
        q = _rope_tile(q_all[:, sl], cos_t, sin_t, first_half, shift) * scale
        qb_ref[:, sl] = q.astype(qb_ref.dtype)
        kb_ref[:, sl] = (k_all[:, sl] + kr).astype(kb_ref.dtype)
        vb_ref[:, sl] = jnp.where(lane == MLA_V_DIM, 1.0, v_all[:, sl]).astype(vb_ref.dtype)


def _inproj_kernel(x_ref, w_ref, bg_ref, cp_ref, sp_ref, cm_ref, sm_ref, qg_ref, kvg_ref, wq_ref, wk_ref, wv_ref,
                   *rest, ln_first):
    if ln_first:
        lng_ref, lnb_ref, ha_ref, qb_ref, kb_ref, vb_ref, hc0_ref, hc1_ref, hc2_ref, hg_ref, x32_ref, xb_ref, stage = rest
        y = _layer_norm(x_ref[...], lng_ref[...], lnb_ref[...])
        x32_ref[...] = y
        x = y.astype(bf16)
        xb_ref[...] = x
    else:
        ha_ref, qb_ref, kb_ref, vb_ref, hc0_ref, hc1_ref, hc2_ref, hg_ref, stage = rest
        x = x_ref[...]
    outs = (ha_ref, None, hc0_ref, hc1_ref, hc2_ref, hg_ref)
    tm = x.shape[0]
    cos_t = cp_ref[...]
    sin_t = sp_ref[...]
    lane = lax.broadcasted_iota(jnp.int32, cos_t.shape, 1)
    first_half = (lane % HEAD_DIM) < (ROT_DIM // 2)
    for (c0, w, kind, oi, o0, scale) in _inproj_plan():
        acc = _dot(x, w_ref[:, c0:c0 + w])
        if kind == "rope":
            tiles = []
            for t in range(w // LANES):
                y = _rope_tile(acc[:, t * LANES:(t + 1) * LANES], cos_t, sin_t, first_half, ROT_DIM // 2)
                tiles.append(y * scale if scale != 1.0 else y)
            acc = jnp.concatenate(tiles, axis=1)
        elif kind == "gate":
            acc = jax.nn.sigmoid(acc + bg_ref[:, o0:o0 + w])
        if kind == "mla":
            _mla_up_projections(acc, lane, cm_ref, sm_ref, qg_ref, kvg_ref, wq_ref, wk_ref, wv_ref,
                                qb_ref, kb_ref, vb_ref)
            continue
        out = outs[oi]
        if kind == "value":
            ones_tile = (lane == 0).astype(out.dtype)
            for h in range(w // LANES):
                out[:, o0 + 2 * h * LANES:o0 + (2 * h + 1) * LANES] = acc[:, h * LANES:(h + 1) * LANES].astype(out.dtype)
                out[:, o0 + (2 * h + 1) * LANES:o0 + (2 * h + 2) * LANES] = ones_tile
        elif 2 <= oi < 2 + DIL_GROUPS:
            d = DIL_PATTERNS[oi - 2][1]
            if d == 1:
                out[0, :, o0:o0 + w] = acc.astype(out.dtype)
            else:
                nt = w // LANES
                for t in range(nt):
                    stage[t] = acc[:, t * LANES:(t + 1) * LANES]
                for r in range(d):
                    rows = [stage[t, pl.ds(r, tm // d, stride=d), :] for t in range(nt)]
                    out[r, :, o0:o0 + w] = jnp.concatenate(rows, axis=1).astype(out.dtype)
        else:
            out[:, o0:o0 + w] = acc.astype(out.dtype)


def _inproj_call(x, w, bg, cos_t, sin_t, cos_m, sin_m, qg, kvg, wq, wk, wv, B, S, ln=None, tm=256):
    T, D = x.shape
    NW = w.shape[1]
    nb = S // tm
    MW = MLA_HEADS * LANES
    row = lambda width: pl.BlockSpec((tm, width), lambda b, i: (b * nb + i, 0))
    const = lambda shape: pl.BlockSpec(shape, lambda b, i: (0, 0))
    consts = [qg, kvg, wq, wk, wv] + (list(ln) if ln else [])
    extra_specs = [row(D), row(D)] if ln else []
    extra_shapes = [jax.ShapeDtypeStruct((T, D), f32), jax.ShapeDtypeStruct((T, D), bf16)] if ln else []
    hc_specs, hc_shapes = [], []
    for _, d in DIL_PATTERNS:
        hc_specs.append(pl.BlockSpec((None, d, tm // d, HCG_W), lambda b, i: (b, 0, i, 0)))
        hc_shapes.append(jax.ShapeDtypeStruct((B, d, S // d, HCG_W), bf16))
    return pl.pallas_call(
        functools.partial(_inproj_kernel, ln_first=bool(ln)), grid=(B, nb),
        in_specs=[row(D), const((D, NW)), const((1, HG_W)), row(LANES), row(LANES), row(LANES), row(LANES)]
        + [const(a.shape) for a in consts],
        out_specs=[row(HA_OUT_W), row(MW), row(MW), row(MW)] + hc_specs + [row(HG_W)] + extra_specs,
        out_shape=[jax.ShapeDtypeStruct((T, HA_OUT_W), bf16)] + [jax.ShapeDtypeStruct((T, MW), bf16)] * 3 + hc_shapes
        + [jax.ShapeDtypeStruct((T, HG_W), bf16)] + extra_shapes,
        scratch_shapes=[pltpu.VMEM((4, tm, LANES), f32)],
        compiler_params=_cparams(("parallel", "parallel")), name="inproj")(
            x, w, bg, cos_t, sin_t, cos_m, sin_m, *consts)


def _flash_kernel(*refs, n_maps, tq, tk, n_kv, lam_init, dv):
    if n_maps == 2:
        q_ref, qn_ref, k_ref, v_ref, lam_ref, g_ref, o_ref, qt_sc, m_sc, acc_sc, s_sc, mc_sc = refs
    else:
        q_ref, qn_ref, k_ref, v_ref, o_ref, qt_sc, m_sc, acc_sc, s_sc, mc_sc = refs
    i = pl.program_id(2)
    cur = i % 2
    nxt = 1 - cur

    def load_queries(ref, slot):
        q = ref[...].astype(f32)
        if n_maps == 2:
            lane = lax.broadcasted_iota(jnp.int32, q.shape, 1)
            zero = jnp.zeros_like(q)
            q = jnp.concatenate([jnp.where(lane < HEAD_DIM, q, zero), jnp.where(lane >= HEAD_DIM, q, zero)], axis=0)
        qt_sc[slot] = q.T.astype(bf16)

    def scores(j, buf, slot):
        off = pl.multiple_of(j * tk, tk)
        s = _dot(k_ref[pl.ds(off, tk), :], qt_sc[slot])
        s_sc[buf] = s
        mc_sc[buf] = jnp.max(s, axis=0, keepdims=True)

    def accumulate(j, buf):
        off = pl.multiple_of(j * tk, tk)
        m_prev = m_sc[...]
        m_new = jnp.maximum(m_prev, mc_sc[buf])
        p = jnp.exp2(s_sc[buf] - m_new)
        a = jnp.exp2(m_prev - m_new)
        pv = lax.dot_general(v_ref[pl.ds(off, tk), 0:dv + SUM_ROWS], p.astype(bf16), (((0,), (0,)), ((), ())),
                             preferred_element_type=f32)
        acc_sc[...] = a * acc_sc[...] + pv
        m_sc[...] = m_new

    @pl.when(i == 0)
    def _():
        load_queries(q_ref, cur)
        scores(0, 0, cur)

    load_queries(qn_ref, nxt)
    m_sc[...] = jnp.full(m_sc.shape, -jnp.inf, f32)
    acc_sc[...] = jnp.zeros(acc_sc.shape, f32)

    unroll = min(FLASH_UNROLL, n_kv)
    assert unroll % 2 == 0 and n_kv % unroll == 0

    def body(jj, carry):
        for u in range(unroll):
            j = unroll * jj + u
            scores(j + 1, (u + 1) % 2, cur)
            accumulate(j, u % 2)
        return carry

    lax.fori_loop(0, n_kv // unroll - 1, body, 0)
    for j in range(n_kv - unroll, n_kv):
        if j + 1 < n_kv:
            scores(j + 1, (j + 1) % 2, cur)
        else:
            scores(0, 0, nxt)
        accumulate(j, j % 2)
    o = acc_sc[0:dv, :] / acc_sc[dv:dv + 1, :]
    if n_maps == 2:
        lv = lam_ref[...]
        lam = (jnp.exp(jnp.sum(lv[0:1] * lv[1:2], axis=-1, keepdims=True))
               - jnp.exp(jnp.sum(lv[2:3] * lv[3:4], axis=-1, keepdims=True)) + lam_init)
        o = o[:, :tq] - lam * o[:, tq:]
        ms = jnp.mean(o * o, axis=0, keepdims=True)
        o = o * lax.rsqrt(ms + 1e-6) * g_ref[...] * (1.0 - lam_init)
    if dv < LANES:
        o = jnp.concatenate([o, jnp.zeros((LANES - dv, o.shape[1]), f32)], axis=0)
    o_ref[...] = o.T.astype(o_ref.dtype)


def _flash_call(q_arr, k_arr, v_arr, n_heads, q_blk0, k_blk0, v_blk0, n_maps, tq, tk, name,
                lam_vec=None, norm_g=None, lam_init=0.0, dv=LANES, vw=LANES):
    B, S, _ = q_arr.shape
    cols = n_maps * tq
    last = S // tq - 1
    assert dv + SUM_ROWS <= vw
    in_specs = [pl.BlockSpec((None, tq, LANES), lambda b, h, i: (b, i, q_blk0 + h)),
                pl.BlockSpec((None, tq, LANES), lambda b, h, i: (b, jnp.minimum(i + 1, last), q_blk0 + h)),
                pl.BlockSpec((None, S, LANES), lambda b, h, i: (b, 0, k_blk0 + h)),
                pl.BlockSpec((None, S, vw), lambda b, h, i: (b, 0, v_blk0 + h))]
    args = [q_arr, q_arr, k_arr, v_arr]
    if n_maps == 2:
        in_specs += [pl.BlockSpec((8, LANES), lambda b, h, i: (0, 0)),
                     pl.BlockSpec((LANES, 1), lambda b, h, i: (0, 0))]
        args += [lam_vec, norm_g]
    kern = functools.partial(_flash_kernel, n_maps=n_maps, tq=tq, tk=tk, n_kv=S // tk, lam_init=lam_init,
                             dv=dv)
    return pl.pallas_call(
        kern, grid=(B, n_heads, S // tq), in_specs=in_specs,
        out_specs=pl.BlockSpec((None, tq, LANES), lambda b, h, i: (b, i, h)),
        out_shape=jax.ShapeDtypeStruct((B, S, n_heads * LANES), bf16),
        scratch_shapes=[pltpu.VMEM((2, LANES, cols), bf16), pltpu.VMEM((1, cols), f32),
                        pltpu.VMEM((dv + SUM_ROWS, cols), f32), pltpu.VMEM((2, tk, cols), f32),
                        pltpu.VMEM((2, 1, cols), f32)],
        compiler_params=_cparams(("parallel", "parallel", "arbitrary")), name=name)(*args)


def _dil_kernel(q_ref, kp_ref, kc_ref, kn_ref, vp_ref, vc_ref, vn_ref, o_ref, lse_ref, kcat, vcat, *, tq, sub_len):
    i = pl.program_id(2)
    side = DIL_SIDE
    kcat[0:side, :] = kp_ref[...]
    kcat[side:side + tq, :] = kc_ref[...]
    kcat[side + tq:, :] = kn_ref[...]
    vcat[0:side, :] = vp_ref[...]
    vcat[side:side + tq, :] = vc_ref[...]
    vcat[side + tq:, :] = vn_ref[...]
    qs, ks = 2 * side, 4 * side
    lane = lax.broadcasted_iota(jnp.int32, (qs, LANES), 1)
    ii = lax.broadcasted_iota(jnp.int32, (qs, ks), 0)
    jj = lax.broadcasted_iota(jnp.int32, (qs, ks), 1)
    band = (jj - ii >= 0) & (jj - ii <= 2 * side)
    for sb in range(tq // qs):
        q0 = sb * qs
        pos = jj + (i * tq + q0 - side)
        mask = band & (pos >= 0) & (pos < sub_len)
        lse_tile = jnp.zeros((qs, LANES), f32)
        for h in range(DIL_HEADS):
            t = h // 2
            qt = q_ref[q0:q0 + qs, t * LANES:(t + 1) * LANES]
            keep = (lane < HEAD_DIM) if h % 2 == 0 else (lane >= HEAD_DIM)
            qm = jnp.where(keep, qt, jnp.zeros_like(qt))
            s = _dot_nt(qm, kcat[q0:q0 + ks, t * LANES:(t + 1) * LANES])
            s = jnp.where(mask, s, NEG_INF)
            m = jnp.max(s, axis=-1, keepdims=True)
            p = jnp.exp2(s - m)
            l = jnp.sum(p, axis=-1, keepdims=True)
            o = _dot(p.astype(bf16), vcat[q0:q0 + ks, h * LANES:(h + 1) * LANES]) / l
            o_ref[q0:q0 + qs, h * LANES:(h + 1) * LANES] = o.astype(o_ref.dtype)
            lse_tile = jnp.where(lane == h, m + jnp.log2(l), lse_tile)
        lse_ref[q0:q0 + qs, :] = lse_tile


def _dil_call(hc):
    B, d, L, _ = hc.shape
    tq = min(1024, L)
    side = DIL_SIDE
    nblk = tq // side
    last = L // side - 1
    qw, vw = 2 * LANES, 4 * LANES
    prev = lambda i: jnp.maximum(i * nblk - 1, 0)
    nxt = lambda i: jnp.minimum((i + 1) * nblk, last)
    in_specs = [
        pl.BlockSpec((None, None, tq, qw), lambda b, r, i: (b, r, i, 0)),
        pl.BlockSpec((None, None, side, qw), lambda b, r, i: (b, r, prev(i), 1)),
        pl.BlockSpec((None, None, tq, qw), lambda b, r, i: (b, r, i, 1)),
        pl.BlockSpec((None, None, side, qw), lambda b, r, i: (b, r, nxt(i), 1)),
        pl.BlockSpec((None, None, side, vw), lambda b, r, i: (b, r, prev(i), 1)),
        pl.BlockSpec((None, None, tq, vw), lambda b, r, i: (b, r, i, 1)),
        pl.BlockSpec((None, None, side, vw), lambda b, r, i: (b, r, nxt(i), 1)),
    ]
    return pl.pallas_call(
        functools.partial(_dil_kernel, tq=tq, sub_len=L), grid=(B, d, L // tq), in_specs=in_specs,
        out_specs=[pl.BlockSpec((None, None, tq, vw), lambda b, r, i: (b, r, i, 0)),
                   pl.BlockSpec((None, None, tq, LANES), lambda b, r, i: (b, r, i, 0))],
        out_shape=[jax.ShapeDtypeStruct((B, d, L, vw), bf16), jax.ShapeDtypeStruct((B, d, L, LANES), f32)],
        scratch_shapes=[pltpu.VMEM((tq + 2 * side, qw), bf16), pltpu.VMEM((tq + 2 * side, vw), bf16)],
        compiler_params=_cparams(("parallel", "parallel", "arbitrary")),
        name=f"dilated_d{d}")(hc, hc, hc, hc, hc, hc, hc)


def _token_order(ref, stage):
    d, n, w = ref.shape
    if d == 1:
        return ref[0].astype(f32)
    nt = w // LANES
    for r in range(d):
        v = ref[r].astype(f32)
        for t in range(nt):
            stage[t, pl.ds(r, n, stride=d), :] = v[:, t * LANES:(t + 1) * LANES]
    return jnp.concatenate([stage[t] for t in range(nt)], axis=1)


def _merge_kernel(x_ref, oa_ref, ob_ref, o1_ref, o2_ref, o3_ref, l1_ref, l2_ref, l3_ref, g_ref,
                  wa_ref, wb_ref, wc_ref, wo_ref, lg_ref, lb_ref, rw_ref, rb_ref, x32_ref, xb_ref, gate_ref,
                  so2, so3, sl2, sl3, *, alpha):
    l1, l2, l3 = _token_order(l1_ref, None), _token_order(l2_ref, sl2), _token_order(l3_ref, sl3)
    o1, o2, o3 = _token_order(o1_ref, None), _token_order(o2_ref, so2), _token_order(o3_ref, so3)
    mx = jnp.maximum(jnp.maximum(l1, l2), l3)
    e1, e2, e3 = jnp.exp2(l1 - mx), jnp.exp2(l2 - mx), jnp.exp2(l3 - mx)
    den = e1 + e2 + e3
    w1, w2, w3 = e1 / den, e2 / den, e3 / den
    tiles = []
    for h in range(DIL_HEADS):
        sl = slice(h * LANES, (h + 1) * LANES)
        oc = w1[:, h:h + 1] * o1[:, sl] + w2[:, h:h + 1] * o2[:, sl] + w3[:, h:h + 1] * o3[:, sl]
        tiles.append(oc.astype(bf16))
    oc = jnp.concatenate(tiles, axis=1)
    obf = ob_ref[...].astype(f32)
    half = LANES // 2
    ob = jnp.concatenate([obf[:, 2 * j * LANES:(2 * j + 1) * LANES]
                          + pltpu.roll(obf[:, (2 * j + 1) * LANES:(2 * j + 2) * LANES], half, 1)
                          for j in range(MLA_HEADS // 2)], axis=1).astype(bf16)
    D = D_MODEL
    y = (g_ref[:, 0:D].astype(f32) * _dot(oa_ref[...], wa_ref[...])
         + g_ref[:, D:2 * D].astype(f32) * _dot(ob, wb_ref[...])
         + g_ref[:, 2 * D:3 * D].astype(f32) * _dot(oc, wc_ref[...]))
    mix = _dot(y.astype(bf16), wo_ref[...])
    out = _layer_norm(alpha * x_ref[...] + mix, lg_ref[...], lb_ref[...])
    x32_ref[...] = out
    xb = out.astype(bf16)
    xb_ref[...] = xb
    gate_ref[...] = _route(_dot_nt(rw_ref[...], xb), rb_ref[...])


def _merge_call(x32, oa, ob, dil_o, dil_l, hg, wa, wb, wc, wo, lg, lb, rw_t, rb, alpha, B, S, tm=512):
    T, D = x32.shape
    nb = S // tm
    row = lambda a: pl.BlockSpec((tm, a.shape[1]), lambda b, i: (b * nb + i, 0))
    cls = lambda a: pl.BlockSpec((None, a.shape[1], tm // a.shape[1], a.shape[3]), lambda b, i: (b, 0, i, 0))
    full = lambda a: pl.BlockSpec(a.shape, lambda b, i: (0, 0))
    consts = [wa, wb, wc, wo, lg, lb, rw_t, rb]
    ow, lw = dil_o[0].shape[3], dil_l[0].shape[3]
    return pl.pallas_call(
        functools.partial(_merge_kernel, alpha=alpha), grid=(B, nb),
        in_specs=[row(x32), row(oa), row(ob)] + [cls(a) for a in dil_o] + [cls(a) for a in dil_l] + [row(hg)]
        + [full(a) for a in consts],
        out_specs=[pl.BlockSpec((tm, D), lambda b, i: (b * nb + i, 0))] * 2
        + [pl.BlockSpec((N_EXPERTS, tm), lambda b, i: (0, b * nb + i))],
        out_shape=[jax.ShapeDtypeStruct((T, D), f32), jax.ShapeDtypeStruct((T, D), bf16),
                   jax.ShapeDtypeStruct((N_EXPERTS, T), f32)],
        scratch_shapes=[pltpu.VMEM((ow // LANES, tm, LANES), f32), pltpu.VMEM((ow // LANES, tm, LANES), f32),
                        pltpu.VMEM((lw // LANES, tm, LANES), f32), pltpu.VMEM((lw // LANES, tm, LANES), f32)],
        compiler_params=_cparams(("parallel", "parallel")), name="merge")(
            x32, oa, ob, *dil_o, *dil_l, hg, *consts)


def _first_true(flags):
    out, seen = [], None
    for f in flags:
        out.append(f if seen is None else f & ~seen)
        seen = f if seen is None else seen | f
    return out


def _route(logits, bias):
    scores = jax.nn.sigmoid(logits)
    biased = scores + bias
    sc = [scores[e:e + 1, :] for e in range(N_EXPERTS)]
    bi = [biased[e:e + 1, :] for e in range(N_EXPERTS)]
    npg = EXPERTS_PER_GROUP
    gscore = []
    for g in range(N_EXPERT_GROUPS):
        v = bi[g * npg:(g + 1) * npg]
        best = None
        for a in range(npg):
            for b in range(a + 1, npg):
                best = v[a] + v[b] if best is None else jnp.maximum(best, v[a] + v[b])
        gscore.append(best)
    gbest = functools.reduce(jnp.maximum, gscore)
    gsel = _first_true([gs == gbest for gs in gscore])
    zero = jnp.zeros_like(gbest)
    bsel = [functools.reduce(lambda x, y: x + y, [jnp.where(gsel[g], bi[g * npg + j], zero)
                                                  for g in range(N_EXPERT_GROUPS)]) for j in range(npg)]
    ssel = [functools.reduce(lambda x, y: x + y, [jnp.where(gsel[g], sc[g * npg + j], zero)
                                                  for g in range(N_EXPERT_GROUPS)]) for j in range(npg)]
    t1 = functools.reduce(jnp.maximum, bsel)
    i1 = _first_true([b == t1 for b in bsel])
    rest = [jnp.where(i1[j], -jnp.inf, bsel[j]) for j in range(npg)]
    t2 = functools.reduce(jnp.maximum, rest)
    i2 = _first_true([rest[j] == t2 for j in range(npg)])
    w1 = functools.reduce(lambda x, y: x + y, [jnp.where(i1[j], ssel[j], zero) for j in range(npg)])
    w2 = functools.reduce(lambda x, y: x + y, [jnp.where(i2[j], ssel[j], zero) for j in range(npg)])
    den = w1 + w2
    local = [jnp.where(i1[j], w1 / den, zero) + jnp.where(i2[j], w2 / den, zero) for j in range(npg)]
    rows = [jnp.where(gsel[e // npg], local[e % npg], zero) for e in range(N_EXPERTS)]
    return jnp.concatenate(rows, axis=0)


def _moe_kernel(x32_ref, xb_ref, gate_ref, wg_ref, wu_ref, wd_ref, wsg_ref, wsu_ref, wsd_ref, lg_ref, lb_ref,
                o32_ref, ob_ref, acc_sc, *, alpha, n_chunks, per_chunk):
    c = pl.program_id(1)
    x = xb_ref[...]

    @pl.when(c == 0)
    def _():
        hs = jax.nn.silu(_dot(x, wsg_ref[...])) * _dot(x, wsu_ref[...])
        acc_sc[...] = _dot(hs.astype(bf16), wsd_ref[...])

    gate = gate_ref[...]
    hid = []
    for e in range(per_chunk):
        h = jax.nn.silu(_dot(x, wg_ref[e])) * _dot(x, wu_ref[e]) * gate[:, e:e + 1]
        hid.append(h.astype(bf16))
    acc_sc[...] += _dot(jnp.concatenate(hid, axis=1), wd_ref[...])

    @pl.when(c == n_chunks - 1)
    def _():
        out = _layer_norm(alpha * x32_ref[...] + acc_sc[...], lg_ref[...], lb_ref[...])
        o32_ref[...] = out
        ob_ref[...] = out.astype(bf16)


def _moe_call(x32, xb, gate_c, wg, wu, wd, wsg, wsu, wsd, lg, lb, alpha, tm=1024, per_chunk=MOE_CHUNK):
    T, D = x32.shape
    n_chunks = N_EXPERTS // per_chunk
    cw = per_chunk * EXPERT_HIDDEN
    row = pl.BlockSpec((tm, D), lambda i, c: (i, 0))
    full = lambda a: pl.BlockSpec(a.shape, lambda i, c: (0, 0))
    return pl.pallas_call(
        functools.partial(_moe_kernel, alpha=alpha, n_chunks=n_chunks, per_chunk=per_chunk),
        grid=(T // tm, n_chunks),
        in_specs=[row, row, pl.BlockSpec((None, tm, per_chunk), lambda i, c: (c, i, 0)),
                  pl.BlockSpec((per_chunk, D, EXPERT_HIDDEN), lambda i, c: (c, 0, 0)),
                  pl.BlockSpec((per_chunk, D, EXPERT_HIDDEN), lambda i, c: (c, 0, 0)),
                  pl.BlockSpec((cw, D), lambda i, c: (c, 0)),
                  full(wsg), full(wsu), full(wsd), full(lg), full(lb)],
        out_specs=[row, row],
        out_shape=[jax.ShapeDtypeStruct((T, D), f32), jax.ShapeDtypeStruct((T, D), bf16)],
        scratch_shapes=[pltpu.VMEM((tm, D), f32)],
        compiler_params=_cparams(("parallel", "arbitrary")), name="moe")(
            x32, xb, gate_c, wg, wu, wd, wsg, wsu, wsd, lg, lb)


def _rope_tables(positions):
    pos = positions.astype(f32).reshape(-1, 1)
    lane = jnp.arange(LANES)

    def tables(inv_freq, idx, rotated, first_half):
        freq = jnp.where(rotated, inv_freq[idx], 0.0)
        ang = pos * freq[None, :]
        sign = jnp.where(first_half, -1.0, 1.0).astype(f32)
        return jnp.cos(ang), jnp.sin(ang) * sign[None, :]

    half = ROT_DIM // 2
    hl = lane % HEAD_DIM
    inv_p = ROPE_THETA ** (-jnp.arange(0, ROT_DIM, 2, dtype=f32) / ROT_DIM)
    cos_p, sin_p = tables(inv_p, hl % half, hl < ROT_DIM, hl < half)
    half = MLA_ROPE_DIM // 2
    rl = lane - MLA_NOPE_DIM
    inv_m = ROPE_THETA ** (-jnp.arange(0, MLA_ROPE_DIM, 2, dtype=f32) / MLA_ROPE_DIM)
    cos_m, sin_m = tables(inv_m, jnp.clip(rl, 0, MLA_ROPE_DIM - 1) % half, (rl >= 0) & (rl < MLA_ROPE_DIM), rl < half)
    return cos_p, sin_p, cos_m, sin_m


def _pack_w_in(w):
    offs = [0]
    for sz in IN_SIZES:
        offs.append(offs[-1] + sz)
    a_q, a_k, a_v, b_cq, b_ckv, b_kr, c_q, c_k, c_v, g = [w[..., offs[i]:offs[i + 1]] for i in range(len(IN_SIZES))]
    z = lambda n: jnp.zeros(w.shape[:-1] + (n,), w.dtype)
    kr_pad = jnp.concatenate([z(MLA_NOPE_DIM), b_kr, z(LANES - MLA_NOPE_DIM - MLA_ROPE_DIM)], axis=-1)
    cols = [a_q, a_k, a_v, b_ckv, kr_pad, b_cq]
    for gi in range(DIL_GROUPS):
        cols += [c_q[..., 256 * gi:256 * (gi + 1)], c_k[..., 256 * gi:256 * (gi + 1)], c_v[..., 512 * gi:512 * (gi + 1)]]
    cols.append(g)
    return jnp.concatenate(cols, axis=-1).astype(bf16)


def _pad_heads(w, n_heads, lo, hi):
    lead = w.shape[:-1]
    per = w.shape[-1] // n_heads
    wh = w.reshape(*lead, n_heads, per)[..., lo:hi]
    wh = jnp.pad(wh, [(0, 0)] * (len(lead) + 1) + [(0, LANES - (hi - lo))])
    return wh.reshape(*lead, n_heads * LANES).astype(bf16)


def kernel(x, positions, ln_in_g, ln_in_b, w_in, b_gate, lam_q1, lam_k1, lam_q2, lam_k2, diff_norm_g, mla_q_norm_g, mla_kv_norm_g, w_mla_qb, w_mla_kvb, w_branch_a, w_branch_b, w_branch_c, w_out, ln1_g, ln1_b, router_w, router_bias, w_exp_gate, w_exp_up, w_exp_down, w_sh_gate, w_sh_up, w_sh_down, ln2_g, ln2_b):
    B, S, D = x.shape
    T = B * S
    depth = w_in.shape[0]
    assert D == D_MODEL and S % (DIL_PATTERNS[-1][1] * 2 * DIL_SIDE) == 0 and S % FLASH_COLS == 0 and S % FLASH_TK == 0
    alpha = (2 * depth) ** 0.25
    cos_p, sin_p, cos_m, sin_m = _rope_tables(positions)
    rw_t = router_w.T.astype(bf16)
    rb = router_bias.reshape(N_EXPERTS, 1).astype(f32)
    vec = lambda v: v.reshape(1, -1).astype(f32)
    per_q = MLA_NOPE_DIM + MLA_ROPE_DIM
    per_kv = MLA_NOPE_DIM + MLA_V_DIM
    w_in_p = _pack_w_in(w_in)
    wq_p = _pad_heads(w_mla_qb, MLA_HEADS, 0, per_q)
    wk_p = _pad_heads(w_mla_kvb, MLA_HEADS, 0, MLA_NOPE_DIM)
    wv_p = _pad_heads(w_mla_kvb, MLA_HEADS, MLA_NOPE_DIM, per_kv)
    wa_b, wb_b, wc_b, wo_b = (w.astype(bf16) for w in (w_branch_a, w_branch_b, w_branch_c, w_out))
    wg_b, wu_b = w_exp_gate.astype(bf16), w_exp_up.astype(bf16)
    wd_b = w_exp_down.reshape(depth, N_EXPERTS * EXPERT_HIDDEN, D).astype(bf16)
    wsg_b, wsu_b, wsd_b = (w.astype(bf16) for w in (w_sh_gate, w_sh_up, w_sh_down))
    for l in range(depth):
        lam_init = 0.8 - 0.6 * math.exp(-0.3 * l)
        proj = _inproj_call(
            x.reshape(T, D) if l == 0 else xb, w_in_p[l], vec(b_gate[l]), cos_p, sin_p, cos_m, sin_m,
            vec(mla_q_norm_g[l]), vec(mla_kv_norm_g[l]), wq_p[l], wk_p[l], wv_p[l],
            B, S, ln=(vec(ln_in_g), vec(ln_in_b)) if l == 0 else None)
        ha, qb, kb, vb, hc0, hc1, hc2, hg = proj[:8]
        if l == 0:
            x32, xb = proj[8:]
        lam_vec = jnp.pad(jnp.stack([lam_q1[l], lam_k1[l], lam_q2[l], lam_k2[l]]).astype(f32),
                          ((0, 4), (0, LANES - HEAD_DIM)))
        ha3 = ha.reshape(B, S, HA_OUT_W)
        oa = _flash_call(ha3, ha3, ha3, DIFF_HEADS, 0, DIFF_HEADS, DIFF_HEADS, n_maps=2, tq=FLASH_COLS // 2, tk=FLASH_TK,
                         name="diff_attn", lam_vec=lam_vec, norm_g=diff_norm_g[l].reshape(-1, 1).astype(f32),
                         lam_init=lam_init, vw=2 * LANES)
        ob = _flash_call(qb.reshape(B, S, -1), kb.reshape(B, S, -1), vb.reshape(B, S, -1), MLA_HEADS, 0, 0, 0,
                         n_maps=1, tq=FLASH_COLS, tk=FLASH_TK, name="mla_attn", dv=MLA_V_DIM)
        dil = [_dil_call(hc) for hc in (hc0, hc1, hc2)]
        x32, xb, gate_t = _merge_call(
            x32, oa.reshape(T, -1), ob.reshape(T, -1), [o for o, _ in dil], [lse for _, lse in dil],
            hg, wa_b[l], wb_b[l], wc_b[l], wo_b[l], vec(ln1_g[l]), vec(ln1_b[l]), rw_t, rb, alpha, B, S)
        gate_c = gate_t.reshape(N_EXPERTS // MOE_CHUNK, MOE_CHUNK, T).transpose(0, 2, 1)
        x32, xb = _moe_call(x32, xb, gate_c, wg_b[l], wu_b[l], wd_b[l], wsg_b[l], wsu_b[l], wsd_b[l],
                            vec(ln2_g[l]), vec(ln2_b[l]), alpha)
    return x32.reshape(B, S, D)
```

```python
import functools
import math

import jax
import jax.numpy as jnp
from jax import lax
from jax.experimental import pallas as pl
from jax.experimental.pallas import tpu as pltpu

f32 = jnp.float32
bf16 = jnp.bfloat16

D_MODEL = 1024
HEAD_DIM = 64
ROPE_THETA = 500000.0
ROT_DIM = HEAD_DIM // 4
DIFF_HEADS = 4
MLA_HEADS = 8
MLA_Q_RANK = 384
MLA_KV_RANK = 256
MLA_NOPE_DIM = 64
MLA_ROPE_DIM = 32
MLA_V_DIM = 64
DIL_PATTERNS = ((128, 1), (512, 4), (2048, 16))
DIL_GROUPS = 3
DIL_HEADS = 4
DIL_SIDE = 64
N_EXPERTS = 16
N_EXPERT_GROUPS = 4
EXPERTS_PER_GROUP = 4
EXPERT_HIDDEN = 256
MOE_CHUNK = EXPERTS_PER_GROUP
IN_SIZES = (512, 512, 512, 384, 256, 32, 768, 768, 1536, 3072)
NEG_INF = -1e30
LOG2E = math.log2(math.e)

LANES = 128
VMEM_LIMIT = 52 * 1024 * 1024

HA_W = 1536
HA_OUT_W = 2048
HB_W = 768
HCG_W = 1024
HC_W = 3 * HCG_W
HG_W = 3072

FLASH_TK = 512
FLASH_COLS = 1024
FLASH_UNROLL = 8
SUM_ROWS = 64


def _cparams(sem):
    return pltpu.CompilerParams(dimension_semantics=sem, vmem_limit_bytes=VMEM_LIMIT)


def _layer_norm(z, g, b):
    mu = jnp.mean(z, axis=-1, keepdims=True)
    zc = z - mu
    var = jnp.mean(zc * zc, axis=-1, keepdims=True)
    return zc * lax.rsqrt(var + 1e-5) * g + b


def _rms_norm(z, g):
    return z * lax.rsqrt(jnp.mean(z * z, axis=-1, keepdims=True) + 1e-6) * g


def _dot(a, b):
    return jnp.dot(a, b, preferred_element_type=f32)


def _dot_nt(a, b):
    return lax.dot_general(a, b, (((1,), (1,)), ((), ())), preferred_element_type=f32)


def _rope_tile(xt, cos_t, sin_t, first_half, shift):
    xr = jnp.where(first_half, pltpu.roll(xt, LANES - shift, 1), pltpu.roll(xt, shift, 1))
    return xt * cos_t + xr * sin_t


def _inproj_plan():
    qscale = HEAD_DIM ** -0.5 * LOG2E
    plan = []
    for c in range(0, 512, 256):
        plan.append((c, 256, "rope", 0, c, qscale))
    for c in range(512, 1024, 256):
        plan.append((c, 256, "rope", 0, c, 1.0))
    plan.append((1024, 512, "value", 0, 1024, 1.0))
    plan.append((HA_W, HB_W, "mla", 1, 0, 1.0))
    base = HA_W + HB_W
    for g in range(DIL_GROUPS):
        o = HCG_W * g
        plan.append((base + o, 256, "rope", 2 + g, 0, qscale))
        plan.append((base + o + 256, 256, "rope", 2 + g, 256, 1.0))
        plan.append((base + o + 512, 512, "plain", 2 + g, 512, 1.0))
    base = HA_W + HB_W + HC_W
    for c in range(0, HG_W, 512):
        plan.append((base + c, 512, "gate", 2 + DIL_GROUPS, c, 1.0))
    return plan


def _mla_up_projections(acc, lane, cm_ref, sm_ref, qg_ref, kvg_ref, wq_ref, wk_ref, wv_ref, qb_ref, kb_ref, vb_ref):
    cos_t, sin_t = cm_ref[...], sm_ref[...]
    first_half = lane < MLA_NOPE_DIM + MLA_ROPE_DIM // 2
    shift = MLA_ROPE_DIM // 2
    scale = (MLA_NOPE_DIM + MLA_ROPE_DIM) ** -0.5 * LOG2E
    cn = _rms_norm(acc[:, 0:MLA_KV_RANK], kvg_ref[...]).astype(bf16)
    kr = _rope_tile(acc[:, MLA_KV_RANK:MLA_KV_RANK + LANES], cos_t, sin_t, first_half, shift)
    qn = _rms_norm(acc[:, MLA_KV_RANK + LANES:], qg_ref[...]).astype(bf16)
    q_all = _dot(qn, wq_ref[...])
    k_all = _dot(cn, wk_ref[...])
    v_all = _dot(cn, wv_ref[...])
    for h in range(MLA_HEADS):
        sl = slice(h * LANES, (h + 1) * LANES)
        q = _rope_tile(q_all[:, sl], cos_t, sin_t, first_half, shift) * scale
        qb_ref[:, sl] = q.astype(qb_ref.dtype)
        kb_ref[:, sl] = (k_all[:, sl] + kr).astype(kb_ref.dtype)
        vb_ref[:, sl] = jnp.where(lane == MLA_V_DIM, 1.0, v_all[:, sl]).astype(vb_ref.dtype)


def _inproj_kernel(x_ref, w_ref, bg_ref, cp_ref, sp_ref, cm_ref, sm_ref, qg_ref, kvg_ref, wq_ref, wk_ref, wv_ref,
                   *rest, ln_first):
    if ln_first:
        lng_ref, lnb_ref, ha_ref, qb_ref, kb_ref, vb_ref, hc0_ref, hc1_ref, hc2_ref, hg_ref, x32_ref, xb_ref, stage = rest
        y = _layer_norm(x_ref[...], lng_ref[...], lnb_ref[...])
        x32_ref[...] = y
        x = y.astype(bf16)
        xb_ref[...] = x
    else:
        ha_ref, qb_ref, kb_ref, vb_ref, hc0_ref, hc1_ref, hc2_ref, hg_ref, stage = rest
        x = x_ref[...]
    outs = (ha_ref, None, hc0_ref, hc1_ref, hc2_ref, hg_ref)
    tm = x.shape[0]
    cos_t = cp_ref[...]
    sin_t = sp_ref[...]
    lane = lax.broadcasted_iota(jnp.int32, cos_t.shape, 1)
    first_half = (lane % HEAD_DIM) < (ROT_DIM // 2)
    for (c0, w, kind, oi, o0, scale) in _inproj_plan():
        acc = _dot(x, w_ref[:, c0:c0 + w])
        if kind == "rope":
            tiles = []
            for t in range(w // LANES):
                y = _rope_tile(acc[:, t * LANES:(t + 1) * LANES], cos_t, sin_t, first_half, ROT_DIM // 2)
                tiles.append(y * scale if scale != 1.0 else y)
            acc = jnp.concatenate(tiles, axis=1)
        elif kind == "gate":
            acc = jax.nn.sigmoid(acc + bg_ref[:, o0:o0 + w])
        if kind == "mla":
            _mla_up_projections(acc, lane, cm_ref, sm_ref, qg_ref, kvg_ref, wq_ref, wk_ref, wv_ref,
                                qb_ref, kb_ref, vb_ref)
            continue
        out = outs[oi]
        if kind == "value":
            ones_tile = (lane == 0).astype(out.dtype)
            for h in range(w // LANES):
                out[:, o0 + 2 * h * LANES:o0 + (2 * h + 1) * LANES] = acc[:, h * LANES:(h + 1) * LANES].astype(out.dtype)
                out[:, o0 + (2 * h + 1) * LANES:o0 + (2 * h + 2) * LANES] = ones_tile
        elif 2 <= oi < 2 + DIL_GROUPS:
            d = DIL_PATTERNS[oi - 2][1]
            if d == 1:
                out[0, :, o0:o0 + w] = acc.astype(out.dtype)
            else:
                nt = w // LANES
                for t in range(nt):
                    stage[t] = acc[:, t * LANES:(t + 1) * LANES]
                for r in range(d):
                    rows = [stage[t, pl.ds(r, tm // d, stride=d), :] for t in range(nt)]
                    out[r, :, o0:o0 + w] = jnp.concatenate(rows, axis=1).astype(out.dtype)
        else:
            out[:, o0:o0 + w] = acc.astype(out.dtype)


def _inproj_call(x, w, bg, cos_t, sin_t, cos_m, sin_m, qg, kvg, wq, wk, wv, B, S, ln=None, tm=256):
    T, D = x.shape
    NW = w.shape[1]
    nb = S // tm
    MW = MLA_HEADS * LANES
    row = lambda width: pl.BlockSpec((tm, width), lambda b, i: (b * nb + i, 0))
    const = lambda shape: pl.BlockSpec(shape, lambda b, i: (0, 0))
    consts = [qg, kvg, wq, wk, wv] + (list(ln) if ln else [])
    extra_specs = [row(D), row(D)] if ln else []
    extra_shapes = [jax.ShapeDtypeStruct((T, D), f32), jax.ShapeDtypeStruct((T, D), bf16)] if ln else []
    hc_specs, hc_shapes = [], []
    for _, d in DIL_PATTERNS:
        hc_specs.append(pl.BlockSpec((None, d, tm // d, HCG_W), lambda b, i: (b, 0, i, 0)))
        hc_shapes.append(jax.ShapeDtypeStruct((B, d, S // d, HCG_W), bf16))
    return pl.pallas_call(
        functools.partial(_inproj_kernel, ln_first=bool(ln)), grid=(B, nb),
        in_specs=[row(D), const((D, NW)), const((1, HG_W)), row(LANES), row(LANES), row(LANES), row(LANES)]
        + [const(a.shape) for a in consts],
        out_specs=[row(HA_OUT_W), row(MW), row(MW), row(MW)] + hc_specs + [row(HG_W)] + extra_specs,
        out_shape=[jax.ShapeDtypeStruct((T, HA_OUT_W), bf16)] + [jax.ShapeDtypeStruct((T, MW), bf16)] * 3 + hc_shapes
        + [jax.ShapeDtypeStruct((T, HG_W), bf16)] + extra_shapes,
        scratch_shapes=[pltpu.VMEM((4, tm, LANES), f32)],
        compiler_params=_cparams(("parallel", "parallel")), name="inproj")(
            x, w, bg, cos_t, sin_t, cos_m, sin_m, *consts)


def _flash_kernel(*refs, n_maps, tq, tk, n_kv, lam_init, dv):
    if n_maps == 2:
        q_ref, qn_ref, k_ref, v_ref, lam_ref, g_ref, o_ref, qt_sc, m_sc, acc_sc, s_sc, mc_sc = refs
    else:
        q_ref, qn_ref, k_ref, v_ref, o_ref, qt_sc, m_sc, acc_sc, s_sc, mc_sc = refs
    i = pl.program_id(2)
    cur = i % 2
    nxt = 1 - cur

    def load_queries(ref, slot):
        q = ref[...].astype(f32)
        if n_maps == 2:
            lane = lax.broadcasted_iota(jnp.int32, q.shape, 1)
            zero = jnp.zeros_like(q)
            q = jnp.concatenate([jnp.where(lane < HEAD_DIM, q, zero), jnp.where(lane >= HEAD_DIM, q, zero)], axis=0)
        qt_sc[slot] = q.T.astype(bf16)

    def scores(j, buf, slot):
        off = pl.multiple_of(j * tk, tk)
        s = _dot(k_ref[pl.ds(off, tk), :], qt_sc[slot])
        s_sc[buf] = s
        mc_sc[buf] = jnp.max(s, axis=0, keepdims=True)

    def accumulate(j, buf):
        off = pl.multiple_of(j * tk, tk)
        m_prev = m_sc[...]
        m_new = jnp.maximum(m_prev, mc_sc[buf])
        p = jnp.exp2(s_sc[buf] - m_new)
        a = jnp.exp2(m_prev - m_new)
        pv = lax.dot_general(v_ref[pl.ds(off, tk), 0:dv + SUM_ROWS], p.astype(bf16), (((0,), (0,)), ((), ())),
                             preferred_element_type=f32)
        acc_sc[...] = a * acc_sc[...] + pv
        m_sc[...] = m_new

    @pl.when(i == 0)
    def _():
        load_queries(q_ref, cur)
        scores(0, 0, cur)

    load_queries(qn_ref, nxt)
    m_sc[...] = jnp.full(m_sc.shape, -jnp.inf, f32)
    acc_sc[...] = jnp.zeros(acc_sc.shape, f32)

    unroll = min(FLASH_UNROLL, n_kv)
    assert unroll % 2 == 0 and n_kv % unroll == 0

    def body(jj, carry):
        for u in range(unroll):
            j = unroll * jj + u
            scores(j + 1, (u + 1) % 2, cur)
            accumulate(j, u % 2)
        return carry

    lax.fori_loop(0, n_kv // unroll - 1, body, 0)
    for j in range(n_kv - unroll, n_kv):
        if j + 1 < n_kv:
            scores(j + 1, (j + 1) % 2, cur)
        else:
            scores(0, 0, nxt)
        accumulate(j, j % 2)
    o = acc_sc[0:dv, :] / acc_sc[dv:dv + 1, :]
    if n_maps == 2:
        lv = lam_ref[...]
        lam = (jnp.exp(jnp.sum(lv[0:1] * lv[1:2], axis=-1, keepdims=True))
               - jnp.exp(jnp.sum(lv[2:3] * lv[3:4], axis=-1, keepdims=True)) + lam_init)
        o = o[:, :tq] - lam * o[:, tq:]
        ms = jnp.mean(o * o, axis=0, keepdims=True)
        o = o * lax.rsqrt(ms + 1e-6) * g_ref[...] * (1.0 - lam_init)
    if dv < LANES:
        o = jnp.concatenate([o, jnp.zeros((LANES - dv, o.shape[1]), f32)], axis=0)
    o_ref[...] = o.T.astype(o_ref.dtype)


def _flash_call(q_arr, k_arr, v_arr, n_heads, q_blk0, k_blk0, v_blk0, n_maps, tq, tk, name,
                lam_vec=None, norm_g=None, lam_init=0.0, dv=LANES, vw=LANES):
    B, S, _ = q_arr.shape
    cols = n_maps * tq
    last = S // tq - 1
    assert dv + SUM_ROWS <= vw
    in_specs = [pl.BlockSpec((None, tq, LANES), lambda b, h, i: (b, i, q_blk0 + h)),
                pl.BlockSpec((None, tq, LANES), lambda b, h, i: (b, jnp.minimum(i + 1, last), q_blk0 + h)),
                pl.BlockSpec((None, S, LANES), lambda b, h, i: (b, 0, k_blk0 + h)),
                pl.BlockSpec((None, S, vw), lambda b, h, i: (b, 0, v_blk0 + h))]
    args = [q_arr, q_arr, k_arr, v_arr]
    if n_maps == 2:
        in_specs += [pl.BlockSpec((8, LANES), lambda b, h, i: (0, 0)),
                     pl.BlockSpec((LANES, 1), lambda b, h, i: (0, 0))]
        args += [lam_vec, norm_g]
    kern = functools.partial(_flash_kernel, n_maps=n_maps, tq=tq, tk=tk, n_kv=S // tk, lam_init=lam_init,
                             dv=dv)
    return pl.pallas_call(
        kern, grid=(B, n_heads, S // tq), in_specs=in_specs,
        out_specs=pl.BlockSpec((None, tq, LANES), lambda b, h, i: (b, i, h)),
        out_shape=jax.ShapeDtypeStruct((B, S, n_heads * LANES), bf16),
        scratch_shapes=[pltpu.VMEM((2, LANES, cols), bf16), pltpu.VMEM((1, cols), f32),
                        pltpu.VMEM((dv + SUM_ROWS, cols), f32), pltpu.VMEM((2, tk, cols), f32),
                        pltpu.VMEM((2, 1, cols), f32)],
        compiler_params=_cparams(("parallel", "parallel", "arbitrary")), name=name)(*args)


def _dil_kernel(q_ref, kp_ref, kc_ref, kn_ref, vp_ref, vc_ref, vn_ref, o_ref, lse_ref, kcat, vcat, *, tq, sub_len):
    i = pl.program_id(2)
    side = DIL_SIDE
    kcat[0:side, :] = kp_ref[...]
    kcat[side:side + tq, :] = kc_ref[...]
    kcat[side + tq:, :] = kn_ref[...]
    vcat[0:side, :] = vp_ref[...]
    vcat[side:side + tq, :] = vc_ref[...]
    vcat[side + tq:, :] = vn_ref[...]
    qs, ks = 2 * side, 4 * side
    lane = lax.broadcasted_iota(jnp.int32, (qs, LANES), 1)
    ii = lax.broadcasted_iota(jnp.int32, (qs, ks), 0)
    jj = lax.broadcasted_iota(jnp.int32, (qs, ks), 1)
    band = (jj - ii >= 0) & (jj - ii <= 2 * side)
    for sb in range(tq // qs):
        q0 = sb * qs
        pos = jj + (i * tq + q0 - side)
        mask = band & (pos >= 0) & (pos < sub_len)
        lse_tile = jnp.zeros((qs, LANES), f32)
        for h in range(DIL_HEADS):
            t = h // 2
            qt = q_ref[q0:q0 + qs, t * LANES:(t + 1) * LANES]
            keep = (lane < HEAD_DIM) if h % 2 == 0 else (lane >= HEAD_DIM)
            qm = jnp.where(keep, qt, jnp.zeros_like(qt))
            s = _dot_nt(qm, kcat[q0:q0 + ks, t * LANES:(t + 1) * LANES])
            s = jnp.where(mask, s, NEG_INF)
            m = jnp.max(s, axis=-1, keepdims=True)
            p = jnp.exp2(s - m)
            l = jnp.sum(p, axis=-1, keepdims=True)
            o = _dot(p.astype(bf16), vcat[q0:q0 + ks, h * LANES:(h + 1) * LANES]) / l
            o_ref[q0:q0 + qs, h * LANES:(h + 1) * LANES] = o.astype(o_ref.dtype)
            lse_tile = jnp.where(lane == h, m + jnp.log2(l), lse_tile)
        lse_ref[q0:q0 + qs, :] = lse_tile


def _dil_call(hc):
    B, d, L, _ = hc.shape
    tq = min(1024, L)
    side = DIL_SIDE
    nblk = tq // side
    last = L // side - 1
    qw, vw = 2 * LANES, 4 * LANES
    prev = lambda i: jnp.maximum(i * nblk - 1, 0)
    nxt = lambda i: jnp.minimum((i + 1) * nblk, last)
    in_specs = [
        pl.BlockSpec((None, None, tq, qw), lambda b, r, i: (b, r, i, 0)),
        pl.BlockSpec((None, None, side, qw), lambda b, r, i: (b, r, prev(i), 1)),
        pl.BlockSpec((None, None, tq, qw), lambda b, r, i: (b, r, i, 1)),
        pl.BlockSpec((None, None, side, qw), lambda b, r, i: (b, r, nxt(i), 1)),
        pl.BlockSpec((None, None, side, vw), lambda b, r, i: (b, r, prev(i), 1)),
        pl.BlockSpec((None, None, tq, vw), lambda b, r, i: (b, r, i, 1)),
        pl.BlockSpec((None, None, side, vw), lambda b, r, i: (b, r, nxt(i), 1)),
    ]
    return pl.pallas_call(
        functools.partial(_dil_kernel, tq=tq, sub_len=L), grid=(B, d, L // tq), in_specs=in_specs,
        out_specs=[pl.BlockSpec((None, None, tq, vw), lambda b, r, i: (b, r, i, 0)),
                   pl.BlockSpec((None, None, tq, LANES), lambda b, r, i: (b, r, i, 0))],
        out_shape=[jax.ShapeDtypeStruct((B, d, L, vw), bf16), jax.ShapeDtypeStruct((B, d, L, LANES), f32)],
        scratch_shapes=[pltpu.VMEM((tq + 2 * side, qw), bf16), pltpu.VMEM((tq + 2 * side, vw), bf16)],
        compiler_params=_cparams(("parallel", "parallel", "arbitrary")),
        name=f"dilated_d{d}")(hc, hc, hc, hc, hc, hc, hc)


def _token_order(ref, stage):
    d, n, w = ref.shape
    if d == 1:
        return ref[0].astype(f32)
    nt = w // LANES
    for r in range(d):
        v = ref[r].astype(f32)
        for t in range(nt):
            stage[t, pl.ds(r, n, stride=d), :] = v[:, t * LANES:(t + 1) * LANES]
    return jnp.concatenate([stage[t] for t in range(nt)], axis=1)


def _merge_kernel(x_ref, oa_ref, ob_ref, o1_ref, o2_ref, o3_ref, l1_ref, l2_ref, l3_ref, g_ref,
                  wa_ref, wb_ref, wc_ref, wo_ref, lg_ref, lb_ref, rw_ref, rb_ref, x32_ref, xb_ref, gate_ref,
                  so2, so3, sl2, sl3, *, alpha):
    l1, l2, l3 = _token_order(l1_ref, None), _token_order(l2_ref, sl2), _token_order(l3_ref, sl3)
    o1, o2, o3 = _token_order(o1_ref, None), _token_order(o2_ref, so2), _token_order(o3_ref, so3)
    mx = jnp.maximum(jnp.maximum(l1, l2), l3)
    e1, e2, e3 = jnp.exp2(l1 - mx), jnp.exp2(l2 - mx), jnp.exp2(l3 - mx)
    den = e1 + e2 + e3
    w1, w2, w3 = e1 / den, e2 / den, e3 / den
    tiles = []
    for h in range(DIL_HEADS):
        sl = slice(h * LANES, (h + 1) * LANES)
        oc = w1[:, h:h + 1] * o1[:, sl] + w2[:, h:h + 1] * o2[:, sl] + w3[:, h:h + 1] * o3[:, sl]
        tiles.append(oc.astype(bf16))
    oc = jnp.concatenate(tiles, axis=1)
    obf = ob_ref[...].astype(f32)
    half = LANES // 2
    ob = jnp.concatenate([obf[:, 2 * j * LANES:(2 * j + 1) * LANES]
                          + pltpu.roll(obf[:, (2 * j + 1) * LANES:(2 * j + 2) * LANES], half, 1)
                          for j in range(MLA_HEADS // 2)], axis=1).astype(bf16)
    D = D_MODEL
    y = (g_ref[:, 0:D].astype(f32) * _dot(oa_ref[...], wa_ref[...])
         + g_ref[:, D:2 * D].astype(f32) * _dot(ob, wb_ref[...])
         + g_ref[:, 2 * D:3 * D].astype(f32) * _dot(oc, wc_ref[...]))
    mix = _dot(y.astype(bf16), wo_ref[...])
    out = _layer_norm(alpha * x_ref[...] + mix, lg_ref[...], lb_ref[...])
    x32_ref[...] = out
    xb = out.astype(bf16)
    xb_ref[...] = xb
    gate_ref[...] = _route(_dot_nt(rw_ref[...], xb), rb_ref[...])


def _merge_call(x32, oa, ob, dil_o, dil_l, hg, wa, wb, wc, wo, lg, lb, rw_t, rb, alpha, B, S, tm=512):
    T, D = x32.shape
    nb = S // tm
    row = lambda a: pl.BlockSpec((tm, a.shape[1]), lambda b, i: (b * nb + i, 0))
    cls = lambda a: pl.BlockSpec((None, a.shape[1], tm // a.shape[1], a.shape[3]), lambda b, i: (b, 0, i, 0))
    full = lambda a: pl.BlockSpec(a.shape, lambda b, i: (0, 0))
    consts = [wa, wb, wc, wo, lg, lb, rw_t, rb]
    ow, lw = dil_o[0].shape[3], dil_l[0].shape[3]
    return pl.pallas_call(
        functools.partial(_merge_kernel, alpha=alpha), grid=(B, nb),
        in_specs=[row(x32), row(oa), row(ob)] + [cls(a) for a in dil_o] + [cls(a) for a in dil_l] + [row(hg)]
        + [full(a) for a in consts],
        out_specs=[pl.BlockSpec((tm, D), lambda b, i: (b * nb + i, 0))] * 2
        + [pl.BlockSpec((N_EXPERTS, tm), lambda b, i: (0, b * nb + i))],
        out_shape=[jax.ShapeDtypeStruct((T, D), f32), jax.ShapeDtypeStruct((T, D), bf16),
                   jax.ShapeDtypeStruct((N_EXPERTS, T), f32)],
        scratch_shapes=[pltpu.VMEM((ow // LANES, tm, LANES), f32), pltpu.VMEM((ow // LANES, tm, LANES), f32),
                        pltpu.VMEM((lw // LANES, tm, LANES), f32), pltpu.VMEM((lw // LANES, tm, LANES), f32)],
        compiler_params=_cparams(("parallel", "parallel")), name="merge")(
            x32, oa, ob, *dil_o, *dil_l, hg, *consts)


def _first_true(flags):
    out, seen = [], None
    for f in flags:
        out.append(f if seen is None else f & ~seen)
        seen = f if seen is None else seen | f
    return out


def _route(logits, bias):
    scores = jax.nn.sigmoid(logits)
    biased = scores + bias
    sc = [scores[e:e + 1, :] for e in range(N_EXPERTS)]
    bi = [biased[e:e + 1, :] for e in range(N_EXPERTS)]
    npg = EXPERTS_PER_GROUP
    gscore = []
    for g in range(N_EXPERT_GROUPS):
        v = bi[g * npg:(g + 1) * npg]
        best = None
        for a in range(npg):
            for b in range(a + 1, npg):
                best = v[a] + v[b] if best is None else jnp.maximum(best, v[a] + v[b])
        gscore.append(best)
    gbest = functools.reduce(jnp.maximum, gscore)
    gsel = _first_true([gs == gbest for gs in gscore])
    zero = jnp.zeros_like(gbest)
    bsel = [functools.reduce(lambda x, y: x + y, [jnp.where(gsel[g], bi[g * npg + j], zero)
                                                  for g in range(N_EXPERT_GROUPS)]) for j in range(npg)]
    ssel = [functools.reduce(lambda x, y: x + y, [jnp.where(gsel[g], sc[g * npg + j], zero)
                                                  for g in range(N_EXPERT_GROUPS)]) for j in range(npg)]
    t1 = functools.reduce(jnp.maximum, bsel)
    i1 = _first_true([b == t1 for b in bsel])
    rest = [jnp.where(i1[j], -jnp.inf, bsel[j]) for j in range(npg)]
    t2 = functools.reduce(jnp.maximum, rest)
    i2 = _first_true([rest[j] == t2 for j in range(npg)])
    w1 = functools.reduce(lambda x, y: x + y, [jnp.where(i1[j], ssel[j], zero) for j in range(npg)])
    w2 = functools.reduce(lambda x, y: x + y, [jnp.where(i2[j], ssel[j], zero) for j in range(npg)])
    den = w1 + w2
    local = [jnp.where(i1[j], w1 / den, zero) + jnp.where(i2[j], w2 / den, zero) for j in range(npg)]
    rows = [jnp.where(gsel[e // npg], local[e % npg], zero) for e in range(N_EXPERTS)]
    return jnp.concatenate(rows, axis=0)


def _moe_kernel(x32_ref, xb_ref, gate_ref, wg_ref, wu_ref, wd_ref, wsg_ref, wsu_ref, wsd_ref, lg_ref, lb_ref,
                o32_ref, ob_ref, acc_sc, *, alpha, n_chunks, per_chunk):
    c = pl.program_id(1)
    x = xb_ref[...]

    @pl.when(c == 0)
    def _():
        hs = jax.nn.silu(_dot(x, wsg_ref[...])) * _dot(x, wsu_ref[...])
        acc_sc[...] = _dot(hs.astype(bf16), wsd_ref[...])

    gate = gate_ref[...]
    hid = []
    for e in range(per_chunk):
        h = jax.nn.silu(_dot(x, wg_ref[e])) * _dot(x, wu_ref[e]) * gate[:, e:e + 1]
        hid.append(h.astype(bf16))
    acc_sc[...] += _dot(jnp.concatenate(hid, axis=1), wd_ref[...])

    @pl.when(c == n_chunks - 1)
    def _():
        out = _layer_norm(alpha * x32_ref[...] + acc_sc[...], lg_ref[...], lb_ref[...])
        o32_ref[...] = out
        ob_ref[...] = out.astype(bf16)


def _moe_call(x32, xb, gate_c, wg, wu, wd, wsg, wsu, wsd, lg, lb, alpha, tm=1024, per_chunk=MOE_CHUNK):
    T, D = x32.shape
    n_chunks = N_EXPERTS // per_chunk
    cw = per_chunk * EXPERT_HIDDEN
    row = pl.BlockSpec((tm, D), lambda i, c: (i, 0))
    full = lambda a: pl.BlockSpec(a.shape, lambda i, c: (0, 0))
    return pl.pallas_call(
        functools.partial(_moe_kernel, alpha=alpha, n_chunks=n_chunks, per_chunk=per_chunk),
        grid=(T // tm, n_chunks),
        in_specs=[row, row, pl.BlockSpec((None, tm, per_chunk), lambda i, c: (c, i, 0)),
                  pl.BlockSpec((per_chunk, D, EXPERT_HIDDEN), lambda i, c: (c, 0, 0)),
                  pl.BlockSpec((per_chunk, D, EXPERT_HIDDEN), lambda i, c: (c, 0, 0)),
                  pl.BlockSpec((cw, D), lambda i, c: (c, 0)),
                  full(wsg), full(wsu), full(wsd), full(lg), full(lb)],
        out_specs=[row, row],
        out_shape=[jax.ShapeDtypeStruct((T, D), f32), jax.ShapeDtypeStruct((T, D), bf16)],
        scratch_shapes=[pltpu.VMEM((tm, D), f32)],
        compiler_params=_cparams(("parallel", "arbitrary")), name="moe")(
            x32, xb, gate_c, wg, wu, wd, wsg, wsu, wsd, lg, lb)


def _rope_tables(positions):
    pos = positions.astype(f32).reshape(-1, 1)
    lane = jnp.arange(LANES)

    def tables(inv_freq, idx, rotated, first_half):
        freq = jnp.where(rotated, inv_freq[idx], 0.0)
        ang = pos * freq[None, :]
        sign = jnp.where(first_half, -1.0, 1.0).astype(f32)
        return jnp.cos(ang), jnp.sin(ang) * sign[None, :]

    half = ROT_DIM // 2
    hl = lane % HEAD_DIM
    inv_p = ROPE_THETA ** (-jnp.arange(0, ROT_DIM, 2, dtype=f32) / ROT_DIM)
    cos_p, sin_p = tables(inv_p, hl % half, hl < ROT_DIM, hl < half)
    half = MLA_ROPE_DIM // 2
    rl = lane - MLA_NOPE_DIM
    inv_m = ROPE_THETA ** (-jnp.arange(0, MLA_ROPE_DIM, 2, dtype=f32) / MLA_ROPE_DIM)
    cos_m, sin_m = tables(inv_m, jnp.clip(rl, 0, MLA_ROPE_DIM - 1) % half, (rl >= 0) & (rl < MLA_ROPE_DIM), rl < half)
    return cos_p, sin_p, cos_m, sin_m


def _pack_w_in(w):
    offs = [0]
    for sz in IN_SIZES:
        offs.append(offs[-1] + sz)
    a_q, a_k, a_v, b_cq, b_ckv, b_kr, c_q, c_k, c_v, g = [w[..., offs[i]:offs[i + 1]] for i in range(len(IN_SIZES))]
    z = lambda n: jnp.zeros(w.shape[:-1] + (n,), w.dtype)
    kr_pad = jnp.concatenate([z(MLA_NOPE_DIM), b_kr, z(LANES - MLA_NOPE_DIM - MLA_ROPE_DIM)], axis=-1)
    cols = [a_q, a_k, a_v, b_ckv, kr_pad, b_cq]
    for gi in range(DIL_GROUPS):
        cols += [c_q[..., 256 * gi:256 * (gi + 1)], c_k[..., 256 * gi:256 * (gi + 1)], c_v[..., 512 * gi:512 * (gi + 1)]]
    cols.append(g)
    return jnp.concatenate(cols, axis=-1).astype(bf16)


def _pad_heads(w, n_heads, lo, hi):
    lead = w.shape[:-1]
    per = w.shape[-1] // n_heads
    wh = w.reshape(*lead, n_heads, per)[..., lo:hi]
    wh = jnp.pad(wh, [(0, 0)] * (len(lead) + 1) + [(0, LANES - (hi - lo))])
    return wh.reshape(*lead, n_heads * LANES).astype(bf16)


def kernel(x, positions, ln_in_g, ln_in_b, w_in, b_gate, lam_q1, lam_k1, lam_q2, lam_k2, diff_norm_g, mla_q_norm_g, mla_kv_norm_g, w_mla_qb, w_mla_kvb, w_branch_a, w_branch_b, w_branch_c, w_out, ln1_g, ln1_b, router_w, router_bias, w_exp_gate, w_exp_up, w_exp_down, w_sh_gate, w_sh_up, w_sh_down, ln2_g, ln2_b):
    B, S, D = x.shape
    T = B * S
    depth = w_in.shape[0]
    assert D == D_MODEL and S % (DIL_PATTERNS[-1][1] * 2 * DIL_SIDE) == 0 and S % FLASH_COLS == 0 and S % FLASH_TK == 0
    alpha = (2 * depth) ** 0.25
    cos_p, sin_p, cos_m, sin_m = _rope_tables(positions)
    rw_t = router_w.T.astype(bf16)
    rb = router_bias.reshape(N_EXPERTS, 1).astype(f32)
    vec = lambda v: v.reshape(1, -1).astype(f32)
    per_q = MLA_NOPE_DIM + MLA_ROPE_DIM
    per_kv = MLA_NOPE_DIM + MLA_V_DIM
    w_in_p = _pack_w_in(w_in)
    wq_p = _pad_heads(w_mla_qb, MLA_HEADS, 0, per_q)
    wk_p = _pad_heads(w_mla_kvb, MLA_HEADS, 0, MLA_NOPE_DIM)
    wv_p = _pad_heads(w_mla_kvb, MLA_HEADS, MLA_NOPE_DIM, per_kv)
    wa_b, wb_b, wc_b, wo_b = (w.astype(bf16) for w in (w_branch_a, w_branch_b, w_branch_c, w_out))
    wg_b, wu_b = w_exp_gate.astype(bf16), w_exp_up.astype(bf16)
    wd_b = w_exp_down.reshape(depth, N_EXPERTS * EXPERT_HIDDEN, D).astype(bf16)
    wsg_b, wsu_b, wsd_b = (w.astype(bf16) for w in (w_sh_gate, w_sh_up, w_sh_down))
    for l in range(depth):
        lam_init = 0.8 - 0.6 * math.exp(-0.3 * l)
        proj = _inproj_call(
            x.reshape(T, D) if l == 0 else xb, w_in_p[l], vec(b_gate[l]), cos_p, sin_p, cos_m, sin_m,
            vec(mla_q_norm_g[l]), vec(mla_kv_norm_g[l]), wq_p[l], wk_p[l], wv_p[l],
            B, S, ln=(vec(ln_in_g), vec(ln_in_b)) if l == 0 else None)
        ha, qb, kb, vb, hc0, hc1, hc2, hg = proj[:8]
        if l == 0:
            x32, xb = proj[8:]
        lam_vec = jnp.pad(jnp.stack([lam_q1[l], lam_k1[l], lam_q2[l], lam_k2[l]]).astype(f32),
                          ((0, 4), (0, LANES - HEAD_DIM)))
        ha3 = ha.reshape(B, S, HA_OUT_W)
        oa = _flash_call(ha3, ha3, ha3, DIFF_HEADS, 0, DIFF_HEADS, DIFF_HEADS, n_maps=2, tq=FLASH_COLS // 2, tk=FLASH_TK,
                         name="diff_attn", lam_vec=lam_vec, norm_g=diff_norm_g[l].reshape(-1, 1).astype(f32),
                         lam_init=lam_init, vw=2 * LANES)
        ob = _flash_call(qb.reshape(B, S, -1), kb.reshape(B, S, -1), vb.reshape(B, S, -1), MLA_HEADS, 0, 0, 0,
                         n_maps=1, tq=FLASH_COLS, tk=FLASH_TK, name="mla_attn", dv=MLA_V_DIM)
        dil = [_dil_call(hc) for hc in (hc0, hc1, hc2)]
        x32, xb, gate_t = _merge_call(
            x32, oa.reshape(T, -1), ob.reshape(T, -1), [o for o, _ in dil], [lse for _, lse in dil],
            hg, wa_b[l], wb_b[l], wc_b[l], wo_b[l], vec(ln1_g[l]), vec(ln1_b[l]), rw_t, rb, alpha, B, S)
        gate_c = gate_t.reshape(N_EXPERTS // MOE_CHUNK, MOE_CHUNK, T).transpose(0, 2, 1)
        x32, xb = _moe_call(x32, xb, gate_c, wg_b[l], wu_b[l], wd_b[l], wsg_b[l], wsu_b[l], wsd_b[l],
                            vec(ln2_g[l]), vec(ln2_b[l]), alpha)
    return x32.reshape(B, S, D)
```

```python
import functools
import math

import jax
import jax.numpy as jnp
from jax import lax
from jax.experimental import pallas as pl
from jax.experimental.pallas import tpu as pltpu

f32 = jnp.float32
bf16 = jnp.bfloat16

D_MODEL = 1024
HEAD_DIM = 64
ROPE_THETA = 500000.0
ROT_DIM = HEAD_DIM // 4
DIFF_HEADS = 4
MLA_HEADS = 8
MLA_Q_RANK = 384
MLA_KV_RANK = 256
MLA_NOPE_DIM = 64
MLA_ROPE_DIM = 32
MLA_V_DIM = 64
DIL_PATTERNS = ((128, 1), (512, 4), (2048, 16))
DIL_GROUPS = 3
DIL_HEADS = 4
DIL_SIDE = 64
N_EXPERTS = 16
N_EXPERT_GROUPS = 4
EXPERTS_PER_GROUP = 4
EXPERT_HIDDEN = 256
MOE_CHUNK = EXPERTS_PER_GROUP
IN_SIZES = (512, 512, 512, 384, 256, 32, 768, 768, 1536, 3072)
NEG_INF = -1e30
LOG2E = math.log2(math.e)

LANES = 128
VMEM_LIMIT = 52 * 1024 * 1024

HA_W = 1536
HA_OUT_W = 2048
HB_W = 768
HCG_W = 1024
HC_W = 3 * HCG_W
HG_W = 3072

FLASH_TK = 512
FLASH_COLS = 1024
FLASH_UNROLL = 8
SUM_ROWS = 16


def _cparams(sem):
    return pltpu.CompilerParams(dimension_semantics=sem, vmem_limit_bytes=VMEM_LIMIT)


def _layer_norm(z, g, b):
    mu = jnp.mean(z, axis=-1, keepdims=True)
    zc = z - mu
    var = jnp.mean(zc * zc, axis=-1, keepdims=True)
    return zc * lax.rsqrt(var + 1e-5) * g + b


def _rms_norm(z, g):
    return z * lax.rsqrt(jnp.mean(z * z, axis=-1, keepdims=True) + 1e-6) * g


def _dot(a, b):
    return jnp.dot(a, b, preferred_element_type=f32)


def _dot_nt(a, b):
    return lax.dot_general(a, b, (((1,), (1,)), ((), ())), preferred_element_type=f32)


def _rope_tile(xt, cos_t, sin_t, first_half, shift):
    xr = jnp.where(first_half, pltpu.roll(xt, LANES - shift, 1), pltpu.roll(xt, shift, 1))
    return xt * cos_t + xr * sin_t


def _inproj_plan():
    qscale = HEAD_DIM ** -0.5 * LOG2E
    plan = []
    for c in range(0, 512, 256):
        plan.append((c, 256, "rope", 0, c, qscale))
    for c in range(512, 1024, 256):
        plan.append((c, 256, "rope", 0, c, 1.0))
    plan.append((1024, 512, "value", 0, 1024, 1.0))
    plan.append((HA_W, HB_W, "mla", 1, 0, 1.0))
    base = HA_W + HB_W
    for g in range(DIL_GROUPS):
        o = HCG_W * g
        plan.append((base + o, 256, "rope", 2 + g, 0, qscale))
        plan.append((base + o + 256, 256, "rope", 2 + g, 256, 1.0))
        plan.append((base + o + 512, 512, "plain", 2 + g, 512, 1.0))
    base = HA_W + HB_W + HC_W
    for c in range(0, HG_W, 512):
        plan.append((base + c, 512, "gate", 2 + DIL_GROUPS, c, 1.0))
    return plan


def _mla_up_projections(acc, lane, cm_ref, sm_ref, qg_ref, kvg_ref, wq_ref, wk_ref, wv_ref, qb_ref, kb_ref, vb_ref):
    cos_t, sin_t = cm_ref[...], sm_ref[...]
    first_half = lane < MLA_NOPE_DIM + MLA_ROPE_DIM // 2
    shift = MLA_ROPE_DIM // 2
    scale = (MLA_NOPE_DIM + MLA_ROPE_DIM) ** -0.5 * LOG2E
    cn = _rms_norm(acc[:, 0:MLA_KV_RANK], kvg_ref[...]).astype(bf16)
    kr = _rope_tile(acc[:, MLA_KV_RANK:MLA_KV_RANK + LANES], cos_t, sin_t, first_half, shift)
    qn = _rms_norm(acc[:, MLA_KV_RANK + LANES:], qg_ref[...]).astype(bf16)
    q_all = _dot(qn, wq_ref[...])
    k_all = _dot(cn, wk_ref[...])
    v_all = _dot(cn, wv_ref[...])
    for h in range(MLA_HEADS):
        sl = slice(h * LANES, (h + 1) * LANES)
        q = _rope_tile(q_all[:, sl], cos_t, sin_t, first_half, shift) * scale
        qb_ref[:, sl] = q.astype(qb_ref.dtype)
        kb_ref[:, sl] = (k_all[:, sl] + kr).astype(kb_ref.dtype)
        vb_ref[:, sl] = jnp.where(lane == MLA_V_DIM, 1.0, v_all[:, sl]).astype(vb_ref.dtype)


def _inproj_kernel(x_ref, w_ref, bg_ref, cp_ref, sp_ref, cm_ref, sm_ref, qg_ref, kvg_ref, wq_ref, wk_ref, wv_ref,
                   *rest, ln_first):
    if ln_first:
        lng_ref, lnb_ref, ha_ref, qb_ref, kb_ref, vb_ref, hc0_ref, hc1_ref, hc2_ref, hg_ref, x32_ref, xb_ref, stage = rest
        y = _layer_norm(x_ref[...], lng_ref[...], lnb_ref[...])
        x32_ref[...] = y
        x = y.astype(bf16)
        xb_ref[...] = x
    else:
        ha_ref, qb_ref, kb_ref, vb_ref, hc0_ref, hc1_ref, hc2_ref, hg_ref, stage = rest
        x = x_ref[...]
    outs = (ha_ref, None, hc0_ref, hc1_ref, hc2_ref, hg_ref)
    tm = x.shape[0]
    cos_t = cp_ref[...]
    sin_t = sp_ref[...]
    lane = lax.broadcasted_iota(jnp.int32, cos_t.shape, 1)
    first_half = (lane % HEAD_DIM) < (ROT_DIM // 2)
    for (c0, w, kind, oi, o0, scale) in _inproj_plan():
        acc = _dot(x, w_ref[:, c0:c0 + w])
        if kind == "rope":
            tiles = []
            for t in range(w // LANES):
                y = _rope_tile(acc[:, t * LANES:(t + 1) * LANES], cos_t, sin_t, first_half, ROT_DIM // 2)
                tiles.append(y * scale if scale != 1.0 else y)
            acc = jnp.concatenate(tiles, axis=1)
        elif kind == "gate":
            acc = jax.nn.sigmoid(acc + bg_ref[:, o0:o0 + w])
        if kind == "mla":
            latents = acc
            continue
        out = outs[oi]
        if kind == "value":
            ones_tile = (lane == 0).astype(out.dtype)
            for h in range(w // LANES):
                out[:, o0 + 2 * h * LANES:o0 + (2 * h + 1) * LANES] = acc[:, h * LANES:(h + 1) * LANES].astype(out.dtype)
                out[:, o0 + (2 * h + 1) * LANES:o0 + (2 * h + 2) * LANES] = ones_tile
        elif 2 <= oi < 2 + DIL_GROUPS:
            d = DIL_PATTERNS[oi - 2][1]
            if d == 1:
                out[0, :, o0:o0 + w] = acc.astype(out.dtype)
            else:
                nt = w // LANES
                for t in range(nt):
                    stage[t] = acc[:, t * LANES:(t + 1) * LANES]
                for r in range(d):
                    rows = [stage[t, pl.ds(r, tm // d, stride=d), :] for t in range(nt)]
                    out[r, :, o0:o0 + w] = jnp.concatenate(rows, axis=1).astype(out.dtype)
        else:
            out[:, o0:o0 + w] = acc.astype(out.dtype)
    _mla_up_projections(latents, lane, cm_ref, sm_ref, qg_ref, kvg_ref, wq_ref, wk_ref, wv_ref, qb_ref, kb_ref, vb_ref)


def _inproj_call(x, w, bg, cos_t, sin_t, cos_m, sin_m, qg, kvg, wq, wk, wv, B, S, ln=None, tm=256):
    T, D = x.shape
    NW = w.shape[1]
    nb = S // tm
    MW = MLA_HEADS * LANES
    row = lambda width: pl.BlockSpec((tm, width), lambda b, i: (b * nb + i, 0))
    const = lambda shape: pl.BlockSpec(shape, lambda b, i: (0, 0))
    consts = [qg, kvg, wq, wk, wv] + (list(ln) if ln else [])
    extra_specs = [row(D), row(D)] if ln else []
    extra_shapes = [jax.ShapeDtypeStruct((T, D), f32), jax.ShapeDtypeStruct((T, D), bf16)] if ln else []
    hc_specs, hc_shapes = [], []
    for _, d in DIL_PATTERNS:
        hc_specs.append(pl.BlockSpec((None, d, tm // d, HCG_W), lambda b, i: (b, 0, i, 0)))
        hc_shapes.append(jax.ShapeDtypeStruct((B, d, S // d, HCG_W), bf16))
    return pl.pallas_call(
        functools.partial(_inproj_kernel, ln_first=bool(ln)), grid=(B, nb),
        in_specs=[row(D), const((D, NW)), const((1, HG_W)), row(LANES), row(LANES), row(LANES), row(LANES)]
        + [const(a.shape) for a in consts],
        out_specs=[row(HA_OUT_W), row(MW), row(MW), row(MW)] + hc_specs + [row(HG_W)] + extra_specs,
        out_shape=[jax.ShapeDtypeStruct((T, HA_OUT_W), bf16)] + [jax.ShapeDtypeStruct((T, MW), bf16)] * 3 + hc_shapes
        + [jax.ShapeDtypeStruct((T, HG_W), bf16)] + extra_shapes,
        scratch_shapes=[pltpu.VMEM((4, tm, LANES), f32)],
        compiler_params=_cparams(("parallel", "parallel")), name="inproj")(
            x, w, bg, cos_t, sin_t, cos_m, sin_m, *consts)


def _flash_kernel(*refs, n_maps, tq, tk, n_kv, lam_init, dv):
    if n_maps == 2:
        q_ref, qn_ref, k_ref, v_ref, lam_ref, g_ref, o_ref, qt_sc, m_sc, acc_sc, s_sc, mc_sc = refs
    else:
        q_ref, qn_ref, k_ref, v_ref, o_ref, qt_sc, m_sc, acc_sc, s_sc, mc_sc = refs
    i = pl.program_id(2)
    cur = i % 2
    nxt = 1 - cur

    def load_queries(ref, slot):
        q = ref[...].astype(f32)
        if n_maps == 2:
            lane = lax.broadcasted_iota(jnp.int32, q.shape, 1)
            zero = jnp.zeros_like(q)
            q = jnp.concatenate([jnp.where(lane < HEAD_DIM, q, zero), jnp.where(lane >= HEAD_DIM, q, zero)], axis=0)
        qt_sc[slot] = q.T.astype(bf16)

    def scores(j, buf, slot):
        off = pl.multiple_of(j * tk, tk)
        s = _dot(k_ref[pl.ds(off, tk), :], qt_sc[slot])
        s_sc[buf] = s
        mc_sc[buf] = jnp.max(s, axis=0, keepdims=True)

    def accumulate(j, buf):
        off = pl.multiple_of(j * tk, tk)
        m_prev = m_sc[...]
        m_new = jnp.maximum(m_prev, mc_sc[buf])
        p = jnp.exp2(s_sc[buf] - m_new)
        a = jnp.exp2(m_prev - m_new)
        pv = lax.dot_general(v_ref[pl.ds(off, tk), 0:dv + SUM_ROWS], p.astype(bf16), (((0,), (0,)), ((), ())),
                             preferred_element_type=f32)
        acc_sc[...] = a * acc_sc[...] + pv
        m_sc[...] = m_new

    @pl.when(i == 0)
    def _():
        load_queries(q_ref, cur)
        scores(0, 0, cur)

    load_queries(qn_ref, nxt)
    m_sc[...] = jnp.full(m_sc.shape, -jnp.inf, f32)
    acc_sc[...] = jnp.zeros(acc_sc.shape, f32)

    unroll = min(FLASH_UNROLL, n_kv)
    assert unroll % 2 == 0 and n_kv % unroll == 0

    def body(jj, carry):
        for u in range(unroll):
            j = unroll * jj + u
            scores(j + 1, (u + 1) % 2, cur)
            accumulate(j, u % 2)
        return carry

    lax.fori_loop(0, n_kv // unroll - 1, body, 0)
    for j in range(n_kv - unroll, n_kv):
        if j + 1 < n_kv:
            scores(j + 1, (j + 1) % 2, cur)
        else:
            scores(0, 0, nxt)
        accumulate(j, j % 2)
    o = acc_sc[0:dv, :] / acc_sc[dv:dv + 1, :]
    if n_maps == 2:
        lv = lam_ref[...]
        lam = (jnp.exp(jnp.sum(lv[0:1] * lv[1:2], axis=-1, keepdims=True))
               - jnp.exp(jnp.sum(lv[2:3] * lv[3:4], axis=-1, keepdims=True)) + lam_init)
        o = o[:, :tq] - lam * o[:, tq:]
        ms = jnp.mean(o * o, axis=0, keepdims=True)
        o = o * lax.rsqrt(ms + 1e-6) * g_ref[...] * (1.0 - lam_init)
    if dv < LANES:
        o = jnp.concatenate([o, jnp.zeros((LANES - dv, o.shape[1]), f32)], axis=0)
    o_ref[...] = o.T.astype(o_ref.dtype)


def _flash_call(q_arr, k_arr, v_arr, n_heads, q_blk0, k_blk0, v_blk0, n_maps, tq, tk, name,
                lam_vec=None, norm_g=None, lam_init=0.0, dv=LANES, vw=LANES):
    B, S, _ = q_arr.shape
    cols = n_maps * tq
    last = S // tq - 1
    assert dv + SUM_ROWS <= vw
    in_specs = [pl.BlockSpec((None, tq, LANES), lambda b, h, i: (b, i, q_blk0 + h)),
                pl.BlockSpec((None, tq, LANES), lambda b, h, i: (b, jnp.minimum(i + 1, last), q_blk0 + h)),
                pl.BlockSpec((None, S, LANES), lambda b, h, i: (b, 0, k_blk0 + h)),
                pl.BlockSpec((None, S, vw), lambda b, h, i: (b, 0, v_blk0 + h))]
    args = [q_arr, q_arr, k_arr, v_arr]
    if n_maps == 2:
        in_specs += [pl.BlockSpec((8, LANES), lambda b, h, i: (0, 0)),
                     pl.BlockSpec((LANES, 1), lambda b, h, i: (0, 0))]
        args += [lam_vec, norm_g]
    kern = functools.partial(_flash_kernel, n_maps=n_maps, tq=tq, tk=tk, n_kv=S // tk, lam_init=lam_init,
                             dv=dv)
    return pl.pallas_call(
        kern, grid=(B, n_heads, S // tq), in_specs=in_specs,
        out_specs=pl.BlockSpec((None, tq, LANES), lambda b, h, i: (b, i, h)),
        out_shape=jax.ShapeDtypeStruct((B, S, n_heads * LANES), bf16),
        scratch_shapes=[pltpu.VMEM((2, LANES, cols), bf16), pltpu.VMEM((1, cols), f32),
                        pltpu.VMEM((dv + SUM_ROWS, cols), f32), pltpu.VMEM((2, tk, cols), f32),
                        pltpu.VMEM((2, 1, cols), f32)],
        compiler_params=_cparams(("parallel", "parallel", "arbitrary")), name=name)(*args)


def _dil_kernel(q_ref, kp_ref, kc_ref, kn_ref, vp_ref, vc_ref, vn_ref, o_ref, lse_ref, kcat, vcat, *, tq, sub_len):
    i = pl.program_id(2)
    side = DIL_SIDE
    kcat[0:side, :] = kp_ref[...]
    kcat[side:side + tq, :] = kc_ref[...]
    kcat[side + tq:, :] = kn_ref[...]
    vcat[0:side, :] = vp_ref[...]
    vcat[side:side + tq, :] = vc_ref[...]
    vcat[side + tq:, :] = vn_ref[...]
    qs, ks = 2 * side, 4 * side
    lane = lax.broadcasted_iota(jnp.int32, (qs, LANES), 1)
    ii = lax.broadcasted_iota(jnp.int32, (qs, ks), 0)
    jj = lax.broadcasted_iota(jnp.int32, (qs, ks), 1)
    band = (jj - ii >= 0) & (jj - ii <= 2 * side)
    for sb in range(tq // qs):
        q0 = sb * qs
        pos = jj + (i * tq + q0 - side)
        mask = band & (pos >= 0) & (pos < sub_len)
        lse_tile = jnp.zeros((qs, LANES), f32)
        for h in range(DIL_HEADS):
            t = h // 2
            qt = q_ref[q0:q0 + qs, t * LANES:(t + 1) * LANES]
            keep = (lane < HEAD_DIM) if h % 2 == 0 else (lane >= HEAD_DIM)
            qm = jnp.where(keep, qt, jnp.zeros_like(qt))
            s = _dot_nt(qm, kcat[q0:q0 + ks, t * LANES:(t + 1) * LANES])
            s = jnp.where(mask, s, NEG_INF)
            m = jnp.max(s, axis=-1, keepdims=True)
            p = jnp.exp2(s - m)
            l = jnp.sum(p, axis=-1, keepdims=True)
            o = _dot(p.astype(bf16), vcat[q0:q0 + ks, h * LANES:(h + 1) * LANES]) / l
            o_ref[q0:q0 + qs, h * LANES:(h + 1) * LANES] = o.astype(o_ref.dtype)
            lse_tile = jnp.where(lane == h, m + jnp.log2(l), lse_tile)
        lse_ref[q0:q0 + qs, :] = lse_tile


def _dil_call(hc):
    B, d, L, _ = hc.shape
    tq = min(1024, L)
    side = DIL_SIDE
    nblk = tq // side
    last = L // side - 1
    qw, vw = 2 * LANES, 4 * LANES
    prev = lambda i: jnp.maximum(i * nblk - 1, 0)
    nxt = lambda i: jnp.minimum((i + 1) * nblk, last)
    in_specs = [
        pl.BlockSpec((None, None, tq, qw), lambda b, r, i: (b, r, i, 0)),
        pl.BlockSpec((None, None, side, qw), lambda b, r, i: (b, r, prev(i), 1)),
        pl.BlockSpec((None, None, tq, qw), lambda b, r, i: (b, r, i, 1)),
        pl.BlockSpec((None, None, side, qw), lambda b, r, i: (b, r, nxt(i), 1)),
        pl.BlockSpec((None, None, side, vw), lambda b, r, i: (b, r, prev(i), 1)),
        pl.BlockSpec((None, None, tq, vw), lambda b, r, i: (b, r, i, 1)),
        pl.BlockSpec((None, None, side, vw), lambda b, r, i: (b, r, nxt(i), 1)),
    ]
    return pl.pallas_call(
        functools.partial(_dil_kernel, tq=tq, sub_len=L), grid=(B, d, L // tq), in_specs=in_specs,
        out_specs=[pl.BlockSpec((None, None, tq, vw), lambda b, r, i: (b, r, i, 0)),
                   pl.BlockSpec((None, None, tq, LANES), lambda b, r, i: (b, r, i, 0))],
        out_shape=[jax.ShapeDtypeStruct((B, d, L, vw), bf16), jax.ShapeDtypeStruct((B, d, L, LANES), f32)],
        scratch_shapes=[pltpu.VMEM((tq + 2 * side, qw), bf16), pltpu.VMEM((tq + 2 * side, vw), bf16)],
        compiler_params=_cparams(("parallel", "parallel", "arbitrary")),
        name=f"dilated_d{d}")(hc, hc, hc, hc, hc, hc, hc)


def _token_order(ref, stage):
    d, n, w = ref.shape
    if d == 1:
        return ref[0].astype(f32)
    nt = w // LANES
    for r in range(d):
        v = ref[r].astype(f32)
        for t in range(nt):
            stage[t, pl.ds(r, n, stride=d), :] = v[:, t * LANES:(t + 1) * LANES]
    return jnp.concatenate([stage[t] for t in range(nt)], axis=1)


def _merge_kernel(x_ref, oa_ref, ob_ref, o1_ref, o2_ref, o3_ref, l1_ref, l2_ref, l3_ref, g_ref,
                  wa_ref, wb_ref, wc_ref, wo_ref, lg_ref, lb_ref, rw_ref, rb_ref, x32_ref, xb_ref, gate_ref,
                  so2, so3, sl2, sl3, *, alpha):
    l1, l2, l3 = _token_order(l1_ref, None), _token_order(l2_ref, sl2), _token_order(l3_ref, sl3)
    o1, o2, o3 = _token_order(o1_ref, None), _token_order(o2_ref, so2), _token_order(o3_ref, so3)
    mx = jnp.maximum(jnp.maximum(l1, l2), l3)
    e1, e2, e3 = jnp.exp2(l1 - mx), jnp.exp2(l2 - mx), jnp.exp2(l3 - mx)
    den = e1 + e2 + e3
    w1, w2, w3 = e1 / den, e2 / den, e3 / den
    tiles = []
    for h in range(DIL_HEADS):
        sl = slice(h * LANES, (h + 1) * LANES)
        oc = w1[:, h:h + 1] * o1[:, sl] + w2[:, h:h + 1] * o2[:, sl] + w3[:, h:h + 1] * o3[:, sl]
        tiles.append(oc.astype(bf16))
    oc = jnp.concatenate(tiles, axis=1)
    obf = ob_ref[...].astype(f32)
    half = LANES // 2
    ob = jnp.concatenate([obf[:, 2 * j * LANES:(2 * j + 1) * LANES]
                          + pltpu.roll(obf[:, (2 * j + 1) * LANES:(2 * j + 2) * LANES], half, 1)
                          for j in range(MLA_HEADS // 2)], axis=1).astype(bf16)
    D = D_MODEL
    y = (g_ref[:, 0:D].astype(f32) * _dot(oa_ref[...], wa_ref[...])
         + g_ref[:, D:2 * D].astype(f32) * _dot(ob, wb_ref[...])
         + g_ref[:, 2 * D:3 * D].astype(f32) * _dot(oc, wc_ref[...]))
    mix = _dot(y.astype(bf16), wo_ref[...])
    out = _layer_norm(alpha * x_ref[...] + mix, lg_ref[...], lb_ref[...])
    x32_ref[...] = out
    xb = out.astype(bf16)
    xb_ref[...] = xb
    gate_ref[...] = _route(_dot_nt(rw_ref[...], xb), rb_ref[...])


def _merge_call(x32, oa, ob, dil_o, dil_l, hg, wa, wb, wc, wo, lg, lb, rw_t, rb, alpha, B, S, tm=512):
    T, D = x32.shape
    nb = S // tm
    row = lambda a: pl.BlockSpec((tm, a.shape[1]), lambda b, i: (b * nb + i, 0))
    cls = lambda a: pl.BlockSpec((None, a.shape[1], tm // a.shape[1], a.shape[3]), lambda b, i: (b, 0, i, 0))
    full = lambda a: pl.BlockSpec(a.shape, lambda b, i: (0, 0))
    consts = [wa, wb, wc, wo, lg, lb, rw_t, rb]
    ow, lw = dil_o[0].shape[3], dil_l[0].shape[3]
    return pl.pallas_call(
        functools.partial(_merge_kernel, alpha=alpha), grid=(B, nb),
        in_specs=[row(x32), row(oa), row(ob)] + [cls(a) for a in dil_o] + [cls(a) for a in dil_l] + [row(hg)]
        + [full(a) for a in consts],
        out_specs=[pl.BlockSpec((tm, D), lambda b, i: (b * nb + i, 0))] * 2
        + [pl.BlockSpec((N_EXPERTS, tm), lambda b, i: (0, b * nb + i))],
        out_shape=[jax.ShapeDtypeStruct((T, D), f32), jax.ShapeDtypeStruct((T, D), bf16),
                   jax.ShapeDtypeStruct((N_EXPERTS, T), f32)],
        scratch_shapes=[pltpu.VMEM((ow // LANES, tm, LANES), f32), pltpu.VMEM((ow // LANES, tm, LANES), f32),
                        pltpu.VMEM((lw // LANES, tm, LANES), f32), pltpu.VMEM((lw // LANES, tm, LANES), f32)],
        compiler_params=_cparams(("parallel", "parallel")), name="merge")(
            x32, oa, ob, *dil_o, *dil_l, hg, *consts)


def _first_true(flags):
    out, seen = [], None
    for f in flags:
        out.append(f if seen is None else f & ~seen)
        seen = f if seen is None else seen | f
    return out


def _route(logits, bias):
    scores = jax.nn.sigmoid(logits)
    biased = scores + bias
    sc = [scores[e:e + 1, :] for e in range(N_EXPERTS)]
    bi = [biased[e:e + 1, :] for e in range(N_EXPERTS)]
    npg = EXPERTS_PER_GROUP
    gscore = []
    for g in range(N_EXPERT_GROUPS):
        v = bi[g * npg:(g + 1) * npg]
        best = None
        for a in range(npg):
            for b in range(a + 1, npg):
                best = v[a] + v[b] if best is None else jnp.maximum(best, v[a] + v[b])
        gscore.append(best)
    gbest = functools.reduce(jnp.maximum, gscore)
    gsel = _first_true([gs == gbest for gs in gscore])
    zero = jnp.zeros_like(gbest)
    bsel = [functools.reduce(lambda x, y: x + y, [jnp.where(gsel[g], bi[g * npg + j], zero)
                                                  for g in range(N_EXPERT_GROUPS)]) for j in range(npg)]
    ssel = [functools.reduce(lambda x, y: x + y, [jnp.where(gsel[g], sc[g * npg + j], zero)
                                                  for g in range(N_EXPERT_GROUPS)]) for j in range(npg)]
    t1 = functools.reduce(jnp.maximum, bsel)
    i1 = _first_true([b == t1 for b in bsel])
    rest = [jnp.where(i1[j], -jnp.inf, bsel[j]) for j in range(npg)]
    t2 = functools.reduce(jnp.maximum, rest)
    i2 = _first_true([rest[j] == t2 for j in range(npg)])
    w1 = functools.reduce(lambda x, y: x + y, [jnp.where(i1[j], ssel[j], zero) for j in range(npg)])
    w2 = functools.reduce(lambda x, y: x + y, [jnp.where(i2[j], ssel[j], zero) for j in range(npg)])
    den = w1 + w2
    local = [jnp.where(i1[j], w1 / den, zero) + jnp.where(i2[j], w2 / den, zero) for j in range(npg)]
    rows = [jnp.where(gsel[e // npg], local[e % npg], zero) for e in range(N_EXPERTS)]
    return jnp.concatenate(rows, axis=0)


def _moe_kernel(x32_ref, xb_ref, gate_ref, wg_ref, wu_ref, wd_ref, wsg_ref, wsu_ref, wsd_ref, lg_ref, lb_ref,
                o32_ref, ob_ref, acc_sc, *, alpha, n_chunks, per_chunk):
    c = pl.program_id(1)
    x = xb_ref[...]

    @pl.when(c == 0)
    def _():
        hs = jax.nn.silu(_dot(x, wsg_ref[...])) * _dot(x, wsu_ref[...])
        acc_sc[...] = _dot(hs.astype(bf16), wsd_ref[...])

    gate = gate_ref[...]
    hid = []
    for e in range(per_chunk):
        h = jax.nn.silu(_dot(x, wg_ref[e])) * _dot(x, wu_ref[e]) * gate[:, e:e + 1]
        hid.append(h.astype(bf16))
    acc_sc[...] += _dot(jnp.concatenate(hid, axis=1), wd_ref[...])

    @pl.when(c == n_chunks - 1)
    def _():
        out = _layer_norm(alpha * x32_ref[...] + acc_sc[...], lg_ref[...], lb_ref[...])
        o32_ref[...] = out
        ob_ref[...] = out.astype(bf16)


def _moe_call(x32, xb, gate_c, wg, wu, wd, wsg, wsu, wsd, lg, lb, alpha, tm=1024, per_chunk=MOE_CHUNK):
    T, D = x32.shape
    n_chunks = N_EXPERTS // per_chunk
    cw = per_chunk * EXPERT_HIDDEN
    row = pl.BlockSpec((tm, D), lambda i, c: (i, 0))
    full = lambda a: pl.BlockSpec(a.shape, lambda i, c: (0, 0))
    return pl.pallas_call(
        functools.partial(_moe_kernel, alpha=alpha, n_chunks=n_chunks, per_chunk=per_chunk),
        grid=(T // tm, n_chunks),
        in_specs=[row, row, pl.BlockSpec((None, tm, per_chunk), lambda i, c: (c, i, 0)),
                  pl.BlockSpec((per_chunk, D, EXPERT_HIDDEN), lambda i, c: (c, 0, 0)),
                  pl.BlockSpec((per_chunk, D, EXPERT_HIDDEN), lambda i, c: (c, 0, 0)),
                  pl.BlockSpec((cw, D), lambda i, c: (c, 0)),
                  full(wsg), full(wsu), full(wsd), full(lg), full(lb)],
        out_specs=[row, row],
        out_shape=[jax.ShapeDtypeStruct((T, D), f32), jax.ShapeDtypeStruct((T, D), bf16)],
        scratch_shapes=[pltpu.VMEM((tm, D), f32)],
        compiler_params=_cparams(("parallel", "arbitrary")), name="moe")(
            x32, xb, gate_c, wg, wu, wd, wsg, wsu, wsd, lg, lb)


def _rope_tables(positions):
    pos = positions.astype(f32).reshape(-1, 1)
    lane = jnp.arange(LANES)

    def tables(inv_freq, idx, rotated, first_half):
        freq = jnp.where(rotated, inv_freq[idx], 0.0)
        ang = pos * freq[None, :]
        sign = jnp.where(first_half, -1.0, 1.0).astype(f32)
        return jnp.cos(ang), jnp.sin(ang) * sign[None, :]

    half = ROT_DIM // 2
    hl = lane % HEAD_DIM
    inv_p = ROPE_THETA ** (-jnp.arange(0, ROT_DIM, 2, dtype=f32) / ROT_DIM)
    cos_p, sin_p = tables(inv_p, hl % half, hl < ROT_DIM, hl < half)
    half = MLA_ROPE_DIM // 2
    rl = lane - MLA_NOPE_DIM
    inv_m = ROPE_THETA ** (-jnp.arange(0, MLA_ROPE_DIM, 2, dtype=f32) / MLA_ROPE_DIM)
    cos_m, sin_m = tables(inv_m, jnp.clip(rl, 0, MLA_ROPE_DIM - 1) % half, (rl >= 0) & (rl < MLA_ROPE_DIM), rl < half)
    return cos_p, sin_p, cos_m, sin_m


def _pack_w_in(w):
    offs = [0]
    for sz in IN_SIZES:
        offs.append(offs[-1] + sz)
    a_q, a_k, a_v, b_cq, b_ckv, b_kr, c_q, c_k, c_v, g = [w[..., offs[i]:offs[i + 1]] for i in range(len(IN_SIZES))]
    z = lambda n: jnp.zeros(w.shape[:-1] + (n,), w.dtype)
    kr_pad = jnp.concatenate([z(MLA_NOPE_DIM), b_kr, z(LANES - MLA_NOPE_DIM - MLA_ROPE_DIM)], axis=-1)
    cols = [a_q, a_k, a_v, b_ckv, kr_pad, b_cq]
    for gi in range(DIL_GROUPS):
        cols += [c_q[..., 256 * gi:256 * (gi + 1)], c_k[..., 256 * gi:256 * (gi + 1)], c_v[..., 512 * gi:512 * (gi + 1)]]
    cols.append(g)
    return jnp.concatenate(cols, axis=-1).astype(bf16)


def _pad_heads(w, n_heads, lo, hi):
    lead = w.shape[:-1]
    per = w.shape[-1] // n_heads
    wh = w.reshape(*lead, n_heads, per)[..., lo:hi]
    wh = jnp.pad(wh, [(0, 0)] * (len(lead) + 1) + [(0, LANES - (hi - lo))])
    return wh.reshape(*lead, n_heads * LANES).astype(bf16)


def kernel(x, positions, ln_in_g, ln_in_b, w_in, b_gate, lam_q1, lam_k1, lam_q2, lam_k2, diff_norm_g, mla_q_norm_g, mla_kv_norm_g, w_mla_qb, w_mla_kvb, w_branch_a, w_branch_b, w_branch_c, w_out, ln1_g, ln1_b, router_w, router_bias, w_exp_gate, w_exp_up, w_exp_down, w_sh_gate, w_sh_up, w_sh_down, ln2_g, ln2_b):
    B, S, D = x.shape
    T = B * S
    depth = w_in.shape[0]
    assert D == D_MODEL and S % (DIL_PATTERNS[-1][1] * 2 * DIL_SIDE) == 0 and S % FLASH_COLS == 0 and S % FLASH_TK == 0
    alpha = (2 * depth) ** 0.25
    cos_p, sin_p, cos_m, sin_m = _rope_tables(positions)
    rw_t = router_w.T.astype(bf16)
    rb = router_bias.reshape(N_EXPERTS, 1).astype(f32)
    vec = lambda v: v.reshape(1, -1).astype(f32)
    per_q = MLA_NOPE_DIM + MLA_ROPE_DIM
    per_kv = MLA_NOPE_DIM + MLA_V_DIM
    w_in_p = _pack_w_in(w_in)
    wq_p = _pad_heads(w_mla_qb, MLA_HEADS, 0, per_q)
    wk_p = _pad_heads(w_mla_kvb, MLA_HEADS, 0, MLA_NOPE_DIM)
    wv_p = _pad_heads(w_mla_kvb, MLA_HEADS, MLA_NOPE_DIM, per_kv)
    wa_b, wb_b, wc_b, wo_b = (w.astype(bf16) for w in (w_branch_a, w_branch_b, w_branch_c, w_out))
    wg_b, wu_b = w_exp_gate.astype(bf16), w_exp_up.astype(bf16)
    wd_b = w_exp_down.reshape(depth, N_EXPERTS * EXPERT_HIDDEN, D).astype(bf16)
    wsg_b, wsu_b, wsd_b = (w.astype(bf16) for w in (w_sh_gate, w_sh_up, w_sh_down))
    for l in range(depth):
        lam_init = 0.8 - 0.6 * math.exp(-0.3 * l)
        proj = _inproj_call(
            x.reshape(T, D) if l == 0 else xb, w_in_p[l], vec(b_gate[l]), cos_p, sin_p, cos_m, sin_m,
            vec(mla_q_norm_g[l]), vec(mla_kv_norm_g[l]), wq_p[l], wk_p[l], wv_p[l],
            B, S, ln=(vec(ln_in_g), vec(ln_in_b)) if l == 0 else None)
        ha, qb, kb, vb, hc0, hc1, hc2, hg = proj[:8]
        if l == 0:
            x32, xb = proj[8:]
        lam_vec = jnp.pad(jnp.stack([lam_q1[l], lam_k1[l], lam_q2[l], lam_k2[l]]).astype(f32),
                          ((0, 4), (0, LANES - HEAD_DIM)))
        ha3 = ha.reshape(B, S, HA_OUT_W)
        oa = _flash_call(ha3, ha3, ha3, DIFF_HEADS, 0, DIFF_HEADS, DIFF_HEADS, n_maps=2, tq=FLASH_COLS // 2, tk=FLASH_TK,
                         name="diff_attn", lam_vec=lam_vec, norm_g=diff_norm_g[l].reshape(-1, 1).astype(f32),
                         lam_init=lam_init, vw=2 * LANES)
        ob = _flash_call(qb.reshape(B, S, -1), kb.reshape(B, S, -1), vb.reshape(B, S, -1), MLA_HEADS, 0, 0, 0,
                         n_maps=1, tq=FLASH_COLS, tk=FLASH_TK, name="mla_attn", dv=MLA_V_DIM)
        dil = [_dil_call(hc) for hc in (hc0, hc1, hc2)]
        x32, xb, gate_t = _merge_call(
            x32, oa.reshape(T, -1), ob.reshape(T, -1), [o for o, _ in dil], [lse for _, lse in dil],
            hg, wa_b[l], wb_b[l], wc_b[l], wo_b[l], vec(ln1_g[l]), vec(ln1_b[l]), rw_t, rb, alpha, B, S)
        gate_c = gate_t.reshape(N_EXPERTS // MOE_CHUNK, MOE_CHUNK, T).transpose(0, 2, 1)
        x32, xb = _moe_call(x32, xb, gate_c, wg_b[l], wu_b[l], wd_b[l], wsg_b[l], wsu_b[l], wsd_b[l],
                            vec(ln2_g[l]), vec(ln2_b[l]), alpha)
    return x32.reshape(B, S, D)
```

```python
import functools
import math

import jax
import jax.numpy as jnp
from jax import lax
from jax.experimental import pallas as pl
from jax.experimental.pallas import tpu as pltpu

f32 = jnp.float32
bf16 = jnp.bfloat16

D_MODEL = 1024
HEAD_DIM = 64
ROPE_THETA = 500000.0
ROT_DIM = HEAD_DIM // 4
DIFF_HEADS = 4
MLA_HEADS = 8
MLA_Q_RANK = 384
MLA_KV_RANK = 256
MLA_NOPE_DIM = 64
MLA_ROPE_DIM = 32
MLA_V_DIM = 64
DIL_PATTERNS = ((128, 1), (512, 4), (2048, 16))
DIL_GROUPS = 3
DIL_HEADS = 4
DIL_SIDE = 64
N_EXPERTS = 16
N_EXPERT_GROUPS = 4
EXPERTS_PER_GROUP = 4
EXPERT_HIDDEN = 256
MOE_CHUNK = EXPERTS_PER_GROUP
IN_SIZES = (512, 512, 512, 384, 256, 32, 768, 768, 1536, 3072)
NEG_INF = -1e30
LOG2E = math.log2(math.e)

LANES = 128
VMEM_LIMIT = 52 * 1024 * 1024

HA_W = 1536
HA_OUT_W = 2048
HB_W = 768
HCG_W = 1024
HC_W = 3 * HCG_W
HG_W = 3072

FLASH_TK = 512
FLASH_COLS = 1024
FLASH_UNROLL = 8
SUM_ROWS = 16


def _cparams(sem):
    return pltpu.CompilerParams(dimension_semantics=sem, vmem_limit_bytes=VMEM_LIMIT)


def _layer_norm(z, g, b):
    mu = jnp.mean(z, axis=-1, keepdims=True)
    zc = z - mu
    var = jnp.mean(zc * zc, axis=-1, keepdims=True)
    return zc * lax.rsqrt(var + 1e-5) * g + b


def _rms_norm(z, g):
    return z * lax.rsqrt(jnp.mean(z * z, axis=-1, keepdims=True) + 1e-6) * g


def _dot(a, b):
    return jnp.dot(a, b, preferred_element_type=f32)


def _dot_nt(a, b):
    return lax.dot_general(a, b, (((1,), (1,)), ((), ())), preferred_element_type=f32)


def _rope_tile(xt, cos_t, sin_t, first_half, shift):
    xr = jnp.where(first_half, pltpu.roll(xt, LANES - shift, 1), pltpu.roll(xt, shift, 1))
    return xt * cos_t + xr * sin_t


def _inproj_plan():
    qscale = HEAD_DIM ** -0.5 * LOG2E
    plan = []
    for c in range(0, 512, 256):
        plan.append((c, 256, "rope", 0, c, qscale))
    for c in range(512, 1024, 256):
        plan.append((c, 256, "rope", 0, c, 1.0))
    plan.append((1024, 512, "value", 0, 1024, 1.0))
    plan.append((HA_W, HB_W, "mla", 1, 0, 1.0))
    base = HA_W + HB_W
    for g in range(DIL_GROUPS):
        o = HCG_W * g
        plan.append((base + o, 256, "rope", 2 + g, 0, qscale))
        plan.append((base + o + 256, 256, "rope", 2 + g, 256, 1.0))
        plan.append((base + o + 512, 512, "plain", 2 + g, 512, 1.0))
    base = HA_W + HB_W + HC_W
    for c in range(0, HG_W, 512):
        plan.append((base + c, 512, "gate", 2 + DIL_GROUPS, c, 1.0))
    return plan


def _mla_up_projections(acc, lane, cm_ref, sm_ref, qg_ref, kvg_ref, wq_ref, wk_ref, wv_ref, qb_ref, kb_ref, vb_ref):
    cos_t, sin_t = cm_ref[...], sm_ref[...]
    first_half = lane < MLA_NOPE_DIM + MLA_ROPE_DIM // 2
    shift = MLA_ROPE_DIM // 2
    scale = (MLA_NOPE_DIM + MLA_ROPE_DIM) ** -0.5 * LOG2E
    cn = _rms_norm(acc[:, 0:MLA_KV_RANK], kvg_ref[...]).astype(bf16)
    kr = _rope_tile(acc[:, MLA_KV_RANK:MLA_KV_RANK + LANES], cos_t, sin_t, first_half, shift)
    qn = _rms_norm(acc[:, MLA_KV_RANK + LANES:], qg_ref[...]).astype(bf16)
    q_all = _dot(qn, wq_ref[...])
    k_all = _dot(cn, wk_ref[...])
    v_all = _dot(cn, wv_ref[...])
    for h in range(MLA_HEADS):
        sl = slice(h * LANES, (h + 1) * LANES)
        q = _rope_tile(q_all[:, sl], cos_t, sin_t, first_half, shift) * scale
        qb_ref[:, sl] = q.astype(qb_ref.dtype)
        kb_ref[:, sl] = (k_all[:, sl] + kr).astype(kb_ref.dtype)
        vb_ref[:, sl] = jnp.where(lane == MLA_V_DIM, 1.0, v_all[:, sl]).astype(vb_ref.dtype)


def _inproj_kernel(x_ref, w_ref, bg_ref, cp_ref, sp_ref, cm_ref, sm_ref, qg_ref, kvg_ref, wq_ref, wk_ref, wv_ref,
                   *rest, ln_first):
    if ln_first:
        lng_ref, lnb_ref, ha_ref, qb_ref, kb_ref, vb_ref, hc0_ref, hc1_ref, hc2_ref, hg_ref, x32_ref, xb_ref, stage = rest
        y = _layer_norm(x_ref[...], lng_ref[...], lnb_ref[...])
        x32_ref[...] = y
        x = y.astype(bf16)
        xb_ref[...] = x
    else:
        ha_ref, qb_ref, kb_ref, vb_ref, hc0_ref, hc1_ref, hc2_ref, hg_ref, stage = rest
        x = x_ref[...]
    outs = (ha_ref, None, hc0_ref, hc1_ref, hc2_ref, hg_ref)
    tm = x.shape[0]
    cos_t = cp_ref[...]
    sin_t = sp_ref[...]
    lane = lax.broadcasted_iota(jnp.int32, cos_t.shape, 1)
    first_half = (lane % HEAD_DIM) < (ROT_DIM // 2)
    for (c0, w, kind, oi, o0, scale) in _inproj_plan():
        acc = _dot(x, w_ref[:, c0:c0 + w])
        if kind == "rope":
            tiles = []
            for t in range(w // LANES):
                y = _rope_tile(acc[:, t * LANES:(t + 1) * LANES], cos_t, sin_t, first_half, ROT_DIM // 2)
                tiles.append(y * scale if scale != 1.0 else y)
            acc = jnp.concatenate(tiles, axis=1)
        elif kind == "gate":
            acc = jax.nn.sigmoid(acc + bg_ref[:, o0:o0 + w])
        if kind == "mla":
            latents = acc
            continue
        out = outs[oi]
        if kind == "value":
            ones_tile = (lane == 0).astype(out.dtype)
            for h in range(w // LANES):
                out[:, o0 + 2 * h * LANES:o0 + (2 * h + 1) * LANES] = acc[:, h * LANES:(h + 1) * LANES].astype(out.dtype)
                out[:, o0 + (2 * h + 1) * LANES:o0 + (2 * h + 2) * LANES] = ones_tile
        elif 2 <= oi < 2 + DIL_GROUPS:
            d = DIL_PATTERNS[oi - 2][1]
            if d == 1:
                out[0, :, o0:o0 + w] = acc.astype(out.dtype)
            else:
                nt = w // LANES
                for t in range(nt):
                    stage[t] = acc[:, t * LANES:(t + 1) * LANES]
                for r in range(d):
                    rows = [stage[t, pl.ds(r, tm // d, stride=d), :] for t in range(nt)]
                    out[r, :, o0:o0 + w] = jnp.concatenate(rows, axis=1).astype(out.dtype)
        else:
            out[:, o0:o0 + w] = acc.astype(out.dtype)
    _mla_up_projections(latents, lane, cm_ref, sm_ref, qg_ref, kvg_ref, wq_ref, wk_ref, wv_ref, qb_ref, kb_ref, vb_ref)


def _inproj_call(x, w, bg, cos_t, sin_t, cos_m, sin_m, qg, kvg, wq, wk, wv, B, S, ln=None, tm=256):
    T, D = x.shape
    NW = w.shape[1]
    nb = S // tm
    MW = MLA_HEADS * LANES
    row = lambda width: pl.BlockSpec((tm, width), lambda b, i: (b * nb + i, 0))
    const = lambda shape: pl.BlockSpec(shape, lambda b, i: (0, 0))
    consts = [qg, kvg, wq, wk, wv] + (list(ln) if ln else [])
    extra_specs = [row(D), row(D)] if ln else []
    extra_shapes = [jax.ShapeDtypeStruct((T, D), f32), jax.ShapeDtypeStruct((T, D), bf16)] if ln else []
    hc_specs, hc_shapes = [], []
    for _, d in DIL_PATTERNS:
        hc_specs.append(pl.BlockSpec((None, d, tm // d, HCG_W), lambda b, i: (b, 0, i, 0)))
        hc_shapes.append(jax.ShapeDtypeStruct((B, d, S // d, HCG_W), bf16))
    return pl.pallas_call(
        functools.partial(_inproj_kernel, ln_first=bool(ln)), grid=(B, nb),
        in_specs=[row(D), const((D, NW)), const((1, HG_W)), row(LANES), row(LANES), row(LANES), row(LANES)]
        + [const(a.shape) for a in consts],
        out_specs=[row(HA_OUT_W), row(MW), row(MW), row(MW)] + hc_specs + [row(HG_W)] + extra_specs,
        out_shape=[jax.ShapeDtypeStruct((T, HA_OUT_W), bf16)] + [jax.ShapeDtypeStruct((T, MW), bf16)] * 3 + hc_shapes
        + [jax.ShapeDtypeStruct((T, HG_W), bf16)] + extra_shapes,
        scratch_shapes=[pltpu.VMEM((4, tm, LANES), f32)],
        compiler_params=_cparams(("parallel", "parallel")), name="inproj")(
            x, w, bg, cos_t, sin_t, cos_m, sin_m, *consts)


def _flash_kernel(*refs, n_maps, tq, tk, n_kv, lam_init, dv, n_tiles, n_steps):
    if n_maps == 2:
        q_ref, qn_ref, k_ref, v_ref, lam_ref, g_ref, o_ref, qt_sc, m_sc, acc_sc, s_sc, mc_sc = refs
    else:
        q_ref, qn_ref, k_ref, v_ref, o_ref, qt_sc, m_sc, acc_sc, s_sc, mc_sc = refs
    step = pl.program_id(0)
    i = jnp.minimum(step, n_steps - 1) % n_tiles
    cur = i % 2
    nxt = 1 - cur
    a_cur = step % 2
    a_prev = 1 - a_cur

    def load_queries(ref, slot):
        q = ref[...].astype(f32)
        if n_maps == 2:
            lane = lax.broadcasted_iota(jnp.int32, q.shape, 1)
            zero = jnp.zeros_like(q)
            q = jnp.concatenate([jnp.where(lane < HEAD_DIM, q, zero), jnp.where(lane >= HEAD_DIM, q, zero)], axis=0)
        qt_sc[slot] = q.T.astype(bf16)

    def scores(j, buf, slot):
        off = pl.multiple_of(j * tk, tk)
        s = _dot(k_ref[pl.ds(off, tk), :], qt_sc[slot])
        s_sc[buf] = s
        mc_sc[buf] = jnp.max(s, axis=0, keepdims=True)

    def accumulate(j, buf):
        off = pl.multiple_of(j * tk, tk)
        m_prev = m_sc[a_cur]
        m_new = jnp.maximum(m_prev, mc_sc[buf])
        p = jnp.exp2(s_sc[buf] - m_new)
        a = jnp.exp2(m_prev - m_new)
        pv = lax.dot_general(v_ref[pl.ds(off, tk), 0:dv + SUM_ROWS], p.astype(bf16), (((0,), (0,)), ((), ())),
                             preferred_element_type=f32)
        acc_sc[a_cur] = a * acc_sc[a_cur] + pv
        m_sc[a_cur] = m_new

    def finish(slot):
        acc = acc_sc[slot]
        o = acc[0:dv, :] / acc[dv:dv + 1, :]
        if n_maps == 2:
            lv = lam_ref[...]
            lam = (jnp.exp(jnp.sum(lv[0:1] * lv[1:2], axis=-1, keepdims=True))
                   - jnp.exp(jnp.sum(lv[2:3] * lv[3:4], axis=-1, keepdims=True)) + lam_init)
            o = o[:, :tq] - lam * o[:, tq:]
            ms = jnp.mean(o * o, axis=0, keepdims=True)
            o = o * lax.rsqrt(ms + 1e-6) * g_ref[...] * (1.0 - lam_init)
        if dv < LANES:
            o = jnp.concatenate([o, jnp.zeros((LANES - dv, o.shape[1]), f32)], axis=0)
        o_ref[...] = o.T.astype(o_ref.dtype)

    @pl.when(step == 0)
    def _():
        acc_sc[a_prev] = jnp.ones(acc_sc.shape[1:], f32)

    @pl.when(i == 0)
    def _():
        load_queries(q_ref, cur)
        scores(0, 0, cur)

    load_queries(qn_ref, nxt)
    m_sc[a_cur] = jnp.full(m_sc.shape[1:], -jnp.inf, f32)
    acc_sc[a_cur] = jnp.zeros(acc_sc.shape[1:], f32)
    finish(a_prev)

    unroll = min(FLASH_UNROLL, n_kv)
    assert unroll % 2 == 0 and n_kv % unroll == 0

    def body(jj, carry):
        for u in range(unroll):
            j = unroll * jj + u
            scores(j + 1, (u + 1) % 2, cur)
            accumulate(j, u % 2)
        return carry

    lax.fori_loop(0, n_kv // unroll - 1, body, 0)
    for j in range(n_kv - unroll, n_kv):
        if j + 1 < n_kv:
            scores(j + 1, (j + 1) % 2, cur)
        else:
            scores(0, 0, nxt)
        accumulate(j, j % 2)


def _flash_call(q_arr, k_arr, v_arr, n_heads, q_blk0, k_blk0, v_blk0, n_maps, tq, tk, name,
                lam_vec=None, norm_g=None, lam_init=0.0, dv=LANES, vw=LANES):
    B, S, _ = q_arr.shape
    cols = n_maps * tq
    nt = S // tq
    n_steps = B * n_heads * nt
    assert dv + SUM_ROWS <= vw

    def tile(s):
        s = jnp.clip(s, 0, n_steps - 1)
        return s // (n_heads * nt), (s // nt) % n_heads, s % nt

    def spec(shape, col0, which):
        def index(s):
            b, h, i = tile(s - 1 if which == "out" else s)
            row = {"q": i, "next": jnp.minimum(i + 1, nt - 1), "kv": 0, "out": i}[which]
            return b, row, col0 + h
        return pl.BlockSpec(shape, index)

    in_specs = [spec((None, tq, LANES), q_blk0, "q"), spec((None, tq, LANES), q_blk0, "next"),
                spec((None, S, LANES), k_blk0, "kv"), spec((None, S, vw), v_blk0, "kv")]
    args = [q_arr, q_arr, k_arr, v_arr]
    if n_maps == 2:
        in_specs += [pl.BlockSpec((8, LANES), lambda s: (0, 0)), pl.BlockSpec((LANES, 1), lambda s: (0, 0))]
        args += [lam_vec, norm_g]
    kern = functools.partial(_flash_kernel, n_maps=n_maps, tq=tq, tk=tk, n_kv=S // tk, lam_init=lam_init,
                             dv=dv, n_tiles=nt, n_steps=n_steps)
    return pl.pallas_call(
        kern, grid=(n_steps + 1,), in_specs=in_specs, out_specs=spec((None, tq, LANES), 0, "out"),
        out_shape=jax.ShapeDtypeStruct((B, S, n_heads * LANES), bf16),
        scratch_shapes=[pltpu.VMEM((2, LANES, cols), bf16), pltpu.VMEM((2, 1, cols), f32),
                        pltpu.VMEM((2, dv + SUM_ROWS, cols), f32), pltpu.VMEM((2, tk, cols), f32),
                        pltpu.VMEM((2, 1, cols), f32)],
        compiler_params=_cparams(("arbitrary",)), name=name)(*args)


def _dil_kernel(q_ref, kp_ref, kc_ref, kn_ref, vp_ref, vc_ref, vn_ref, o_ref, lse_ref, kcat, vcat, *, tq, sub_len):
    i = pl.program_id(2)
    side = DIL_SIDE
    kcat[0:side, :] = kp_ref[...]
    kcat[side:side + tq, :] = kc_ref[...]
    kcat[side + tq:, :] = kn_ref[...]
    vcat[0:side, :] = vp_ref[...]
    vcat[side:side + tq, :] = vc_ref[...]
    vcat[side + tq:, :] = vn_ref[...]
    qs, ks = 2 * side, 4 * side
    lane = lax.broadcasted_iota(jnp.int32, (qs, LANES), 1)
    ii = lax.broadcasted_iota(jnp.int32, (qs, ks), 0)
    jj = lax.broadcasted_iota(jnp.int32, (qs, ks), 1)
    band = (jj - ii >= 0) & (jj - ii <= 2 * side)
    for sb in range(tq // qs):
        q0 = sb * qs
        pos = jj + (i * tq + q0 - side)
        mask = band & (pos >= 0) & (pos < sub_len)
        lse_tile = jnp.zeros((qs, LANES), f32)
        for h in range(DIL_HEADS):
            t = h // 2
            qt = q_ref[q0:q0 + qs, t * LANES:(t + 1) * LANES]
            keep = (lane < HEAD_DIM) if h % 2 == 0 else (lane >= HEAD_DIM)
            qm = jnp.where(keep, qt, jnp.zeros_like(qt))
            s = _dot_nt(qm, kcat[q0:q0 + ks, t * LANES:(t + 1) * LANES])
            s = jnp.where(mask, s, NEG_INF)
            m = jnp.max(s, axis=-1, keepdims=True)
            p = jnp.exp2(s - m)
            l = jnp.sum(p, axis=-1, keepdims=True)
            o = _dot(p.astype(bf16), vcat[q0:q0 + ks, h * LANES:(h + 1) * LANES]) / l
            o_ref[q0:q0 + qs, h * LANES:(h + 1) * LANES] = o.astype(o_ref.dtype)
            lse_tile = jnp.where(lane == h, m + jnp.log2(l), lse_tile)
        lse_ref[q0:q0 + qs, :] = lse_tile


def _dil_call(hc):
    B, d, L, _ = hc.shape
    tq = min(1024, L)
    side = DIL_SIDE
    nblk = tq // side
    last = L // side - 1
    qw, vw = 2 * LANES, 4 * LANES
    prev = lambda i: jnp.maximum(i * nblk - 1, 0)
    nxt = lambda i: jnp.minimum((i + 1) * nblk, last)
    in_specs = [
        pl.BlockSpec((None, None, tq, qw), lambda b, r, i: (b, r, i, 0)),
        pl.BlockSpec((None, None, side, qw), lambda b, r, i: (b, r, prev(i), 1)),
        pl.BlockSpec((None, None, tq, qw), lambda b, r, i: (b, r, i, 1)),
        pl.BlockSpec((None, None, side, qw), lambda b, r, i: (b, r, nxt(i), 1)),
        pl.BlockSpec((None, None, side, vw), lambda b, r, i: (b, r, prev(i), 1)),
        pl.BlockSpec((None, None, tq, vw), lambda b, r, i: (b, r, i, 1)),
        pl.BlockSpec((None, None, side, vw), lambda b, r, i: (b, r, nxt(i), 1)),
    ]
    return pl.pallas_call(
        functools.partial(_dil_kernel, tq=tq, sub_len=L), grid=(B, d, L // tq), in_specs=in_specs,
        out_specs=[pl.BlockSpec((None, None, tq, vw), lambda b, r, i: (b, r, i, 0)),
                   pl.BlockSpec((None, None, tq, LANES), lambda b, r, i: (b, r, i, 0))],
        out_shape=[jax.ShapeDtypeStruct((B, d, L, vw), bf16), jax.ShapeDtypeStruct((B, d, L, LANES), f32)],
        scratch_shapes=[pltpu.VMEM((tq + 2 * side, qw), bf16), pltpu.VMEM((tq + 2 * side, vw), bf16)],
        compiler_params=_cparams(("parallel", "parallel", "arbitrary")),
        name=f"dilated_d{d}")(hc, hc, hc, hc, hc, hc, hc)


def _token_order(ref, stage):
    d, n, w = ref.shape
    if d == 1:
        return ref[0].astype(f32)
    nt = w // LANES
    for r in range(d):
        v = ref[r].astype(f32)
        for t in range(nt):
            stage[t, pl.ds(r, n, stride=d), :] = v[:, t * LANES:(t + 1) * LANES]
    return jnp.concatenate([stage[t] for t in range(nt)], axis=1)


def _merge_kernel(x_ref, oa_ref, ob_ref, o1_ref, o2_ref, o3_ref, l1_ref, l2_ref, l3_ref, g_ref,
                  wa_ref, wb_ref, wc_ref, wo_ref, lg_ref, lb_ref, rw_ref, rb_ref, x32_ref, xb_ref, gate_ref,
                  so2, so3, sl2, sl3, *, alpha):
    l1, l2, l3 = _token_order(l1_ref, None), _token_order(l2_ref, sl2), _token_order(l3_ref, sl3)
    o1, o2, o3 = _token_order(o1_ref, None), _token_order(o2_ref, so2), _token_order(o3_ref, so3)
    mx = jnp.maximum(jnp.maximum(l1, l2), l3)
    e1, e2, e3 = jnp.exp2(l1 - mx), jnp.exp2(l2 - mx), jnp.exp2(l3 - mx)
    den = e1 + e2 + e3
    w1, w2, w3 = e1 / den, e2 / den, e3 / den
    tiles = []
    for h in range(DIL_HEADS):
        sl = slice(h * LANES, (h + 1) * LANES)
        oc = w1[:, h:h + 1] * o1[:, sl] + w2[:, h:h + 1] * o2[:, sl] + w3[:, h:h + 1] * o3[:, sl]
        tiles.append(oc.astype(bf16))
    oc = jnp.concatenate(tiles, axis=1)
    obf = ob_ref[...].astype(f32)
    half = LANES // 2
    ob = jnp.concatenate([obf[:, 2 * j * LANES:(2 * j + 1) * LANES]
                          + pltpu.roll(obf[:, (2 * j + 1) * LANES:(2 * j + 2) * LANES], half, 1)
                          for j in range(MLA_HEADS // 2)], axis=1).astype(bf16)
    D = D_MODEL
    y = (g_ref[:, 0:D].astype(f32) * _dot(oa_ref[...], wa_ref[...])
         + g_ref[:, D:2 * D].astype(f32) * _dot(ob, wb_ref[...])
         + g_ref[:, 2 * D:3 * D].astype(f32) * _dot(oc, wc_ref[...]))
    mix = _dot(y.astype(bf16), wo_ref[...])
    out = _layer_norm(alpha * x_ref[...] + mix, lg_ref[...], lb_ref[...])
    x32_ref[...] = out
    xb = out.astype(bf16)
    xb_ref[...] = xb
    gate_ref[...] = _route(_dot_nt(rw_ref[...], xb), rb_ref[...])


def _merge_call(x32, oa, ob, dil_o, dil_l, hg, wa, wb, wc, wo, lg, lb, rw_t, rb, alpha, B, S, tm=512):
    T, D = x32.shape
    nb = S // tm
    row = lambda a: pl.BlockSpec((tm, a.shape[1]), lambda b, i: (b * nb + i, 0))
    cls = lambda a: pl.BlockSpec((None, a.shape[1], tm // a.shape[1], a.shape[3]), lambda b, i: (b, 0, i, 0))
    full = lambda a: pl.BlockSpec(a.shape, lambda b, i: (0, 0))
    consts = [wa, wb, wc, wo, lg, lb, rw_t, rb]
    ow, lw = dil_o[0].shape[3], dil_l[0].shape[3]
    return pl.pallas_call(
        functools.partial(_merge_kernel, alpha=alpha), grid=(B, nb),
        in_specs=[row(x32), row(oa), row(ob)] + [cls(a) for a in dil_o] + [cls(a) for a in dil_l] + [row(hg)]
        + [full(a) for a in consts],
        out_specs=[pl.BlockSpec((tm, D), lambda b, i: (b * nb + i, 0))] * 2
        + [pl.BlockSpec((N_EXPERTS, tm), lambda b, i: (0, b * nb + i))],
        out_shape=[jax.ShapeDtypeStruct((T, D), f32), jax.ShapeDtypeStruct((T, D), bf16),
                   jax.ShapeDtypeStruct((N_EXPERTS, T), f32)],
        scratch_shapes=[pltpu.VMEM((ow // LANES, tm, LANES), f32), pltpu.VMEM((ow // LANES, tm, LANES), f32),
                        pltpu.VMEM((lw // LANES, tm, LANES), f32), pltpu.VMEM((lw // LANES, tm, LANES), f32)],
        compiler_params=_cparams(("parallel", "parallel")), name="merge")(
            x32, oa, ob, *dil_o, *dil_l, hg, *consts)


def _first_true(flags):
    out, seen = [], None
    for f in flags:
        out.append(f if seen is None else f & ~seen)
        seen = f if seen is None else seen | f
    return out


def _route(logits, bias):
    scores = jax.nn.sigmoid(logits)
    biased = scores + bias
    sc = [scores[e:e + 1, :] for e in range(N_EXPERTS)]
    bi = [biased[e:e + 1, :] for e in range(N_EXPERTS)]
    npg = EXPERTS_PER_GROUP
    gscore = []
    for g in range(N_EXPERT_GROUPS):
        v = bi[g * npg:(g + 1) * npg]
        best = None
        for a in range(npg):
            for b in range(a + 1, npg):
                best = v[a] + v[b] if best is None else jnp.maximum(best, v[a] + v[b])
        gscore.append(best)
    gbest = functools.reduce(jnp.maximum, gscore)
    gsel = _first_true([gs == gbest for gs in gscore])
    zero = jnp.zeros_like(gbest)
    bsel = [functools.reduce(lambda x, y: x + y, [jnp.where(gsel[g], bi[g * npg + j], zero)
                                                  for g in range(N_EXPERT_GROUPS)]) for j in range(npg)]
    ssel = [functools.reduce(lambda x, y: x + y, [jnp.where(gsel[g], sc[g * npg + j], zero)
                                                  for g in range(N_EXPERT_GROUPS)]) for j in range(npg)]
    t1 = functools.reduce(jnp.maximum, bsel)
    i1 = _first_true([b == t1 for b in bsel])
    rest = [jnp.where(i1[j], -jnp.inf, bsel[j]) for j in range(npg)]
    t2 = functools.reduce(jnp.maximum, rest)
    i2 = _first_true([rest[j] == t2 for j in range(npg)])
    w1 = functools.reduce(lambda x, y: x + y, [jnp.where(i1[j], ssel[j], zero) for j in range(npg)])
    w2 = functools.reduce(lambda x, y: x + y, [jnp.where(i2[j], ssel[j], zero) for j in range(npg)])
    den = w1 + w2
    local = [jnp.where(i1[j], w1 / den, zero) + jnp.where(i2[j], w2 / den, zero) for j in range(npg)]
    rows = [jnp.where(gsel[e // npg], local[e % npg], zero) for e in range(N_EXPERTS)]
    return jnp.concatenate(rows, axis=0)


def _moe_kernel(x32_ref, xb_ref, gate_ref, wg_ref, wu_ref, wd_ref, wsg_ref, wsu_ref, wsd_ref, lg_ref, lb_ref,
                o32_ref, ob_ref, acc_sc, *, alpha, n_chunks, per_chunk):
    c = pl.program_id(1)
    x = xb_ref[...]

    @pl.when(c == 0)
    def _():
        hs = jax.nn.silu(_dot(x, wsg_ref[...])) * _dot(x, wsu_ref[...])
        acc_sc[...] = _dot(hs.astype(bf16), wsd_ref[...])

    gate = gate_ref[...]
    hid = []
    for e in range(per_chunk):
        h = jax.nn.silu(_dot(x, wg_ref[e])) * _dot(x, wu_ref[e]) * gate[:, e:e + 1]
        hid.append(h.astype(bf16))
    acc_sc[...] += _dot(jnp.concatenate(hid, axis=1), wd_ref[...])

    @pl.when(c == n_chunks - 1)
    def _():
        out = _layer_norm(alpha * x32_ref[...] + acc_sc[...], lg_ref[...], lb_ref[...])
        o32_ref[...] = out
        ob_ref[...] = out.astype(bf16)


def _moe_call(x32, xb, gate_c, wg, wu, wd, wsg, wsu, wsd, lg, lb, alpha, tm=1024, per_chunk=MOE_CHUNK):
    T, D = x32.shape
    n_chunks = N_EXPERTS // per_chunk
    cw = per_chunk * EXPERT_HIDDEN
    row = pl.BlockSpec((tm, D), lambda i, c: (i, 0))
    full = lambda a: pl.BlockSpec(a.shape, lambda i, c: (0, 0))
    return pl.pallas_call(
        functools.partial(_moe_kernel, alpha=alpha, n_chunks=n_chunks, per_chunk=per_chunk),
        grid=(T // tm, n_chunks),
        in_specs=[row, row, pl.BlockSpec((None, tm, per_chunk), lambda i, c: (c, i, 0)),
                  pl.BlockSpec((per_chunk, D, EXPERT_HIDDEN), lambda i, c: (c, 0, 0)),
                  pl.BlockSpec((per_chunk, D, EXPERT_HIDDEN), lambda i, c: (c, 0, 0)),
                  pl.BlockSpec((cw, D), lambda i, c: (c, 0)),
                  full(wsg), full(wsu), full(wsd), full(lg), full(lb)],
        out_specs=[row, row],
        out_shape=[jax.ShapeDtypeStruct((T, D), f32), jax.ShapeDtypeStruct((T, D), bf16)],
        scratch_shapes=[pltpu.VMEM((tm, D), f32)],
        compiler_params=_cparams(("parallel", "arbitrary")), name="moe")(
            x32, xb, gate_c, wg, wu, wd, wsg, wsu, wsd, lg, lb)


def _rope_tables(positions):
    pos = positions.astype(f32).reshape(-1, 1)
    lane = jnp.arange(LANES)

    def tables(inv_freq, idx, rotated, first_half):
        freq = jnp.where(rotated, inv_freq[idx], 0.0)
        ang = pos * freq[None, :]
        sign = jnp.where(first_half, -1.0, 1.0).astype(f32)
        return jnp.cos(ang), jnp.sin(ang) * sign[None, :]

    half = ROT_DIM // 2
    hl = lane % HEAD_DIM
    inv_p = ROPE_THETA ** (-jnp.arange(0, ROT_DIM, 2, dtype=f32) / ROT_DIM)
    cos_p, sin_p = tables(inv_p, hl % half, hl < ROT_DIM, hl < half)
    half = MLA_ROPE_DIM // 2
    rl = lane - MLA_NOPE_DIM
    inv_m = ROPE_THETA ** (-jnp.arange(0, MLA_ROPE_DIM, 2, dtype=f32) / MLA_ROPE_DIM)
    cos_m, sin_m = tables(inv_m, jnp.clip(rl, 0, MLA_ROPE_DIM - 1) % half, (rl >= 0) & (rl < MLA_ROPE_DIM), rl < half)
    return cos_p, sin_p, cos_m, sin_m


def _pack_w_in(w):
    offs = [0]
    for sz in IN_SIZES:
        offs.append(offs[-1] + sz)
    a_q, a_k, a_v, b_cq, b_ckv, b_kr, c_q, c_k, c_v, g = [w[..., offs[i]:offs[i + 1]] for i in range(len(IN_SIZES))]
    z = lambda n: jnp.zeros(w.shape[:-1] + (n,), w.dtype)
    kr_pad = jnp.concatenate([z(MLA_NOPE_DIM), b_kr, z(LANES - MLA_NOPE_DIM - MLA_ROPE_DIM)], axis=-1)
    cols = [a_q, a_k, a_v, b_ckv, kr_pad, b_cq]
    for gi in range(DIL_GROUPS):
        cols += [c_q[..., 256 * gi:256 * (gi + 1)], c_k[..., 256 * gi:256 * (gi + 1)], c_v[..., 512 * gi:512 * (gi + 1)]]
    cols.append(g)
    return jnp.concatenate(cols, axis=-1).astype(bf16)


def _pad_heads(w, n_heads, lo, hi):
    lead = w.shape[:-1]
    per = w.shape[-1] // n_heads
    wh = w.reshape(*lead, n_heads, per)[..., lo:hi]
    wh = jnp.pad(wh, [(0, 0)] * (len(lead) + 1) + [(0, LANES - (hi - lo))])
    return wh.reshape(*lead, n_heads * LANES).astype(bf16)


def kernel(x, positions, ln_in_g, ln_in_b, w_in, b_gate, lam_q1, lam_k1, lam_q2, lam_k2, diff_norm_g, mla_q_norm_g, mla_kv_norm_g, w_mla_qb, w_mla_kvb, w_branch_a, w_branch_b, w_branch_c, w_out, ln1_g, ln1_b, router_w, router_bias, w_exp_gate, w_exp_up, w_exp_down, w_sh_gate, w_sh_up, w_sh_down, ln2_g, ln2_b):
    B, S, D = x.shape
    T = B * S
    depth = w_in.shape[0]
    assert D == D_MODEL and S % (DIL_PATTERNS[-1][1] * 2 * DIL_SIDE) == 0 and S % FLASH_COLS == 0 and S % FLASH_TK == 0
    alpha = (2 * depth) ** 0.25
    cos_p, sin_p, cos_m, sin_m = _rope_tables(positions)
    rw_t = router_w.T.astype(bf16)
    rb = router_bias.reshape(N_EXPERTS, 1).astype(f32)
    vec = lambda v: v.reshape(1, -1).astype(f32)
    per_q = MLA_NOPE_DIM + MLA_ROPE_DIM
    per_kv = MLA_NOPE_DIM + MLA_V_DIM
    w_in_p = _pack_w_in(w_in)
    wq_p = _pad_heads(w_mla_qb, MLA_HEADS, 0, per_q)
    wk_p = _pad_heads(w_mla_kvb, MLA_HEADS, 0, MLA_NOPE_DIM)
    wv_p = _pad_heads(w_mla_kvb, MLA_HEADS, MLA_NOPE_DIM, per_kv)
    wa_b, wb_b, wc_b, wo_b = (w.astype(bf16) for w in (w_branch_a, w_branch_b, w_branch_c, w_out))
    wg_b, wu_b = w_exp_gate.astype(bf16), w_exp_up.astype(bf16)
    wd_b = w_exp_down.reshape(depth, N_EXPERTS * EXPERT_HIDDEN, D).astype(bf16)
    wsg_b, wsu_b, wsd_b = (w.astype(bf16) for w in (w_sh_gate, w_sh_up, w_sh_down))
    for l in range(depth):
        lam_init = 0.8 - 0.6 * math.exp(-0.3 * l)
        proj = _inproj_call(
            x.reshape(T, D) if l == 0 else xb, w_in_p[l], vec(b_gate[l]), cos_p, sin_p, cos_m, sin_m,
            vec(mla_q_norm_g[l]), vec(mla_kv_norm_g[l]), wq_p[l], wk_p[l], wv_p[l],
            B, S, ln=(vec(ln_in_g), vec(ln_in_b)) if l == 0 else None)
        ha, qb, kb, vb, hc0, hc1, hc2, hg = proj[:8]
        if l == 0:
            x32, xb = proj[8:]
        lam_vec = jnp.pad(jnp.stack([lam_q1[l], lam_k1[l], lam_q2[l], lam_k2[l]]).astype(f32),
                          ((0, 4), (0, LANES - HEAD_DIM)))
        ha3 = ha.reshape(B, S, HA_OUT_W)
        oa = _flash_call(ha3, ha3, ha3, DIFF_HEADS, 0, DIFF_HEADS, DIFF_HEADS, n_maps=2, tq=FLASH_COLS // 2, tk=FLASH_TK,
                         name="diff_attn", lam_vec=lam_vec, norm_g=diff_norm_g[l].reshape(-1, 1).astype(f32),
                         lam_init=lam_init, vw=2 * LANES)
        ob = _flash_call(qb.reshape(B, S, -1), kb.reshape(B, S, -1), vb.reshape(B, S, -1), MLA_HEADS, 0, 0, 0,
                         n_maps=1, tq=FLASH_COLS, tk=FLASH_TK, name="mla_attn", dv=MLA_V_DIM)
        dil = [_dil_call(hc) for hc in (hc0, hc1, hc2)]
        x32, xb, gate_t = _merge_call(
            x32, oa.reshape(T, -1), ob.reshape(T, -1), [o for o, _ in dil], [lse for _, lse in dil],
            hg, wa_b[l], wb_b[l], wc_b[l], wo_b[l], vec(ln1_g[l]), vec(ln1_b[l]), rw_t, rb, alpha, B, S)
        gate_c = gate_t.reshape(N_EXPERTS // MOE_CHUNK, MOE_CHUNK, T).transpose(0, 2, 1)
        x32, xb = _moe_call(x32, xb, gate_c, wg_b[l], wu_b[l], wd_b[l], wsg_b[l], wsu_b[l], wsd_b[l],
                            vec(ln2_g[l]), vec(ln2_b[l]), alpha)
    return x32.reshape(B, S, D)
```

```python
import functools
import math

import jax
import jax.numpy as jnp
from jax import lax
from jax.experimental import pallas as pl
from jax.experimental.pallas import tpu as pltpu

f32 = jnp.float32
bf16 = jnp.bfloat16

D_MODEL = 1024
HEAD_DIM = 64
ROPE_THETA = 500000.0
ROT_DIM = HEAD_DIM // 4
DIFF_HEADS = 4
MLA_HEADS = 8
MLA_Q_RANK = 384
MLA_KV_RANK = 256
MLA_NOPE_DIM = 64
MLA_ROPE_DIM = 32
MLA_V_DIM = 64
DIL_PATTERNS = ((128, 1), (512, 4), (2048, 16))
DIL_GROUPS = 3
DIL_HEADS = 4
DIL_SIDE = 64
N_EXPERTS = 16
N_EXPERT_GROUPS = 4
EXPERTS_PER_GROUP = 4
EXPERT_HIDDEN = 256
MOE_CHUNK = EXPERTS_PER_GROUP
IN_SIZES = (512, 512, 512, 384, 256, 32, 768, 768, 1536, 3072)
NEG_INF = -1e30
LOG2E = math.log2(math.e)

LANES = 128
VMEM_LIMIT = 52 * 1024 * 1024

HA_W = 1536
HA_OUT_W = 2048
HB_W = 768
HCG_W = 1024
HC_W = 3 * HCG_W
HG_W = 3072

FLASH_TK = 512
FLASH_COLS = 1024
FLASH_UNROLL = 8
SUM_ROWS = 16


def _cparams(sem):
    return pltpu.CompilerParams(dimension_semantics=sem, vmem_limit_bytes=VMEM_LIMIT)


def _layer_norm(z, g, b):
    mu = jnp.mean(z, axis=-1, keepdims=True)
    zc = z - mu
    var = jnp.mean(zc * zc, axis=-1, keepdims=True)
    return zc * lax.rsqrt(var + 1e-5) * g + b


def _rms_norm(z, g):
    return z * lax.rsqrt(jnp.mean(z * z, axis=-1, keepdims=True) + 1e-6) * g


def _dot(a, b):
    return jnp.dot(a, b, preferred_element_type=f32)


def _dot_nt(a, b):
    return lax.dot_general(a, b, (((1,), (1,)), ((), ())), preferred_element_type=f32)


def _rope_tile(xt, cos_t, sin_t, first_half, shift):
    xr = jnp.where(first_half, pltpu.roll(xt, LANES - shift, 1), pltpu.roll(xt, shift, 1))
    return xt * cos_t + xr * sin_t


def _inproj_plan():
    qscale = HEAD_DIM ** -0.5 * LOG2E
    plan = []
    for c in range(0, 512, 256):
        plan.append((c, 256, "rope", 0, c, qscale))
    for c in range(512, 1024, 256):
        plan.append((c, 256, "rope", 0, c, 1.0))
    plan.append((1024, 512, "value", 0, 1024, 1.0))
    plan.append((HA_W, HB_W, "mla", 1, 0, 1.0))
    base = HA_W + HB_W
    for g in range(DIL_GROUPS):
        o = HCG_W * g
        plan.append((base + o, 256, "rope", 2 + g, 0, qscale))
        plan.append((base + o + 256, 256, "rope", 2 + g, 256, 1.0))
        plan.append((base + o + 512, 512, "plain", 2 + g, 512, 1.0))
    base = HA_W + HB_W + HC_W
    for c in range(0, HG_W, 512):
        plan.append((base + c, 512, "gate", 2 + DIL_GROUPS, c, 1.0))
    return plan


def _mla_up_projections(acc, lane, cm_ref, sm_ref, qg_ref, kvg_ref, wq_ref, wk_ref, wv_ref, qb_ref, kb_ref, vb_ref):
    cos_t, sin_t = cm_ref[...], sm_ref[...]
    first_half = lane < MLA_NOPE_DIM + MLA_ROPE_DIM // 2
    shift = MLA_ROPE_DIM // 2
    scale = (MLA_NOPE_DIM + MLA_ROPE_DIM) ** -0.5 * LOG2E
    cn = _rms_norm(acc[:, 0:MLA_KV_RANK], kvg_ref[...]).astype(bf16)
    kr = _rope_tile(acc[:, MLA_KV_RANK:MLA_KV_RANK + LANES], cos_t, sin_t, first_half, shift)
    qn = _rms_norm(acc[:, MLA_KV_RANK + LANES:], qg_ref[...]).astype(bf16)
    q_all = _dot(qn, wq_ref[...])
    k_all = _dot(cn, wk_ref[...])
    v_all = _dot(cn, wv_ref[...])
    for h in range(MLA_HEADS):
        sl = slice(h * LANES, (h + 1) * LANES)
        q = _rope_tile(q_all[:, sl], cos_t, sin_t, first_half, shift) * scale
        qb_ref[:, sl] = q.astype(qb_ref.dtype)
        kb_ref[:, sl] = (k_all[:, sl] + kr).astype(kb_ref.dtype)
        vb_ref[:, sl] = jnp.where(lane == MLA_V_DIM, 1.0, v_all[:, sl]).astype(vb_ref.dtype)


def _inproj_kernel(x_ref, w_ref, bg_ref, cp_ref, sp_ref, cm_ref, sm_ref, qg_ref, kvg_ref, wq_ref, wk_ref, wv_ref,
                   *rest, ln_first):
    if ln_first:
        lng_ref, lnb_ref, ha_ref, qb_ref, kb_ref, vb_ref, hc0_ref, hc1_ref, hc2_ref, hg_ref, x32_ref, xb_ref, stage = rest
        y = _layer_norm(x_ref[...], lng_ref[...], lnb_ref[...])
        x32_ref[...] = y
        x = y.astype(bf16)
        xb_ref[...] = x
    else:
        ha_ref, qb_ref, kb_ref, vb_ref, hc0_ref, hc1_ref, hc2_ref, hg_ref, stage = rest
        x = x_ref[...]
    outs = (ha_ref, None, hc0_ref, hc1_ref, hc2_ref, hg_ref)
    tm = x.shape[0]
    cos_t = cp_ref[...]
    sin_t = sp_ref[...]
    lane = lax.broadcasted_iota(jnp.int32, cos_t.shape, 1)
    first_half = (lane % HEAD_DIM) < (ROT_DIM // 2)
    for (c0, w, kind, oi, o0, scale) in _inproj_plan():
        acc = _dot(x, w_ref[:, c0:c0 + w])
        if kind == "rope":
            tiles = []
            for t in range(w // LANES):
                y = _rope_tile(acc[:, t * LANES:(t + 1) * LANES], cos_t, sin_t, first_half, ROT_DIM // 2)
                tiles.append(y * scale if scale != 1.0 else y)
            acc = jnp.concatenate(tiles, axis=1)
        elif kind == "gate":
            acc = jax.nn.sigmoid(acc + bg_ref[:, o0:o0 + w])
        if kind == "mla":
            latents = acc
            continue
        out = outs[oi]
        if kind == "value":
            ones_tile = (lane == 0).astype(out.dtype)
            for h in range(w // LANES):
                out[:, o0 + 2 * h * LANES:o0 + (2 * h + 1) * LANES] = acc[:, h * LANES:(h + 1) * LANES].astype(out.dtype)
                out[:, o0 + (2 * h + 1) * LANES:o0 + (2 * h + 2) * LANES] = ones_tile
        elif 2 <= oi < 2 + DIL_GROUPS:
            d = DIL_PATTERNS[oi - 2][1]
            if d == 1:
                out[0, :, o0:o0 + w] = acc.astype(out.dtype)
            else:
                nt = w // LANES
                for t in range(nt):
                    stage[t] = acc[:, t * LANES:(t + 1) * LANES]
                for r in range(d):
                    rows = [stage[t, pl.ds(r, tm // d, stride=d), :] for t in range(nt)]
                    out[r, :, o0:o0 + w] = jnp.concatenate(rows, axis=1).astype(out.dtype)
        else:
            out[:, o0:o0 + w] = acc.astype(out.dtype)
    _mla_up_projections(latents, lane, cm_ref, sm_ref, qg_ref, kvg_ref, wq_ref, wk_ref, wv_ref, qb_ref, kb_ref, vb_ref)


def _inproj_call(x, w, bg, cos_t, sin_t, cos_m, sin_m, qg, kvg, wq, wk, wv, B, S, ln=None, tm=256):
    T, D = x.shape
    NW = w.shape[1]
    nb = S // tm
    MW = MLA_HEADS * LANES
    row = lambda width: pl.BlockSpec((tm, width), lambda b, i: (b * nb + i, 0))
    const = lambda shape: pl.BlockSpec(shape, lambda b, i: (0, 0))
    consts = [qg, kvg, wq, wk, wv] + (list(ln) if ln else [])
    extra_specs = [row(D), row(D)] if ln else []
    extra_shapes = [jax.ShapeDtypeStruct((T, D), f32), jax.ShapeDtypeStruct((T, D), bf16)] if ln else []
    hc_specs, hc_shapes = [], []
    for _, d in DIL_PATTERNS:
        hc_specs.append(pl.BlockSpec((None, d, tm // d, HCG_W), lambda b, i: (b, 0, i, 0)))
        hc_shapes.append(jax.ShapeDtypeStruct((B, d, S // d, HCG_W), bf16))
    return pl.pallas_call(
        functools.partial(_inproj_kernel, ln_first=bool(ln)), grid=(B, nb),
        in_specs=[row(D), const((D, NW)), const((1, HG_W)), row(LANES), row(LANES), row(LANES), row(LANES)]
        + [const(a.shape) for a in consts],
        out_specs=[row(HA_OUT_W), row(MW), row(MW), row(MW)] + hc_specs + [row(HG_W)] + extra_specs,
        out_shape=[jax.ShapeDtypeStruct((T, HA_OUT_W), bf16)] + [jax.ShapeDtypeStruct((T, MW), bf16)] * 3 + hc_shapes
        + [jax.ShapeDtypeStruct((T, HG_W), bf16)] + extra_shapes,
        scratch_shapes=[pltpu.VMEM((4, tm, LANES), f32)],
        compiler_params=_cparams(("parallel", "parallel")), name="inproj")(
            x, w, bg, cos_t, sin_t, cos_m, sin_m, *consts)


def _flash_kernel(*refs, n_maps, tq, tk, n_kv, lam_init, dv):
    if n_maps == 2:
        q_ref, qn_ref, k_ref, v_ref, lam_ref, g_ref, o_ref, qt_sc, m_sc, acc_sc, s_sc, mc_sc = refs
    else:
        q_ref, qn_ref, k_ref, v_ref, o_ref, qt_sc, m_sc, acc_sc, s_sc, mc_sc = refs
    i = pl.program_id(2)
    cur = i % 2
    nxt = 1 - cur

    def load_queries(ref, slot):
        q = ref[...].astype(f32)
        if n_maps == 2:
            lane = lax.broadcasted_iota(jnp.int32, q.shape, 1)
            zero = jnp.zeros_like(q)
            q = jnp.concatenate([jnp.where(lane < HEAD_DIM, q, zero), jnp.where(lane >= HEAD_DIM, q, zero)], axis=0)
        qt_sc[slot] = q.T.astype(bf16)

    def scores(j, buf, slot):
        off = pl.multiple_of(j * tk, tk)
        s = _dot(k_ref[pl.ds(off, tk), :], qt_sc[slot])
        s_sc[buf] = s
        mc_sc[buf] = jnp.max(s, axis=0, keepdims=True)

    def accumulate(j, buf):
        off = pl.multiple_of(j * tk, tk)
        m_prev = m_sc[...]
        m_new = jnp.maximum(m_prev, mc_sc[buf])
        p = jnp.exp2(s_sc[buf] - m_new)
        a = jnp.exp2(m_prev - m_new)
        pv = lax.dot_general(v_ref[pl.ds(off, tk), 0:dv + SUM_ROWS], p.astype(bf16), (((0,), (0,)), ((), ())),
                             preferred_element_type=f32)
        acc_sc[...] = a * acc_sc[...] + pv
        m_sc[...] = m_new

    @pl.when(i == 0)
    def _():
        load_queries(q_ref, cur)
        scores(0, 0, cur)

    load_queries(qn_ref, nxt)
    m_sc[...] = jnp.full(m_sc.shape, -jnp.inf, f32)
    acc_sc[...] = jnp.zeros(acc_sc.shape, f32)

    unroll = min(FLASH_UNROLL, n_kv)
    assert unroll % 2 == 0 and n_kv % unroll == 0

    def body(jj, carry):
        for u in range(unroll):
            j = unroll * jj + u
            scores(j + 1, (u + 1) % 2, cur)
            accumulate(j, u % 2)
        return carry

    lax.fori_loop(0, n_kv // unroll - 1, body, 0)
    for j in range(n_kv - unroll, n_kv):
        if j + 1 < n_kv:
            scores(j + 1, (j + 1) % 2, cur)
        else:
            scores(0, 0, nxt)
        accumulate(j, j % 2)
    o = acc_sc[0:dv, :] / acc_sc[dv:dv + 1, :]
    if n_maps == 2:
        lv = lam_ref[...]
        lam = (jnp.exp(jnp.sum(lv[0:1] * lv[1:2], axis=-1, keepdims=True))
               - jnp.exp(jnp.sum(lv[2:3] * lv[3:4], axis=-1, keepdims=True)) + lam_init)
        o = o[:, :tq] - lam * o[:, tq:]
        ms = jnp.mean(o * o, axis=0, keepdims=True)
        o = o * lax.rsqrt(ms + 1e-6) * g_ref[...] * (1.0 - lam_init)
    if dv < LANES:
        o = jnp.concatenate([o, jnp.zeros((LANES - dv, o.shape[1]), f32)], axis=0)
    o_ref[...] = o.T.astype(o_ref.dtype)


def _flash_call(q_arr, k_arr, v_arr, n_heads, q_blk0, k_blk0, v_blk0, n_maps, tq, tk, name,
                lam_vec=None, norm_g=None, lam_init=0.0, dv=LANES, vw=LANES):
    B, S, _ = q_arr.shape
    cols = n_maps * tq
    last = S // tq - 1
    assert dv + SUM_ROWS <= vw
    in_specs = [pl.BlockSpec((None, tq, LANES), lambda b, h, i: (b, i, q_blk0 + h)),
                pl.BlockSpec((None, tq, LANES), lambda b, h, i: (b, jnp.minimum(i + 1, last), q_blk0 + h)),
                pl.BlockSpec((None, S, LANES), lambda b, h, i: (b, 0, k_blk0 + h)),
                pl.BlockSpec((None, S, vw), lambda b, h, i: (b, 0, v_blk0 + h))]
    args = [q_arr, q_arr, k_arr, v_arr]
    if n_maps == 2:
        in_specs += [pl.BlockSpec((8, LANES), lambda b, h, i: (0, 0)),
                     pl.BlockSpec((LANES, 1), lambda b, h, i: (0, 0))]
        args += [lam_vec, norm_g]
    kern = functools.partial(_flash_kernel, n_maps=n_maps, tq=tq, tk=tk, n_kv=S // tk, lam_init=lam_init,
                             dv=dv)
    return pl.pallas_call(
        kern, grid=(B, n_heads, S // tq), in_specs=in_specs,
        out_specs=pl.BlockSpec((None, tq, LANES), lambda b, h, i: (b, i, h)),
        out_shape=jax.ShapeDtypeStruct((B, S, n_heads * LANES), bf16),
        scratch_shapes=[pltpu.VMEM((2, LANES, cols), bf16), pltpu.VMEM((1, cols), f32),
                        pltpu.VMEM((dv + SUM_ROWS, cols), f32), pltpu.VMEM((2, tk, cols), f32),
                        pltpu.VMEM((2, 1, cols), f32)],
        compiler_params=_cparams(("parallel", "parallel", "arbitrary")), name=name)(*args)


def _dil_kernel(q_ref, kp_ref, kc_ref, kn_ref, vp_ref, vc_ref, vn_ref, o_ref, lse_ref, kcat, vcat, *, tq, sub_len):
    i = pl.program_id(2)
    side = DIL_SIDE
    kcat[0:side, :] = kp_ref[...]
    kcat[side:side + tq, :] = kc_ref[...]
    kcat[side + tq:, :] = kn_ref[...]
    vcat[0:side, :] = vp_ref[...]
    vcat[side:side + tq, :] = vc_ref[...]
    vcat[side + tq:, :] = vn_ref[...]
    qs, ks = 2 * side, 4 * side
    lane = lax.broadcasted_iota(jnp.int32, (qs, LANES), 1)
    ii = lax.broadcasted_iota(jnp.int32, (qs, ks), 0)
    jj = lax.broadcasted_iota(jnp.int32, (qs, ks), 1)
    band = (jj - ii >= 0) & (jj - ii <= 2 * side)
    def scores(sb, h):
        q0, t = sb * qs, h // 2
        qt = q_ref[q0:q0 + qs, t * LANES:(t + 1) * LANES]
        keep = (lane < HEAD_DIM) if h % 2 == 0 else (lane >= HEAD_DIM)
        qm = jnp.where(keep, qt, jnp.zeros_like(qt))
        return _dot_nt(qm, kcat[q0:q0 + ks, t * LANES:(t + 1) * LANES])

    blocks = [(sb, h) for sb in range(tq // qs) for h in range(DIL_HEADS)]
    s_next = scores(*blocks[0])
    for n, (sb, h) in enumerate(blocks):
        q0 = sb * qs
        s = s_next
        if n + 1 < len(blocks):
            s_next = scores(*blocks[n + 1])
        if h == 0:
            pos = jj + (i * tq + q0 - side)
            mask = band & (pos >= 0) & (pos < sub_len)
            lse_tile = jnp.zeros((qs, LANES), f32)
        s = jnp.where(mask, s, NEG_INF)
        m = jnp.max(s, axis=-1, keepdims=True)
        p = jnp.exp2(s - m)
        l = jnp.sum(p, axis=-1, keepdims=True)
        o = _dot(p.astype(bf16), vcat[q0:q0 + ks, h * LANES:(h + 1) * LANES]) / l
        o_ref[q0:q0 + qs, h * LANES:(h + 1) * LANES] = o.astype(o_ref.dtype)
        lse_tile = jnp.where(lane == h, m + jnp.log2(l), lse_tile)
        if h == DIL_HEADS - 1:
            lse_ref[q0:q0 + qs, :] = lse_tile


def _dil_call(hc):
    B, d, L, _ = hc.shape
    tq = min(1024, L)
    side = DIL_SIDE
    nblk = tq // side
    last = L // side - 1
    qw, vw = 2 * LANES, 4 * LANES
    prev = lambda i: jnp.maximum(i * nblk - 1, 0)
    nxt = lambda i: jnp.minimum((i + 1) * nblk, last)
    in_specs = [
        pl.BlockSpec((None, None, tq, qw), lambda b, r, i: (b, r, i, 0)),
        pl.BlockSpec((None, None, side, qw), lambda b, r, i: (b, r, prev(i), 1)),
        pl.BlockSpec((None, None, tq, qw), lambda b, r, i: (b, r, i, 1)),
        pl.BlockSpec((None, None, side, qw), lambda b, r, i: (b, r, nxt(i), 1)),
        pl.BlockSpec((None, None, side, vw), lambda b, r, i: (b, r, prev(i), 1)),
        pl.BlockSpec((None, None, tq, vw), lambda b, r, i: (b, r, i, 1)),
        pl.BlockSpec((None, None, side, vw), lambda b, r, i: (b, r, nxt(i), 1)),
    ]
    return pl.pallas_call(
        functools.partial(_dil_kernel, tq=tq, sub_len=L), grid=(B, d, L // tq), in_specs=in_specs,
        out_specs=[pl.BlockSpec((None, None, tq, vw), lambda b, r, i: (b, r, i, 0)),
                   pl.BlockSpec((None, None, tq, LANES), lambda b, r, i: (b, r, i, 0))],
        out_shape=[jax.ShapeDtypeStruct((B, d, L, vw), bf16), jax.ShapeDtypeStruct((B, d, L, LANES), f32)],
        scratch_shapes=[pltpu.VMEM((tq + 2 * side, qw), bf16), pltpu.VMEM((tq + 2 * side, vw), bf16)],
        compiler_params=_cparams(("parallel", "parallel", "arbitrary")),
        name=f"dilated_d{d}")(hc, hc, hc, hc, hc, hc, hc)


def _token_order(ref, stage):
    d, n, w = ref.shape
    if d == 1:
        return ref[0].astype(f32)
    nt = w // LANES
    for r in range(d):
        v = ref[r].astype(f32)
        for t in range(nt):
            stage[t, pl.ds(r, n, stride=d), :] = v[:, t * LANES:(t + 1) * LANES]
    return jnp.concatenate([stage[t] for t in range(nt)], axis=1)


def _merge_kernel(x_ref, oa_ref, ob_ref, o1_ref, o2_ref, o3_ref, l1_ref, l2_ref, l3_ref, g_ref,
                  wa_ref, wb_ref, wc_ref, wo_ref, lg_ref, lb_ref, rw_ref, rb_ref, x32_ref, xb_ref, gate_ref,
                  so2, so3, sl2, sl3, *, alpha):
    l1, l2, l3 = _token_order(l1_ref, None), _token_order(l2_ref, sl2), _token_order(l3_ref, sl3)
    o1, o2, o3 = _token_order(o1_ref, None), _token_order(o2_ref, so2), _token_order(o3_ref, so3)
    mx = jnp.maximum(jnp.maximum(l1, l2), l3)
    e1, e2, e3 = jnp.exp2(l1 - mx), jnp.exp2(l2 - mx), jnp.exp2(l3 - mx)
    den = e1 + e2 + e3
    w1, w2, w3 = e1 / den, e2 / den, e3 / den
    tiles = []
    for h in range(DIL_HEADS):
        sl = slice(h * LANES, (h + 1) * LANES)
        oc = w1[:, h:h + 1] * o1[:, sl] + w2[:, h:h + 1] * o2[:, sl] + w3[:, h:h + 1] * o3[:, sl]
        tiles.append(oc.astype(bf16))
    oc = jnp.concatenate(tiles, axis=1)
    obf = ob_ref[...].astype(f32)
    half = LANES // 2
    ob = jnp.concatenate([obf[:, 2 * j * LANES:(2 * j + 1) * LANES]
                          + pltpu.roll(obf[:, (2 * j + 1) * LANES:(2 * j + 2) * LANES], half, 1)
                          for j in range(MLA_HEADS // 2)], axis=1).astype(bf16)
    D = D_MODEL
    y = (g_ref[:, 0:D].astype(f32) * _dot(oa_ref[...], wa_ref[...])
         + g_ref[:, D:2 * D].astype(f32) * _dot(ob, wb_ref[...])
         + g_ref[:, 2 * D:3 * D].astype(f32) * _dot(oc, wc_ref[...]))
    mix = _dot(y.astype(bf16), wo_ref[...])
    out = _layer_norm(alpha * x_ref[...] + mix, lg_ref[...], lb_ref[...])
    x32_ref[...] = out
    xb = out.astype(bf16)
    xb_ref[...] = xb
    gate_ref[...] = _route(_dot_nt(rw_ref[...], xb), rb_ref[...])


def _merge_call(x32, oa, ob, dil_o, dil_l, hg, wa, wb, wc, wo, lg, lb, rw_t, rb, alpha, B, S, tm=512):
    T, D = x32.shape
    nb = S // tm
    row = lambda a: pl.BlockSpec((tm, a.shape[1]), lambda b, i: (b * nb + i, 0))
    cls = lambda a: pl.BlockSpec((None, a.shape[1], tm // a.shape[1], a.shape[3]), lambda b, i: (b, 0, i, 0))
    full = lambda a: pl.BlockSpec(a.shape, lambda b, i: (0, 0))
    consts = [wa, wb, wc, wo, lg, lb, rw_t, rb]
    ow, lw = dil_o[0].shape[3], dil_l[0].shape[3]
    return pl.pallas_call(
        functools.partial(_merge_kernel, alpha=alpha), grid=(B, nb),
        in_specs=[row(x32), row(oa), row(ob)] + [cls(a) for a in dil_o] + [cls(a) for a in dil_l] + [row(hg)]
        + [full(a) for a in consts],
        out_specs=[pl.BlockSpec((tm, D), lambda b, i: (b * nb + i, 0))] * 2
        + [pl.BlockSpec((N_EXPERTS, tm), lambda b, i: (0, b * nb + i))],
        out_shape=[jax.ShapeDtypeStruct((T, D), f32), jax.ShapeDtypeStruct((T, D), bf16),
                   jax.ShapeDtypeStruct((N_EXPERTS, T), f32)],
        scratch_shapes=[pltpu.VMEM((ow // LANES, tm, LANES), f32), pltpu.VMEM((ow // LANES, tm, LANES), f32),
                        pltpu.VMEM((lw // LANES, tm, LANES), f32), pltpu.VMEM((lw // LANES, tm, LANES), f32)],
        compiler_params=_cparams(("parallel", "parallel")), name="merge")(
            x32, oa, ob, *dil_o, *dil_l, hg, *consts)


def _first_true(flags):
    out, seen = [], None
    for f in flags:
        out.append(f if seen is None else f & ~seen)
        seen = f if seen is None else seen | f
    return out


def _route(logits, bias):
    scores = jax.nn.sigmoid(logits)
    biased = scores + bias
    sc = [scores[e:e + 1, :] for e in range(N_EXPERTS)]
    bi = [biased[e:e + 1, :] for e in range(N_EXPERTS)]
    npg = EXPERTS_PER_GROUP
    gscore = []
    for g in range(N_EXPERT_GROUPS):
        v = bi[g * npg:(g + 1) * npg]
        best = None
        for a in range(npg):
            for b in range(a + 1, npg):
                best = v[a] + v[b] if best is None else jnp.maximum(best, v[a] + v[b])
        gscore.append(best)
    gbest = functools.reduce(jnp.maximum, gscore)
    gsel = _first_true([gs == gbest for gs in gscore])
    zero = jnp.zeros_like(gbest)
    bsel = [functools.reduce(lambda x, y: x + y, [jnp.where(gsel[g], bi[g * npg + j], zero)
                                                  for g in range(N_EXPERT_GROUPS)]) for j in range(npg)]
    ssel = [functools.reduce(lambda x, y: x + y, [jnp.where(gsel[g], sc[g * npg + j], zero)
                                                  for g in range(N_EXPERT_GROUPS)]) for j in range(npg)]
    t1 = functools.reduce(jnp.maximum, bsel)
    i1 = _first_true([b == t1 for b in bsel])
    rest = [jnp.where(i1[j], -jnp.inf, bsel[j]) for j in range(npg)]
    t2 = functools.reduce(jnp.maximum, rest)
    i2 = _first_true([rest[j] == t2 for j in range(npg)])
    w1 = functools.reduce(lambda x, y: x + y, [jnp.where(i1[j], ssel[j], zero) for j in range(npg)])
    w2 = functools.reduce(lambda x, y: x + y, [jnp.where(i2[j], ssel[j], zero) for j in range(npg)])
    den = w1 + w2
    local = [jnp.where(i1[j], w1 / den, zero) + jnp.where(i2[j], w2 / den, zero) for j in range(npg)]
    rows = [jnp.where(gsel[e // npg], local[e % npg], zero) for e in range(N_EXPERTS)]
    return jnp.concatenate(rows, axis=0)


def _moe_kernel(x32_ref, xb_ref, gate_ref, wg_ref, wu_ref, wd_ref, wsg_ref, wsu_ref, wsd_ref, lg_ref, lb_ref,
                o32_ref, ob_ref, acc_sc, *, alpha, n_chunks, per_chunk):
    c = pl.program_id(1)
    x = xb_ref[...]

    @pl.when(c == 0)
    def _():
        hs = jax.nn.silu(_dot(x, wsg_ref[...])) * _dot(x, wsu_ref[...])
        acc_sc[...] = _dot(hs.astype(bf16), wsd_ref[...])

    gate = gate_ref[...]
    hid = []
    for e in range(per_chunk):
        h = jax.nn.silu(_dot(x, wg_ref[e])) * _dot(x, wu_ref[e]) * gate[:, e:e + 1]
        hid.append(h.astype(bf16))
    acc_sc[...] += _dot(jnp.concatenate(hid, axis=1), wd_ref[...])

    @pl.when(c == n_chunks - 1)
    def _():
        out = _layer_norm(alpha * x32_ref[...] + acc_sc[...], lg_ref[...], lb_ref[...])
        o32_ref[...] = out
        ob_ref[...] = out.astype(bf16)


def _moe_call(x32, xb, gate_c, wg, wu, wd, wsg, wsu, wsd, lg, lb, alpha, tm=1024, per_chunk=MOE_CHUNK):
    T, D = x32.shape
    n_chunks = N_EXPERTS // per_chunk
    cw = per_chunk * EXPERT_HIDDEN
    row = pl.BlockSpec((tm, D), lambda i, c: (i, 0))
    full = lambda a: pl.BlockSpec(a.shape, lambda i, c: (0, 0))
    return pl.pallas_call(
        functools.partial(_moe_kernel, alpha=alpha, n_chunks=n_chunks, per_chunk=per_chunk),
        grid=(T // tm, n_chunks),
        in_specs=[row, row, pl.BlockSpec((None, tm, per_chunk), lambda i, c: (c, i, 0)),
                  pl.BlockSpec((per_chunk, D, EXPERT_HIDDEN), lambda i, c: (c, 0, 0)),
                  pl.BlockSpec((per_chunk, D, EXPERT_HIDDEN), lambda i, c: (c, 0, 0)),
                  pl.BlockSpec((cw, D), lambda i, c: (c, 0)),
                  full(wsg), full(wsu), full(wsd), full(lg), full(lb)],
        out_specs=[row, row],
        out_shape=[jax.ShapeDtypeStruct((T, D), f32), jax.ShapeDtypeStruct((T, D), bf16)],
        scratch_shapes=[pltpu.VMEM((tm, D), f32)],
        compiler_params=_cparams(("parallel", "arbitrary")), name="moe")(
            x32, xb, gate_c, wg, wu, wd, wsg, wsu, wsd, lg, lb)


def _rope_tables(positions):
    pos = positions.astype(f32).reshape(-1, 1)
    lane = jnp.arange(LANES)

    def tables(inv_freq, idx, rotated, first_half):
        freq = jnp.where(rotated, inv_freq[idx], 0.0)
        ang = pos * freq[None, :]
        sign = jnp.where(first_half, -1.0, 1.0).astype(f32)
        return jnp.cos(ang), jnp.sin(ang) * sign[None, :]

    half = ROT_DIM // 2
    hl = lane % HEAD_DIM
    inv_p = ROPE_THETA ** (-jnp.arange(0, ROT_DIM, 2, dtype=f32) / ROT_DIM)
    cos_p, sin_p = tables(inv_p, hl % half, hl < ROT_DIM, hl < half)
    half = MLA_ROPE_DIM // 2
    rl = lane - MLA_NOPE_DIM
    inv_m = ROPE_THETA ** (-jnp.arange(0, MLA_ROPE_DIM, 2, dtype=f32) / MLA_ROPE_DIM)
    cos_m, sin_m = tables(inv_m, jnp.clip(rl, 0, MLA_ROPE_DIM - 1) % half, (rl >= 0) & (rl < MLA_ROPE_DIM), rl < half)
    return cos_p, sin_p, cos_m, sin_m


def _pack_w_in(w):
    offs = [0]
    for sz in IN_SIZES:
        offs.append(offs[-1] + sz)
    a_q, a_k, a_v, b_cq, b_ckv, b_kr, c_q, c_k, c_v, g = [w[..., offs[i]:offs[i + 1]] for i in range(len(IN_SIZES))]
    z = lambda n: jnp.zeros(w.shape[:-1] + (n,), w.dtype)
    kr_pad = jnp.concatenate([z(MLA_NOPE_DIM), b_kr, z(LANES - MLA_NOPE_DIM - MLA_ROPE_DIM)], axis=-1)
    cols = [a_q, a_k, a_v, b_ckv, kr_pad, b_cq]
    for gi in range(DIL_GROUPS):
        cols += [c_q[..., 256 * gi:256 * (gi + 1)], c_k[..., 256 * gi:256 * (gi + 1)], c_v[..., 512 * gi:512 * (gi + 1)]]
    cols.append(g)
    return jnp.concatenate(cols, axis=-1).astype(bf16)


def _pad_heads(w, n_heads, lo, hi):
    lead = w.shape[:-1]
    per = w.shape[-1] // n_heads
    wh = w.reshape(*lead, n_heads, per)[..., lo:hi]
    wh = jnp.pad(wh, [(0, 0)] * (len(lead) + 1) + [(0, LANES - (hi - lo))])
    return wh.reshape(*lead, n_heads * LANES).astype(bf16)


def kernel(x, positions, ln_in_g, ln_in_b, w_in, b_gate, lam_q1, lam_k1, lam_q2, lam_k2, diff_norm_g, mla_q_norm_g, mla_kv_norm_g, w_mla_qb, w_mla_kvb, w_branch_a, w_branch_b, w_branch_c, w_out, ln1_g, ln1_b, router_w, router_bias, w_exp_gate, w_exp_up, w_exp_down, w_sh_gate, w_sh_up, w_sh_down, ln2_g, ln2_b):
    B, S, D = x.shape
    T = B * S
    depth = w_in.shape[0]
    assert D == D_MODEL and S % (DIL_PATTERNS[-1][1] * 2 * DIL_SIDE) == 0 and S % FLASH_COLS == 0 and S % FLASH_TK == 0
    alpha = (2 * depth) ** 0.25
    cos_p, sin_p, cos_m, sin_m = _rope_tables(positions)
    rw_t = router_w.T.astype(bf16)
    rb = router_bias.reshape(N_EXPERTS, 1).astype(f32)
    vec = lambda v: v.reshape(1, -1).astype(f32)
    per_q = MLA_NOPE_DIM + MLA_ROPE_DIM
    per_kv = MLA_NOPE_DIM + MLA_V_DIM
    w_in_p = _pack_w_in(w_in)
    wq_p = _pad_heads(w_mla_qb, MLA_HEADS, 0, per_q)
    wk_p = _pad_heads(w_mla_kvb, MLA_HEADS, 0, MLA_NOPE_DIM)
    wv_p = _pad_heads(w_mla_kvb, MLA_HEADS, MLA_NOPE_DIM, per_kv)
    wa_b, wb_b, wc_b, wo_b = (w.astype(bf16) for w in (w_branch_a, w_branch_b, w_branch_c, w_out))
    wg_b, wu_b = w_exp_gate.astype(bf16), w_exp_up.astype(bf16)
    wd_b = w_exp_down.reshape(depth, N_EXPERTS * EXPERT_HIDDEN, D).astype(bf16)
    wsg_b, wsu_b, wsd_b = (w.astype(bf16) for w in (w_sh_gate, w_sh_up, w_sh_down))
    for l in range(depth):
        lam_init = 0.8 - 0.6 * math.exp(-0.3 * l)
        proj = _inproj_call(
            x.reshape(T, D) if l == 0 else xb, w_in_p[l], vec(b_gate[l]), cos_p, sin_p, cos_m, sin_m,
            vec(mla_q_norm_g[l]), vec(mla_kv_norm_g[l]), wq_p[l], wk_p[l], wv_p[l],
            B, S, ln=(vec(ln_in_g), vec(ln_in_b)) if l == 0 else None)
        ha, qb, kb, vb, hc0, hc1, hc2, hg = proj[:8]
        if l == 0:
            x32, xb = proj[8:]
        lam_vec = jnp.pad(jnp.stack([lam_q1[l], lam_k1[l], lam_q2[l], lam_k2[l]]).astype(f32),
                          ((0, 4), (0, LANES - HEAD_DIM)))
        ha3 = ha.reshape(B, S, HA_OUT_W)
        oa = _flash_call(ha3, ha3, ha3, DIFF_HEADS, 0, DIFF_HEADS, DIFF_HEADS, n_maps=2, tq=FLASH_COLS // 2, tk=FLASH_TK,
                         name="diff_attn", lam_vec=lam_vec, norm_g=diff_norm_g[l].reshape(-1, 1).astype(f32),
                         lam_init=lam_init, vw=2 * LANES)
        ob = _flash_call(qb.reshape(B, S, -1), kb.reshape(B, S, -1), vb.reshape(B, S, -1), MLA_HEADS, 0, 0, 0,
                         n_maps=1, tq=FLASH_COLS, tk=FLASH_TK, name="mla_attn", dv=MLA_V_DIM)
        dil = [_dil_call(hc) for hc in (hc0, hc1, hc2)]
        x32, xb, gate_t = _merge_call(
            x32, oa.reshape(T, -1), ob.reshape(T, -1), [o for o, _ in dil], [lse for _, lse in dil],
            hg, wa_b[l], wb_b[l], wc_b[l], wo_b[l], vec(ln1_g[l]), vec(ln1_b[l]), rw_t, rb, alpha, B, S)
        gate_c = gate_t.reshape(N_EXPERTS // MOE_CHUNK, MOE_CHUNK, T).transpose(0, 2, 1)
        x32, xb = _moe_call(x32, xb, gate_c, wg_b[l], wu_b[l], wd_b[l], wsg_b[l], wsu_b[l], wsd_b[l],
                            vec(ln2_g[l]), vec(ln2_b[l]), alpha)
    return x32.reshape(B, S, D)
```
